```python
import math
import jax, jax.numpy as jnp
from jax import lax
import numpy as np

D_MODEL = 1024
BATCH = 16
SEQ = 2048
DEPTH = 4
DEC_BATCH = 128
DEC_SEQ = 1
PAST_LEN = 8192
PAGE_SIZE = 128

N_ATTN_LAYERS = (DEPTH + 1) // 2
N_SSM_LAYERS = DEPTH // 2
RMS_EPS = 1e-6
MLA_HEADS = 16
QK_NOPE = 64
QK_ROPE = 32
V_DIM = 64
Q_RANK = 384
KV_RANK = 256
MLA_IN = Q_RANK + KV_RANK + QK_ROPE
ROPE_THETA = 10000.0
ATTN_SCALE = (QK_NOPE + QK_ROPE) ** -0.5
Q_BLOCK = 128
D_INNER = 2 * D_MODEL
SSM_HEAD_DIM = 64
SSM_HEADS = D_INNER // SSM_HEAD_DIM
SSM_GROUPS = 4
D_STATE = 128
CONV_W = 4
CONV_DIM = D_INNER + 2 * SSM_GROUPS * D_STATE
SSM_IN = D_INNER + CONV_DIM + SSM_HEADS
SSD_CHUNK = 128
DT_MIN = 0.001
DT_MAX = 0.1
D_FF = 2816
N_EXPERTS = 8
TOP_K = 2
D_FF_EXPERT = 2816

kernel_name = 'hybrid_mla_mamba2_moe_adaln_step'

F32 = jnp.float32


def rmsnorm(x, g):
    xf = x.astype(F32)
    xf = xf * lax.rsqrt(jnp.mean(xf * xf, axis=-1, keepdims=True) + RMS_EPS)
    return xf.astype(x.dtype) * g


def adaln(c, w, b):
    m = jax.nn.silu(c) @ w + b
    m = m.reshape(c.shape[0], 1, 6, D_MODEL)
    return [m[:, :, k] for k in range(6)]


def modulate(x, g, shift, scale):
    return rmsnorm(x, g) * (1 + scale) + shift


def rope(x, pos):
    half = x.shape[-1] // 2
    inv = ROPE_THETA ** (-jnp.arange(half, dtype=F32) / half)
    ang = pos.astype(F32)[:, None] * inv[None, :]
    cos = jnp.cos(ang)[None, :, None, :]
    sin = jnp.sin(ang)[None, :, None, :]
    xf = x.astype(F32)
    x1, x2 = xf[..., :half], xf[..., half:]
    return jnp.concatenate([x1 * cos - x2 * sin, x2 * cos + x1 * sin], axis=-1).astype(x.dtype)


def causal_attention(q, k, v, q_off):
    b, lq, h, dk = q.shape
    t, hk, dv = k.shape[1], k.shape[2], v.shape[-1]
    g = h // hk
    blk = min(Q_BLOCK, lq)
    nb = -(-lq // blk)
    pad = nb * blk - lq
    q = jnp.pad(q, ((0, 0), (0, pad), (0, 0), (0, 0)))
    qb = q.reshape(b, nb, blk, hk, g, dk).transpose(1, 0, 2, 3, 4, 5)
    qpos = (q_off + jnp.arange(nb * blk)).reshape(nb, blk)
    kpos = jnp.arange(t)

    def block(args):
        qi, pi = args
        s = jnp.einsum('bqkgd,btkd->bkgqt', qi, k).astype(F32) * ATTN_SCALE
        s = jnp.where(kpos[None, :] <= pi[:, None], s, -jnp.inf)
        p = jax.nn.softmax(s, axis=-1).astype(v.dtype)
        return jnp.einsum('bkgqt,btkd->bqkgd', p, v)

    o = lax.map(block, (qb, qpos))
    return o.transpose(1, 0, 2, 3, 4, 5).reshape(b, nb * blk, h, dv)[:, :lq]


def mla_project(h, q_off, w_in, q_g, kv_g, w_qb):
    b, l, _ = h.shape
    pos = q_off + jnp.arange(l)
    proj = h @ w_in
    cq = rmsnorm(proj[..., :Q_RANK], q_g)
    ckv = rmsnorm(proj[..., Q_RANK:Q_RANK + KV_RANK], kv_g)
    kr = rope(proj[..., Q_RANK + KV_RANK:][:, :, None, :], pos)[:, :, 0]
    q = (cq @ w_qb).reshape(b, l, MLA_HEADS, QK_NOPE + QK_ROPE)
    q_nope = q[..., :QK_NOPE]
    q_pe = rope(q[..., QK_NOPE:], pos)
    return q_nope, q_pe, ckv, kr


def mla_prompt(h, w_in, q_g, kv_g, w_qb, w_kvb, w_o):
    b, l, _ = h.shape
    q_nope, q_pe, ckv, kr = mla_project(h, 0, w_in, q_g, kv_g, w_qb)
    kv = jnp.einsum('btr,rhd->bthd', ckv, w_kvb.reshape(KV_RANK, MLA_HEADS, QK_NOPE + V_DIM))
    k = jnp.concatenate([kv[..., :QK_NOPE], jnp.broadcast_to(kr[:, :, None, :], (b, l, MLA_HEADS, QK_ROPE))], axis=-1)
    q = jnp.concatenate([q_nope, q_pe], axis=-1)
    o = causal_attention(q, k, kv[..., QK_NOPE:], 0)
    return o.reshape(b, l, MLA_HEADS * V_DIM) @ w_o, ckv, kr


def mla_sample(h, past_lat, past_rope, w_in, q_g, kv_g, w_qb, w_kvb, w_o):
    b, l, _ = h.shape
    t0 = past_lat.shape[1]
    q_nope, q_pe, ckv, kr = mla_project(h, t0, w_in, q_g, kv_g, w_qb)
    w = w_kvb.reshape(KV_RANK, MLA_HEADS, QK_NOPE + V_DIM)
    q_lat = jnp.einsum('bshd,rhd->bshr', q_nope, w[..., :QK_NOPE])
    k_lat = jnp.concatenate([past_lat, ckv], axis=1)
    k_pe = jnp.concatenate([past_rope, kr], axis=1)
    q = jnp.concatenate([q_lat, q_pe], axis=-1)
    k = jnp.concatenate([k_lat, k_pe], axis=-1)[:, :, None, :]
    o_lat = causal_attention(q, k, k_lat[:, :, None, :], t0)
    o = jnp.einsum('bshr,rhd->bshd', o_lat, w[..., QK_NOPE:])
    return o.reshape(b, l, MLA_HEADS * V_DIM) @ w_o, ckv, kr


def gather_pages(pool, page_table):
    return pool[page_table].reshape(page_table.shape[0], -1, pool.shape[-1])


def causal_conv(u, buf, w, b):
    up = jnp.concatenate([buf, u], axis=1)
    l = u.shape[1]
    y = sum(up[:, k:k + l] * w[k] for k in range(CONV_W)) + b
    return y, up[:, -(CONV_W - 1):]


def ssd(x, dt, a_neg, bm, cm, h0):
    b, l, nh, p = x.shape
    g, n = bm.shape[2], bm.shape[3]
    hg = nh // g
    cl = min(SSD_CHUNK, l)
    nc = -(-l // cl)
    pad = nc * cl - l

    def padl(u):
        return jnp.pad(u.astype(F32), [(0, 0), (0, pad)] + [(0, 0)] * (u.ndim - 2))

    xf = padl(x).reshape(b, nc, cl, g, hg, p)
    dtf = padl(dt).reshape(b, nc, cl, g, hg)
    bf = padl(bm).reshape(b, nc, cl, g, n)
    cf = padl(cm).reshape(b, nc, cl, g, n)
    cum = jnp.cumsum(jnp.moveaxis(dtf * a_neg.reshape(g, hg), 2, -1), axis=-1)
    causal = jnp.tril(jnp.ones((cl, cl), dtype=bool))
    seg = jnp.exp(jnp.where(causal, cum[..., :, None] - cum[..., None, :], -jnp.inf))
    xdt = xf * dtf[..., None]
    cb = jnp.einsum('bclgn,bcsgn->bcgls', cf, bf)
    y_diag = jnp.einsum('bcghls,bcsghp->bclghp', cb[:, :, :, None] * seg, xdt)
    decay_out = jnp.exp(cum[..., -1:] - cum)
    chunk_states = jnp.einsum('bcsgn,bcghs,bcsghp->bcghpn', bf, decay_out, xdt)
    chunk_decay = jnp.exp(cum[..., -1])

    def step(hc, inp):
        s_c, d_c = inp
        return hc * d_c[..., None, None] + s_c, hc

    h_last, h_in = lax.scan(step, h0.astype(F32).reshape(b, g, hg, p, n),
                            (jnp.moveaxis(chunk_states, 1, 0), jnp.moveaxis(chunk_decay, 1, 0)))
    h_in = jnp.moveaxis(h_in, 0, 1)
    y_off = jnp.einsum('bclgn,bcghpn,bcghl->bclghp', cf, h_in, jnp.exp(cum))
    y = (y_diag + y_off).reshape(b, nc * cl, nh, p)[:, :l]
    return y.astype(x.dtype), h_last.reshape(b, nh, p, n).astype(h0.dtype)


def mamba2(h, conv_buf, ssm_h, w_in, conv_w, conv_b, dt_bias, a_log, d_skip, norm_g, w_out):
    b, l, _ = h.shape
    proj = h @ w_in
    z = proj[..., :D_INNER]
    xbc, conv_new = causal_conv(proj[..., D_INNER:D_INNER + CONV_DIM], conv_buf, conv_w, conv_b)
    xbc = jax.nn.silu(xbc)
    xs = xbc[..., :D_INNER].reshape(b, l, SSM_HEADS, SSM_HEAD_DIM)
    bm = xbc[..., D_INNER:D_INNER + SSM_GROUPS * D_STATE].reshape(b, l, SSM_GROUPS, D_STATE)
    cm = xbc[..., D_INNER + SSM_GROUPS * D_STATE:].reshape(b, l, SSM_GROUPS, D_STATE)
    dt = jax.nn.softplus((proj[..., D_INNER + CONV_DIM:] + dt_bias).astype(F32))
    a_neg = -jnp.exp(a_log.astype(F32))
    y, h_new = ssd(xs, dt, a_neg, bm, cm, ssm_h)
    y = y + d_skip[:, None] * xs
    y = rmsnorm(y.reshape(b, l, D_INNER) * jax.nn.silu(z), norm_g)
    return y @ w_out, conv_new, h_new


def swiglu(h, wg, wu, wd):
    return (jax.nn.silu(h @ wg) * (h @ wu)) @ wd


def moe(h, w_router, w_gate, w_up, w_down):
    probs = jax.nn.softmax((h @ w_router).astype(F32), axis=-1)
    top_p, top_i = lax.top_k(probs, TOP_K)
    top_p = top_p / jnp.sum(top_p, axis=-1, keepdims=True)
    combine = jnp.sum(jax.nn.one_hot(top_i, N_EXPERTS, dtype=F32) * top_p[..., None], axis=-2).astype(h.dtype)
    y = 0
    for e in range(N_EXPERTS):
        y = y + combine[..., e:e + 1] * swiglu(h, w_gate[e], w_up[e], w_down[e])
    return y


def setup_inputs(seed: int = 0) -> dict:
    key = jax.random.key(seed)
    ks = iter(jax.random.split(key, 64))

    def nrm(shape, scale):
        return jax.random.normal(next(ks), shape, F32) * scale

    def gain(shape):
        return 1.0 + nrm(shape, 0.02)

    n_pages = PAST_LEN // PAGE_SIZE
    n_used = DEC_BATCH * n_pages
    n_pool = n_used + max(1, n_used // 4)
    x_prompt = nrm((BATCH, SEQ, D_MODEL), 1.0)
    x_sample = nrm((DEC_BATCH, DEC_SEQ, D_MODEL), 1.0)
    c_prompt = nrm((BATCH, D_MODEL), 1.0)
    c_sample = nrm((DEC_BATCH, D_MODEL), 1.0)
    cache_kv_latent = nrm((N_ATTN_LAYERS, n_pool, PAGE_SIZE, KV_RANK), 1.0)
    cache_k_rope = nrm((N_ATTN_LAYERS, n_pool, PAGE_SIZE, QK_ROPE), 1.0)
    page_table = jax.random.permutation(next(ks), n_pool)[:n_used].reshape(DEC_BATCH, n_pages).astype(jnp.int32)
    state_ssm = nrm((N_SSM_LAYERS, DEC_BATCH, SSM_HEADS, SSM_HEAD_DIM, D_STATE), 0.5)
    state_conv = nrm((N_SSM_LAYERS, DEC_BATCH, CONV_W - 1, CONV_DIM), 1.0)
    dt0 = jnp.exp(jax.random.uniform(next(ks), (N_SSM_LAYERS, SSM_HEADS), F32, math.log(DT_MIN), math.log(DT_MAX)))
    dt_bias = dt0 + jnp.log(-jnp.expm1(-dt0))
    a_log = jnp.log(jax.random.uniform(next(ks), (N_SSM_LAYERS, SSM_HEADS), F32, 1.0, 16.0))
    return {
        'x_prompt': x_prompt,
        'x_sample': x_sample,
        'c_prompt': c_prompt,
        'c_sample': c_sample,
        'cache_kv_latent': cache_kv_latent,
        'cache_k_rope': cache_k_rope,
        'page_table': page_table,
        'state_ssm': state_ssm,
        'state_conv': state_conv,
        'w_ada': nrm((DEPTH, D_MODEL, 6 * D_MODEL), 0.5 * D_MODEL ** -0.5),
        'b_ada': nrm((DEPTH, 6 * D_MODEL), 0.02),
        'norm_mix_g': gain((DEPTH, D_MODEL)),
        'norm_ffn_g': gain((DEPTH, D_MODEL)),
        'norm_final_g': gain((D_MODEL,)),
        'mla_w_in': nrm((N_ATTN_LAYERS, D_MODEL, MLA_IN), D_MODEL ** -0.5),
        'mla_q_norm_g': gain((N_ATTN_LAYERS, Q_RANK)),
        'mla_kv_norm_g': gain((N_ATTN_LAYERS, KV_RANK)),
        'mla_w_qb': nrm((N_ATTN_LAYERS, Q_RANK, MLA_HEADS * (QK_NOPE + QK_ROPE)), Q_RANK ** -0.5),
        'mla_w_kvb': nrm((N_ATTN_LAYERS, KV_RANK, MLA_HEADS * (QK_NOPE + V_DIM)), KV_RANK ** -0.5),
        'mla_w_o': nrm((N_ATTN_LAYERS, MLA_HEADS * V_DIM, D_MODEL), (MLA_HEADS * V_DIM) ** -0.5),
        'ssm_w_in': nrm((N_SSM_LAYERS, D_MODEL, SSM_IN), D_MODEL ** -0.5),
        'ssm_conv_w': nrm((N_SSM_LAYERS, CONV_W, CONV_DIM), CONV_W ** -0.5),
        'ssm_conv_b': nrm((N_SSM_LAYERS, CONV_DIM), 0.02),
        'ssm_dt_bias': dt_bias,
        'ssm_a_log': a_log,
        'ssm_d': 1.0 + nrm((N_SSM_LAYERS, SSM_HEADS), 0.1),
        'ssm_norm_g': gain((N_SSM_LAYERS, D_INNER)),
        'ssm_w_out': nrm((N_SSM_LAYERS, D_INNER, D_MODEL), D_INNER ** -0.5),
        'ffn_w_gate': nrm((N_ATTN_LAYERS, D_MODEL, D_FF), D_MODEL ** -0.5),
        'ffn_w_up': nrm((N_ATTN_LAYERS, D_MODEL, D_FF), D_MODEL ** -0.5),
        'ffn_w_down': nrm((N_ATTN_LAYERS, D_FF, D_MODEL), D_FF ** -0.5),
        'moe_w_router': nrm((N_SSM_LAYERS, D_MODEL, N_EXPERTS), D_MODEL ** -0.5),
        'moe_w_gate': nrm((N_SSM_LAYERS, N_EXPERTS, D_MODEL, D_FF_EXPERT), D_MODEL ** -0.5),
        'moe_w_up': nrm((N_SSM_LAYERS, N_EXPERTS, D_MODEL, D_FF_EXPERT), D_MODEL ** -0.5),
        'moe_w_down': nrm((N_SSM_LAYERS, N_EXPERTS, D_FF_EXPERT, D_MODEL), D_FF_EXPERT ** -0.5),
    }


def reference(x_prompt, x_sample, c_prompt, c_sample, cache_kv_latent, cache_k_rope, page_table,
              state_ssm, state_conv, w_ada, b_ada, norm_mix_g, norm_ffn_g, norm_final_g,
              mla_w_in, mla_q_norm_g, mla_kv_norm_g, mla_w_qb, mla_w_kvb, mla_w_o,
              ssm_w_in, ssm_conv_w, ssm_conv_b, ssm_dt_bias, ssm_a_log, ssm_d, ssm_norm_g, ssm_w_out,
              ffn_w_gate, ffn_w_up, ffn_w_down, moe_w_router, moe_w_gate, moe_w_up, moe_w_down):
    xp, xs = x_prompt, x_sample
    lat_p, rope_p, lat_s, rope_s = [], [], [], []
    ssm_p, conv_p, ssm_s, conv_s = [], [], [], []
    for i in range(DEPTH):
        j = i // 2
        mp = adaln(c_prompt, w_ada[i], b_ada[i])
        ms = adaln(c_sample, w_ada[i], b_ada[i])
        hp = modulate(xp, norm_mix_g[i], mp[0], mp[1])
        hs = modulate(xs, norm_mix_g[i], ms[0], ms[1])
        if i % 2 == 0:
            mla_w = (mla_w_in[j], mla_q_norm_g[j], mla_kv_norm_g[j], mla_w_qb[j], mla_w_kvb[j], mla_w_o[j])
            op, ckv_p, kr_p = mla_prompt(hp, *mla_w)
            past_lat = gather_pages(cache_kv_latent[j], page_table)
            past_rope = gather_pages(cache_k_rope[j], page_table)
            os_, ckv_s, kr_s = mla_sample(hs, past_lat, past_rope, *mla_w)
            lat_p.append(ckv_p)
            rope_p.append(kr_p)
            lat_s.append(ckv_s)
            rope_s.append(kr_s)
        else:
            ssm_w = (ssm_w_in[j], ssm_conv_w[j], ssm_conv_b[j], ssm_dt_bias[j], ssm_a_log[j], ssm_d[j],
                     ssm_norm_g[j], ssm_w_out[j])
            zero_conv = jnp.zeros((xp.shape[0], CONV_W - 1, CONV_DIM), xp.dtype)
            zero_ssm = jnp.zeros((xp.shape[0], SSM_HEADS, SSM_HEAD_DIM, D_STATE), xp.dtype)
            op, cb_p, h_p = mamba2(hp, zero_conv, zero_ssm, *ssm_w)
            os_, cb_s, h_s = mamba2(hs, state_conv[j], state_ssm[j], *ssm_w)
            ssm_p.append(h_p)
            conv_p.append(cb_p)
            ssm_s.append(h_s)
            conv_s.append(cb_s)
        xp = xp + mp[2] * op
        xs = xs + ms[2] * os_
        hp = modulate(xp, norm_ffn_g[i], mp[3], mp[4])
        hs = modulate(xs, norm_ffn_g[i], ms[3], ms[4])
        if i % 2 == 0:
            fp = swiglu(hp, ffn_w_gate[j], ffn_w_up[j], ffn_w_down[j])
            fs = swiglu(hs, ffn_w_gate[j], ffn_w_up[j], ffn_w_down[j])
        else:
            fp = moe(hp, moe_w_router[j], moe_w_gate[j], moe_w_up[j], moe_w_down[j])
            fs = moe(hs, moe_w_router[j], moe_w_gate[j], moe_w_up[j], moe_w_down[j])
        xp = xp + mp[5] * fp
        xs = xs + ms[5] * fs
    y_prompt = rmsnorm(xp, norm_final_g)
    y_sample = rmsnorm(xs, norm_final_g)
    return (y_prompt, y_sample, jnp.stack(lat_p), jnp.stack(rope_p), jnp.stack(lat_s), jnp.stack(rope_s),
            jnp.stack(ssm_p), jnp.stack(conv_p), jnp.stack(ssm_s), jnp.stack(conv_s))
```

```python
import functools

import jax
import jax.numpy as jnp
from jax import lax
from jax.experimental import pallas as pl
from jax.experimental.pallas import tpu as pltpu

F32 = jnp.float32
BF16 = jnp.bfloat16

RMS_EPS = 1e-6
MLA_HEADS = 16
QK_NOPE = 64
QK_ROPE = 32
V_DIM = 64
ROPE_THETA = 10000.0
PAGE_SIZE = 128
SSM_HEAD_DIM = 64
SSM_GROUPS = 4
D_STATE = 128
CONV_W = 4
SSD_CHUNK = 128
N_EXPERTS = 8

LANE = 128
SUBLANE = 8
VMEM_LIMIT = 56 << 20

ROW_TILE = 256
FFN_ROW_TILE = 512
FFN_COL_TILE = 1408
ATTN_TILE = 512
DECODE_PAGES = 16
NEG = -1e30


def _params(sem):
    return pltpu.CompilerParams(dimension_semantics=sem, vmem_limit_bytes=VMEM_LIMIT)


def _dot(a, b):
    return jnp.dot(a, b, preferred_element_type=F32)


def _dot_nt(a, b):
    return lax.dot_general(a, b, (((1,), (1,)), ((), ())), preferred_element_type=F32)


def _dot_tn(a, b):
    return lax.dot_general(a, b, (((0,), (0,)), ((), ())), preferred_element_type=F32)


def _split3(v):
    hi = v.astype(BF16)
    r = v - hi.astype(F32)
    mid = r.astype(BF16)
    lo = (r - mid.astype(F32)).astype(BF16)
    return hi, mid, lo


def _silu(x):
    return x * jax.nn.sigmoid(x)


def _rms(x, g):
    return (x * lax.rsqrt(jnp.mean(x * x, axis=-1, keepdims=True) + RMS_EPS)) * g


def _modulate(x, g, shift, scale):
    return _rms(x, g) * (1.0 + scale) + shift


def _mod_specs(n_rows, tile, rows_per_seq, d):
    if rows_per_seq == 1:
        return pl.BlockSpec((1, tile, d), lambda i, *_: (i, 0, 0))
    tiles_per_seq = rows_per_seq // tile
    return pl.BlockSpec((1, 1, d), lambda i, *_: (i // tiles_per_seq, 0, 0))


def _mod_arr(m, rows_per_seq, tile):
    if rows_per_seq == 1:
        return m.reshape(m.shape[0] // tile, tile, m.shape[1])
    return m.reshape(m.shape[0], 1, m.shape[1])


def _ada_kernel(c_ref, w_ref, b_ref, o_ref):
    s = _silu(c_ref[...]).astype(BF16)
    o_ref[0] = _dot(s, w_ref[0].astype(BF16)) + b_ref[0]


def _ada(c_all, w_ada, b_ada):
    depth, d, d6 = w_ada.shape
    n = c_all.shape[0]
    return pl.pallas_call(
        _ada_kernel,
        grid=(depth, d6 // d),
        in_specs=[pl.BlockSpec((n, d), lambda i, k: (0, 0)),
                  pl.BlockSpec((1, d, d), lambda i, k: (i, 0, k)),
                  pl.BlockSpec((1, 1, d), lambda i, k: (i, 0, k))],
        out_specs=pl.BlockSpec((1, n, d), lambda i, k: (i, 0, k)),
        out_shape=jax.ShapeDtypeStruct((depth, n, d6), F32),
        compiler_params=_params(("arbitrary", "arbitrary")),
        name="adaln",
    )(c_all, w_ada, b_ada.reshape(depth, 1, d6))


def _mla_proj_kernel(x_ref, g_ref, sh_ref, sc_ref, ct_ref, st_ref, cts_ref, sts_ref, wcq_ref, wckv_ref,
                     wkra_ref, wkrb_ref, qg_ref, kvg_ref, wqa_ref, wqb_ref, *rest, heads, with_kv):
    if with_kv:
        wka_ref, wv_ref, ckv_ref, kr_ref, q_ref, k_ref, v_ref = rest
    else:
        ckv_ref, kr_ref, q_ref = rest
    h = _modulate(x_ref[...], g_ref[...], sh_ref[0], sc_ref[0]).astype(BF16)
    cq = _rms(_dot(h, wcq_ref[...]), qg_ref[...]).astype(BF16)
    ckv = _rms(_dot(h, wckv_ref[...]), kvg_ref[...])
    kr = _dot(h, wkra_ref[...]) * ct_ref[...] + _dot(h, wkrb_ref[...]) * st_ref[...]
    ckv_ref[...] = ckv
    kr_ref[...] = kr[:, :QK_ROPE]
    rep = lambda t: jnp.concatenate([t] * heads, axis=1)
    q = _dot(cq, wqa_ref[...]) * rep(cts_ref[...]) + _dot(cq, wqb_ref[...]) * rep(sts_ref[...])
    q_ref[...] = q.astype(BF16)
    if with_kv:
        ckvb = ckv.astype(BF16)
        k_ref[...] = (_dot(ckvb, wka_ref[...]) + rep(kr)).astype(BF16)
        v_ref[...] = _dot(ckvb, wv_ref[...]).astype(BF16)


def _rot_cols(w):
    half = w.shape[-1] // 2
    return jnp.concatenate([-w[..., half:], w[..., :half]], axis=-1)


def _prep_mla(w_in, w_qb, w_kvb):
    d = w_in.shape[0]
    q_rank = w_qb.shape[0]
    kv_rank = w_kvb.shape[0]
    hd = LANE
    pad = hd - QK_ROPE - QK_NOPE
    w_kr = w_in[:, q_rank + kv_rank:]
    zk = jnp.zeros((d, hd - QK_ROPE), F32)
    wq = w_qb.reshape(q_rank, MLA_HEADS, QK_NOPE + QK_ROPE)
    q_nope, q_pe = wq[..., :QK_NOPE], wq[..., QK_NOPE:]
    zq = jnp.zeros((q_rank, MLA_HEADS, pad), F32)
    wkv = w_kvb.reshape(kv_rank, MLA_HEADS, QK_NOPE + V_DIM)
    k_nope, v = wkv[..., :QK_NOPE], wkv[..., QK_NOPE:]
    return dict(
        wcq=w_in[:, :q_rank].astype(BF16),
        wckv=w_in[:, q_rank:q_rank + kv_rank].astype(BF16),
        wkra=jnp.concatenate([w_kr, zk], axis=1).astype(BF16),
        wkrb=jnp.concatenate([_rot_cols(w_kr), zk], axis=1).astype(BF16),
        wqa=jnp.concatenate([q_pe, q_nope, zq], axis=-1).reshape(q_rank, MLA_HEADS * hd).astype(BF16),
        wqb=jnp.concatenate([_rot_cols(q_pe), jnp.zeros((q_rank, MLA_HEADS, hd - QK_ROPE), F32)],
                            axis=-1).reshape(q_rank, MLA_HEADS * hd).astype(BF16),
        wka=jnp.concatenate([jnp.zeros((kv_rank, MLA_HEADS, QK_ROPE), F32), k_nope,
                             jnp.zeros((kv_rank, MLA_HEADS, pad), F32)],
                            axis=-1).reshape(kv_rank, MLA_HEADS * hd).astype(BF16),
        wv=v.reshape(kv_rank, MLA_HEADS * V_DIM).astype(BF16),
        wabs=jnp.concatenate([jnp.zeros((MLA_HEADS, QK_ROPE, kv_rank), F32), k_nope.transpose(1, 2, 0),
                              jnp.zeros((MLA_HEADS, pad, kv_rank), F32)], axis=1).astype(BF16),
        wvbd=_block_diag_pairs(v.transpose(1, 0, 2)).astype(BF16),
    )


def _block_diag_pairs(v):
    h, r, dv = v.shape
    v = v.reshape(h // 2, 2, r, dv)
    z = jnp.zeros((h // 2, r, dv), v.dtype)
    top = jnp.concatenate([v[:, 0], z], axis=-1)
    bot = jnp.concatenate([z, v[:, 1]], axis=-1)
    return jnp.concatenate([top, bot], axis=1)


def _rope_tables(pos, scale):
    half = QK_ROPE // 2
    inv = ROPE_THETA ** (-jnp.arange(half, dtype=F32) / half)
    ang = pos.astype(F32)[:, None] * inv[None, :]
    cos, sin = jnp.cos(ang), jnp.sin(ang)
    n = pos.shape[0]
    ct = jnp.concatenate([cos, cos, jnp.ones((n, LANE - QK_ROPE), F32)], axis=1)
    st = jnp.concatenate([sin, sin, jnp.zeros((n, LANE - QK_ROPE), F32)], axis=1)
    return ct, st, ct * scale, st * scale


def _mla_proj(x, g, shift, scale, tabs, qg, kvg, w, rows_per_seq, with_kv):
    t, d = x.shape
    tile = min(ROW_TILE, t)
    n_tab_tiles = tabs[0].shape[0] // tile
    heads = MLA_HEADS
    q_rank, kv_rank = w["wcq"].shape[1], w["wckv"].shape[1]
    full = lambda a: pl.BlockSpec(a.shape, lambda i: (0,) * a.ndim)
    row = lambda n: pl.BlockSpec((tile, n), lambda i: (i, 0))
    tab = pl.BlockSpec((tile, LANE), lambda i: (i % n_tab_tiles, 0))
    mod = _mod_specs(t, tile, rows_per_seq, d)
    ws = [w["wcq"], w["wckv"], w["wkra"], w["wkrb"], qg, kvg, w["wqa"], w["wqb"]]
    outs = [jax.ShapeDtypeStruct((t, kv_rank), F32), jax.ShapeDtypeStruct((t, QK_ROPE), F32),
            jax.ShapeDtypeStruct((t, heads * LANE), BF16)]
    out_specs = [row(kv_rank), row(QK_ROPE), row(heads * LANE)]
    if with_kv:
        ws += [w["wka"], w["wv"]]
        outs += [jax.ShapeDtypeStruct((t, heads * LANE), BF16), jax.ShapeDtypeStruct((t, heads * V_DIM), BF16)]
        out_specs += [row(heads * LANE), row(heads * V_DIM)]
    return pl.pallas_call(
        functools.partial(_mla_proj_kernel, heads=heads, with_kv=with_kv),
        grid=(t // tile,),
        in_specs=[row(d), full(g), mod, mod, tab, tab, tab, tab] + [full(a) for a in ws],
        out_specs=out_specs,
        out_shape=outs,
        compiler_params=_params(("arbitrary",)),
        name="mla_proj",
    )(x, g, _mod_arr(shift, rows_per_seq, tile), _mod_arr(scale, rows_per_seq, tile), *tabs, *ws)


def _flash_kernel(qi_tab, ki_tab, q_ref, k_ref, v_ref, o_ref, m_sc, l_sc, acc_sc, *, tile):
    t = pl.program_id(2)
    qi = qi_tab[t]
    ki = ki_tab[t]

    @pl.when(ki == 0)
    def _():
        m_sc[...] = jnp.full(m_sc.shape, NEG, F32)
        l_sc[...] = jnp.zeros(l_sc.shape, F32)
        acc_sc[...] = jnp.zeros(acc_sc.shape, F32)

    v = v_ref[0]
    row = lax.broadcasted_iota(jnp.int32, (tile, tile), 0) + qi * tile
    col = lax.broadcasted_iota(jnp.int32, (tile, tile), 1) + ki * tile
    visible = col <= row
    for h in range(2):
        q = q_ref[0, :, h * LANE:(h + 1) * LANE]
        k = k_ref[0, :, h * LANE:(h + 1) * LANE]
        s = jnp.where(visible, _dot_nt(q, k), NEG)
        m_prev = m_sc[h]
        m_new = jnp.maximum(m_prev, jnp.max(s, axis=1, keepdims=True))
        alpha = jnp.exp(m_prev - m_new)
        p = jnp.exp(s - m_new)
        l_sc[h] = alpha * l_sc[h] + jnp.sum(p, axis=1, keepdims=True)
        acc_sc[h] = alpha * acc_sc[h] + _dot(p.astype(BF16), v)
        m_sc[h] = m_new

    @pl.when(ki == qi)
    def _():
        lane = lax.broadcasted_iota(jnp.int32, (tile, LANE), 1)
        o = jnp.where(lane < V_DIM, acc_sc[0] / l_sc[0], acc_sc[1] / l_sc[1])
        o_ref[0] = o.astype(o_ref.dtype)


def _flash(q, k, v):
    b, l, _ = q.shape
    tile = min(ATTN_TILE, l)
    nq = l // tile
    pairs = [(i, j) for i in range(nq) for j in range(i + 1)]
    qi_tab = jnp.array([p[0] for p in pairs], jnp.int32)
    ki_tab = jnp.array([p[1] for p in pairs], jnp.int32)
    grid_spec = pltpu.PrefetchScalarGridSpec(
        num_scalar_prefetch=2,
        grid=(b, MLA_HEADS // 2, len(pairs)),
        in_specs=[pl.BlockSpec((1, tile, 2 * LANE), lambda b_, h, t, qt, kt: (b_, qt[t], h)),
                  pl.BlockSpec((1, tile, 2 * LANE), lambda b_, h, t, qt, kt: (b_, kt[t], h)),
                  pl.BlockSpec((1, tile, 2 * V_DIM), lambda b_, h, t, qt, kt: (b_, kt[t], h))],
        out_specs=pl.BlockSpec((1, tile, 2 * V_DIM), lambda b_, h, t, qt, kt: (b_, qt[t], h)),
        scratch_shapes=[pltpu.VMEM((2, tile, 1), F32), pltpu.VMEM((2, tile, 1), F32),
                        pltpu.VMEM((2, tile, 2 * V_DIM), F32)],
    )
    return pl.pallas_call(
        functools.partial(_flash_kernel, tile=tile),
        grid_spec=grid_spec,
        out_shape=jax.ShapeDtypeStruct((b, l, MLA_HEADS * V_DIM), BF16),
        compiler_params=_params(("arbitrary", "arbitrary", "arbitrary")),
        name="flash_attn",
    )(qi_tab, ki_tab, q, k, v)


def _headmat_kernel(a_ref, w_ref, o_ref):
    o_ref[...] = _dot(a_ref[...], w_ref[0]).astype(o_ref.dtype)


def _headmat(a, w, in_w, out_w):
    n, rows = w.shape[0], a.shape[0]
    return pl.pallas_call(
        _headmat_kernel,
        grid=(n,),
        in_specs=[pl.BlockSpec((rows, in_w), lambda i: (0, i)),
                  pl.BlockSpec((1, in_w, out_w), lambda i: (i, 0, 0))],
        out_specs=pl.BlockSpec((rows, out_w), lambda i: (0, i)),
        out_shape=jax.ShapeDtypeStruct((rows, n * out_w), BF16),
        compiler_params=_params(("arbitrary",)),
        name="head_matmul",
    )(a, w)


def _decode_kernel(pt_ref, qlat_ref, q_ref, ckv_ref, kr_ref, *rest, pages):
    lat_refs, rope_refs = rest[:pages], rest[pages:2 * pages]
    o_ref, m_sc, l_sc, acc_sc = rest[2 * pages:]
    c = pl.program_id(1)

    @pl.when(c == 0)
    def _():
        m_sc[...] = jnp.full(m_sc.shape, NEG, F32)
        l_sc[...] = jnp.zeros(l_sc.shape, F32)
        acc_sc[...] = jnp.zeros(acc_sc.shape, F32)

    ql = qlat_ref[0]
    qp = q_ref[0][:, :QK_ROPE]
    m, l, acc = m_sc[...], l_sc[...], acc_sc[...]
    for i in range(pages):
        lat = lat_refs[i][0, 0].astype(BF16)
        rp = rope_refs[i][0, 0].astype(BF16)
        s = _dot_nt(ql, lat) + _dot_nt(qp, rp)
        m_new = jnp.maximum(m, jnp.max(s, axis=1, keepdims=True))
        alpha = jnp.exp(m - m_new)
        p = jnp.exp(s - m_new)
        l = alpha * l + jnp.sum(p, axis=1, keepdims=True)
        acc = alpha * acc + _dot(p.astype(BF16), lat)
        m = m_new
    m_sc[...], l_sc[...], acc_sc[...] = m, l, acc

    @pl.when(c == pl.num_programs(1) - 1)
    def _():
        kl = ckv_ref[0].astype(BF16).astype(F32)
        kp = kr_ref[0].astype(BF16).astype(F32)
        s = (jnp.sum(ql.astype(F32) * kl, axis=1, keepdims=True)
             + jnp.sum(qp.astype(F32) * kp, axis=1, keepdims=True))
        m_new = jnp.maximum(m, s)
        alpha = jnp.exp(m - m_new)
        p = jnp.exp(s - m_new)
        lf = alpha * l + p
        accf = alpha * acc + p.astype(BF16).astype(F32) * kl
        o_ref[0] = (accf / lf).astype(o_ref.dtype)


def _decode_attn(layer, page_table, qlat, q, ckv, kr, cache_lat, cache_rope):
    b, heads, rank = qlat.shape
    n_pages = page_table.shape[1]
    pages = min(DECODE_PAGES, n_pages)
    lat_spec = lambda i: pl.BlockSpec(
        (1, 1, PAGE_SIZE, rank), lambda b_, c, pt: (layer, pt[b_ * n_pages + c * pages + i], 0, 0))
    rope_spec = lambda i: pl.BlockSpec(
        (1, 1, PAGE_SIZE, QK_ROPE), lambda b_, c, pt: (layer, pt[b_ * n_pages + c * pages + i], 0, 0))
    seq = lambda n, w: pl.BlockSpec((1, n, w), lambda b_, c, pt: (b_, 0, 0))
    grid_spec = pltpu.PrefetchScalarGridSpec(
        num_scalar_prefetch=1,
        grid=(b, n_pages // pages),
        in_specs=[seq(heads, rank), seq(heads, LANE), seq(1, rank), seq(1, QK_ROPE)]
        + [lat_spec(i) for i in range(pages)] + [rope_spec(i) for i in range(pages)],
        out_specs=seq(heads, rank),
        scratch_shapes=[pltpu.VMEM((heads, 1), F32), pltpu.VMEM((heads, 1), F32), pltpu.VMEM((heads, rank), F32)],
    )
    return pl.pallas_call(
        functools.partial(_decode_kernel, pages=pages),
        grid_spec=grid_spec,
        out_shape=jax.ShapeDtypeStruct((b, heads, rank), BF16),
        compiler_params=_params(("arbitrary", "arbitrary")),
        name="decode_attn",
    )(page_table.reshape(-1), qlat, q, ckv, kr, *([cache_lat] * pages), *([cache_rope] * pages))


def _proj_res_kernel(a_ref, w_ref, x_ref, gate_ref, o_ref):
    o_ref[...] = x_ref[...] + gate_ref[0] * _dot(a_ref[...], w_ref[...])


def _proj_res(a, w, x, gate, rows_per_seq):
    t, d = x.shape
    k = a.shape[1]
    tile = min(FFN_ROW_TILE, t)
    return pl.pallas_call(
        _proj_res_kernel,
        grid=(t // tile,),
        in_specs=[pl.BlockSpec((tile, k), lambda i: (i, 0)), pl.BlockSpec((k, d), lambda i: (0, 0)),
                  pl.BlockSpec((tile, d), lambda i: (i, 0)), _mod_specs(t, tile, rows_per_seq, d)],
        out_specs=pl.BlockSpec((tile, d), lambda i: (i, 0)),
        out_shape=jax.ShapeDtypeStruct((t, d), F32),
        compiler_params=_params(("arbitrary",)),
        name="proj_residual",
    )(a, w, x, _mod_arr(gate, rows_per_seq, tile))


def _ffn_kernel(*refs, routed):
    if routed:
        x_ref, g_ref, sh_ref, sc_ref, gate_ref, cw_ref, wg_ref, wu_ref, wd_ref, o_ref, h_sc, acc_sc = refs
    else:
        x_ref, g_ref, sh_ref, sc_ref, gate_ref, wg_ref, wu_ref, wd_ref, o_ref, h_sc, acc_sc = refs
    e, f = pl.program_id(1), pl.program_id(2)

    @pl.when((e == 0) & (f == 0))
    def _():
        h_sc[...] = _modulate(x_ref[...], g_ref[...], sh_ref[0], sc_ref[0]).astype(BF16)
        acc_sc[...] = jnp.zeros(acc_sc.shape, F32)

    h = h_sc[...]
    a = (_silu(_dot(h, wg_ref[0])) * _dot(h, wu_ref[0])).astype(BF16)
    y = _dot(a, wd_ref[0])
    if routed:
        cw = cw_ref[...]
        lane = lax.broadcasted_iota(jnp.int32, cw.shape, 1)
        y = y * jnp.sum(jnp.where(lane == e, cw, 0.0), axis=1, keepdims=True)
    acc_sc[...] += y

    @pl.when((e == pl.num_programs(1) - 1) & (f == pl.num_programs(2) - 1))
    def _():
        o_ref[...] = x_ref[...] + gate_ref[0] * acc_sc[...]


def _ffn(x, g, shift, scale, gate, wg, wu, wd, rows_per_seq, cw=None):
    t, d = x.shape
    n_exp, _, ff = wg.shape
    tile = min(FFN_ROW_TILE, t)
    tf = FFN_COL_TILE if ff % FFN_COL_TILE == 0 else ff
    mod = _mod_specs(t, tile, rows_per_seq, d)
    row = pl.BlockSpec((tile, d), lambda i, e, f: (i, 0))
    in_specs = [row, pl.BlockSpec(g.shape, lambda i, e, f: (0, 0)), mod, mod, mod]
    args = [x, g] + [_mod_arr(m, rows_per_seq, tile) for m in (shift, scale, gate)]
    if cw is not None:
        in_specs.append(pl.BlockSpec((tile, LANE), lambda i, e, f: (i, 0)))
        args.append(cw)
    in_specs += [pl.BlockSpec((1, d, tf), lambda i, e, f: (e, 0, f)),
                 pl.BlockSpec((1, d, tf), lambda i, e, f: (e, 0, f)),
                 pl.BlockSpec((1, tf, d), lambda i, e, f: (e, f, 0))]
    return pl.pallas_call(
        functools.partial(_ffn_kernel, routed=cw is not None),
        grid=(t // tile, n_exp, ff // tf),
        in_specs=in_specs,
        out_specs=row,
        out_shape=jax.ShapeDtypeStruct((t, d), F32),
        scratch_shapes=[pltpu.VMEM((tile, d), BF16), pltpu.VMEM((tile, d), F32)],
        compiler_params=_params(("arbitrary", "arbitrary", "arbitrary")),
        name="swiglu",
    )(*args, wg, wu, wd)


def _router_kernel(x_ref, g_ref, sh_ref, sc_ref, wr_ref, cw_ref):
    h = _modulate(x_ref[...], g_ref[...], sh_ref[0], sc_ref[0])
    hi = h.astype(BF16)
    lo = (h - hi.astype(F32)).astype(BF16)
    w = wr_ref[...]
    whi = w.astype(BF16)
    wlo = (w - whi.astype(F32)).astype(BF16)
    logits = _dot(hi, whi) + _dot(lo, whi) + _dot(hi, wlo)
    lane = lax.broadcasted_iota(jnp.int32, logits.shape, 1).astype(F32)
    logits = jnp.where(lane < N_EXPERTS, logits, NEG)
    p = jnp.exp(logits - jnp.max(logits, axis=1, keepdims=True))
    p = p / jnp.sum(p, axis=1, keepdims=True)
    p1 = jnp.max(p, axis=1, keepdims=True)
    i1 = jnp.min(jnp.where(p == p1, lane, float(LANE)), axis=1, keepdims=True)
    rest = jnp.where(lane == i1, -1.0, p)
    p2 = jnp.max(rest, axis=1, keepdims=True)
    i2 = jnp.min(jnp.where(rest == p2, lane, float(LANE)), axis=1, keepdims=True)
    den = p1 + p2
    cw_ref[...] = jnp.where(lane == i1, p1 / den, 0.0) + jnp.where(lane == i2, p2 / den, 0.0)


def _router(x, g, shift, scale, w_router, rows_per_seq):
    t, d = x.shape
    tile = min(FFN_ROW_TILE, t)
    wr = jnp.concatenate([w_router, jnp.zeros((d, LANE - w_router.shape[1]), F32)], axis=1)
    mod = _mod_specs(t, tile, rows_per_seq, d)
    return pl.pallas_call(
        _router_kernel,
        grid=(t // tile,),
        in_specs=[pl.BlockSpec((tile, d), lambda i: (i, 0)), pl.BlockSpec(g.shape, lambda i: (0, 0)), mod, mod,
                  pl.BlockSpec((d, LANE), lambda i: (0, 0))],
        out_specs=pl.BlockSpec((tile, LANE), lambda i: (i, 0)),
        out_shape=jax.ShapeDtypeStruct((t, LANE), F32),
        compiler_params=_params(("arbitrary",)),
        name="router",
    )(x, g, _mod_arr(shift, rows_per_seq, tile), _mod_arr(scale, rows_per_seq, tile), wr)


def _softplus(x):
    return jnp.maximum(x, 0.0) + jnp.log1p(jnp.exp(-jnp.abs(x)))


def _ssm_in_kernel(x_ref, g_ref, sh_ref, sc_ref, wz_ref, wx_ref, wdt_ref, cw_ref, cb_ref, dtb_ref,
                   z_ref, xs_ref, b_ref, c_ref, dt_ref, conv_ref, ext_sc, *, tile, tiles_per_seq, d_inner, n_heads):
    i = pl.program_id(0)

    @pl.when(i % tiles_per_seq == 0)
    def _():
        ext_sc[0:SUBLANE, :] = jnp.zeros((SUBLANE, ext_sc.shape[1]), F32)

    h = _modulate(x_ref[...], g_ref[...], sh_ref[0], sc_ref[0]).astype(BF16)
    z_ref[...] = _dot(h, wz_ref[...])
    u = _dot(h, wx_ref[...])
    ext_sc[SUBLANE:SUBLANE + tile, :] = u
    cw = cw_ref[...]
    y = cw[0:1] * ext_sc[SUBLANE - 3:SUBLANE - 3 + tile, :]
    y = y + cw[1:2] * ext_sc[SUBLANE - 2:SUBLANE - 2 + tile, :]
    y = y + cw[2:3] * ext_sc[SUBLANE - 1:SUBLANE - 1 + tile, :]
    y = y + cw[3:4] * u + cb_ref[...]
    tail = ext_sc[tile:tile + SUBLANE, :]
    conv_ref[0] = tail
    ext_sc[0:SUBLANE, :] = tail
    xbc = _silu(y)
    gn = (xbc.shape[1] - d_inner) // 2
    xs_ref[...] = xbc[:, :d_inner]
    b_ref[...] = xbc[:, d_inner:d_inner + gn].astype(BF16)
    c_ref[...] = xbc[:, d_inner + gn:].astype(BF16)
    dt = _softplus(_dot(h, wdt_ref[...]) + dtb_ref[...])
    lane = lax.broadcasted_iota(jnp.int32, dt.shape, 1)
    dt_ref[...] = jnp.where(lane < n_heads, dt, 0.0)


def _ssm_in(x, g, shift, scale, w, rows_per_seq):
    t, d = x.shape
    tile = min(ROW_TILE, rows_per_seq)
    d_inner, conv_dim = w["wz"].shape[1], w["wx"].shape[1]
    gn = (conv_dim - d_inner) // 2
    n_seq = t // rows_per_seq
    tiles_per_seq = rows_per_seq // tile
    full = lambda a: pl.BlockSpec(a.shape, lambda i: (0,) * a.ndim)
    row = lambda n: pl.BlockSpec((tile, n), lambda i: (i, 0))
    mod = _mod_specs(t, tile, rows_per_seq, d)
    ws = [w["wz"], w["wx"], w["wdt"], w["conv_w"], w["conv_b"], w["dt_bias"]]
    return pl.pallas_call(
        functools.partial(_ssm_in_kernel, tile=tile, tiles_per_seq=tiles_per_seq, d_inner=d_inner,
                          n_heads=d_inner // SSM_HEAD_DIM),
        grid=(t // tile,),
        in_specs=[row(d), full(g), mod, mod] + [full(a) for a in ws],
        out_specs=[row(d_inner), row(d_inner), row(gn), row(gn), row(LANE),
                   pl.BlockSpec((1, SUBLANE, conv_dim), lambda i: (i // tiles_per_seq, 0, 0))],
        out_shape=[jax.ShapeDtypeStruct((t, d_inner), F32), jax.ShapeDtypeStruct((t, d_inner), F32),
                   jax.ShapeDtypeStruct((t, gn), BF16), jax.ShapeDtypeStruct((t, gn), BF16),
                   jax.ShapeDtypeStruct((t, LANE), F32), jax.ShapeDtypeStruct((n_seq, SUBLANE, conv_dim), F32)],
        scratch_shapes=[pltpu.VMEM((tile + SUBLANE, conv_dim), F32)],
        compiler_params=_params(("arbitrary",)),
        name="ssm_in_conv",
    )(x, g, _mod_arr(shift, rows_per_seq, tile), _mod_arr(scale, rows_per_seq, tile), *ws)


def _ssd_kernel(xs_ref, b_ref, c_ref, dt_ref, z_ref, alog_ref, e_ref, dskip_ref, ng_ref, y_ref, hout_ref,
                st_sc, y_sc, *, cl, n_groups, pairs_per_group):
    c = pl.program_id(1)

    @pl.when(c == 0)
    def _():
        st_sc[...] = jnp.zeros(st_sc.shape, F32)

    dt = dt_ref[0]
    d_a = dt * (-jnp.exp(alog_ref[...]))
    rowi = lax.broadcasted_iota(jnp.int32, (cl, cl), 0)
    coli = lax.broadcasted_iota(jnp.int32, (cl, cl), 1)
    causal = rowi >= coli
    ltri = jnp.where(causal, 1.0, 0.0).astype(BF16)
    cum = sum(_dot(ltri, part) for part in _split3(d_a))
    cum_t = cum.T
    expand = e_ref[...]
    dt_x = sum(_dot(part, expand) for part in _split3(dt))
    cum_x = sum(_dot(part, expand) for part in _split3(cum))
    xs = xs_ref[0]
    xdt = xs * dt_x
    xdt_b = xdt.astype(BF16)
    w_out = (xdt * jnp.exp(cum_x[cl - 1:cl, :] - cum_x)).astype(BF16)
    grow = jnp.exp(cum_x)
    lane = lax.broadcasted_iota(jnp.int32, (cl, LANE), 1)
    first = lax.broadcasted_iota(jnp.int32, (LANE, 1), 0) < SSM_HEAD_DIM
    for g in range(n_groups):
        bg = b_ref[0, :, g * D_STATE:(g + 1) * D_STATE]
        cg = c_ref[0, :, g * D_STATE:(g + 1) * D_STATE]
        cb = _dot_nt(cg, bg)
        for j in range(pairs_per_group):
            pr = g * pairs_per_group + j
            h0, h1 = 2 * pr, 2 * pr + 1
            sl = slice(pr * LANE, (pr + 1) * LANE)
            seg0 = jnp.exp(jnp.where(causal, cum[:, h0:h0 + 1] - cum_t[h0:h0 + 1, :], -jnp.inf))
            seg1 = jnp.exp(jnp.where(causal, cum[:, h1:h1 + 1] - cum_t[h1:h1 + 1, :], -jnp.inf))
            xp = xdt_b[:, sl]
            y_diag = jnp.where(lane < SSM_HEAD_DIM, _dot((cb * seg0).astype(BF16), xp),
                               _dot((cb * seg1).astype(BF16), xp))
            st = st_sc[pr]
            y_sc[:, sl] = y_diag + _dot_nt(cg, st.astype(BF16)) * grow[:, sl]
            decay = jnp.exp(jnp.where(first, cum_t[h0:h0 + 1, cl - 1:cl], cum_t[h1:h1 + 1, cl - 1:cl]))
            st_sc[pr] = st * decay + _dot_tn(w_out[:, sl], bg)
    zz = z_ref[0]
    y = (y_sc[...] + dskip_ref[...] * xs) * _silu(zz)
    y_ref[0] = _rms(y, ng_ref[...]).astype(y_ref.dtype)

    @pl.when(c == pl.num_programs(1) - 1)
    def _():
        hout_ref[0] = st_sc[...]


def _ssd(xs, bm, cm, dt, z, w, n_seq, seq_len):
    d_inner = xs.shape[1]
    gn = bm.shape[1]
    n_pairs = d_inner // LANE
    cl = min(SSD_CHUNK, seq_len)
    nc = seq_len // cl
    r3 = lambda a: a.reshape(n_seq, seq_len, a.shape[1])
    blk = lambda n: pl.BlockSpec((1, cl, n), lambda b, c: (b, c, 0))
    full = lambda a: pl.BlockSpec(a.shape, lambda b, c: (0,) * a.ndim)
    consts = [w["a_log"], w["expand"], w["d_skip"], w["norm_g"]]
    y, h_out = pl.pallas_call(
        functools.partial(_ssd_kernel, cl=cl, n_groups=SSM_GROUPS, pairs_per_group=n_pairs // SSM_GROUPS),
        grid=(n_seq, nc),
        in_specs=[blk(d_inner), blk(gn), blk(gn), blk(LANE), blk(d_inner)] + [full(a) for a in consts],
        out_specs=[blk(d_inner), pl.BlockSpec((1, n_pairs, LANE, D_STATE), lambda b, c: (b, 0, 0, 0))],
        out_shape=[jax.ShapeDtypeStruct((n_seq, seq_len, d_inner), BF16),
                   jax.ShapeDtypeStruct((n_seq, n_pairs, LANE, D_STATE), F32)],
        scratch_shapes=[pltpu.VMEM((n_pairs, LANE, D_STATE), F32), pltpu.VMEM((cl, d_inner), F32)],
        compiler_params=_params(("arbitrary", "arbitrary")),
        name="ssd_scan",
    )(r3(xs), r3(bm), r3(cm), r3(dt), r3(z), *consts)
    return y.reshape(n_seq * seq_len, d_inner), h_out


def _prep_ssm(w_in, conv_w, conv_b, dt_bias, a_log, d_skip, norm_g, w_out):
    d = w_in.shape[0]
    d_inner = w_out.shape[0]
    n_heads = d_inner // SSM_HEAD_DIM
    conv_dim = conv_w.shape[1]
    pad = LANE - n_heads
    w_dt = w_in[:, d_inner + conv_dim:]
    rep = lambda v: jnp.repeat(v, SSM_HEAD_DIM, axis=-1)
    head_of = jnp.arange(d_inner) // SSM_HEAD_DIM
    return dict(
        wz=w_in[:, :d_inner].astype(BF16),
        wx=w_in[:, d_inner:d_inner + conv_dim].astype(BF16),
        wdt=jnp.concatenate([w_dt, jnp.zeros((d, pad), F32)], axis=1).astype(BF16),
        wdt_x=rep(w_dt).astype(BF16),
        conv_w=conv_w, conv_b=conv_b.reshape(1, conv_dim),
        dt_bias=jnp.concatenate([dt_bias, jnp.zeros((pad,), F32)]).reshape(1, LANE),
        dt_bias_x=rep(dt_bias).reshape(1, d_inner),
        a_log=jnp.concatenate([a_log, jnp.zeros((pad,), F32)]).reshape(1, LANE),
        a_log_x=rep(a_log).reshape(1, d_inner),
        expand=(jnp.arange(LANE)[:, None] == head_of[None, :]).astype(BF16),
        d_skip=rep(d_skip).reshape(1, d_inner),
        norm_g=norm_g.reshape(1, d_inner),
        w_out=w_out.astype(BF16),
    )


def _ssm_step_in_kernel(x_ref, g_ref, sh_ref, sc_ref, wz_ref, wx_ref, wdt_ref, cw_ref, cb_ref, dtb_ref, alog_ref,
                        buf_ref, z_ref, xs_ref, b_ref, c_ref, conv_ref, xdt_t_ref, dec_t_ref, *, d_inner):
    h = _modulate(x_ref[...], g_ref[...], sh_ref[0], sc_ref[0]).astype(BF16)
    z_ref[...] = _dot(h, wz_ref[...])
    u = _dot(h, wx_ref[...])
    cw = cw_ref[...]
    y = cw[0:1] * buf_ref[0] + cw[1:2] * buf_ref[1] + cw[2:3] * buf_ref[2] + cw[3:4] * u + cb_ref[...]
    conv_ref[0] = buf_ref[1]
    conv_ref[1] = buf_ref[2]
    conv_ref[2] = u
    xbc = _silu(y)
    gn = (xbc.shape[1] - d_inner) // 2
    xs = xbc[:, :d_inner]
    xs_ref[...] = xs
    b_ref[...] = xbc[:, d_inner:d_inner + gn]
    c_ref[...] = xbc[:, d_inner + gn:]
    dt = _softplus(_dot(h, wdt_ref[...]) + dtb_ref[...])
    xdt_t_ref[...] = (xs * dt).T
    dec_t_ref[...] = jnp.exp(dt * (-jnp.exp(alog_ref[...]))).T


def _ssm_step_in(x, g, shift, scale, buf_t, w):
    n, d = x.shape
    d_inner, conv_dim = w["wz"].shape[1], w["wx"].shape[1]
    gn = (conv_dim - d_inner) // 2
    ws = [w["wz"], w["wx"], w["wdt_x"], w["conv_w"], w["conv_b"], w["dt_bias_x"], w["a_log_x"], buf_t]
    sds = jax.ShapeDtypeStruct
    return pl.pallas_call(
        functools.partial(_ssm_step_in_kernel, d_inner=d_inner),
        out_shape=[sds((n, d_inner), F32), sds((n, d_inner), F32), sds((n, gn), F32), sds((n, gn), F32),
                   sds((CONV_W - 1, n, conv_dim), F32), sds((d_inner, n), F32), sds((d_inner, n), F32)],
        compiler_params=pltpu.CompilerParams(vmem_limit_bytes=VMEM_LIMIT),
        name="ssm_step_in",
    )(x, g, shift.reshape(1, n, d), scale.reshape(1, n, d), *ws)


def _ssm_step_kernel(st_ref, xdt_t_ref, dec_t_ref, b_ref, c_ref, hout_ref, y_t_ref, *, n_groups):
    b = pl.program_id(0)

    @pl.when(b == 0)
    def _():
        y_t_ref[...] = jnp.zeros(y_t_ref.shape, F32)

    rg = st_ref.shape[2] // n_groups
    mine = lax.broadcasted_iota(jnp.int32, (rg, LANE), 1) == b
    for g in range(n_groups):
        rs = slice(g * rg, (g + 1) * rg)
        xcol = jnp.sum(jnp.where(mine, xdt_t_ref[rs, :], 0.0), axis=1, keepdims=True)
        dcol = jnp.sum(jnp.where(mine, dec_t_ref[rs, :], 0.0), axis=1, keepdims=True)
        bg = b_ref[0, :, g * D_STATE:(g + 1) * D_STATE]
        cg = c_ref[0, :, g * D_STATE:(g + 1) * D_STATE]
        s_new = st_ref[0, 0, rs, :] * dcol + xcol * bg
        hout_ref[0, rs, :] = s_new
        ycol = jnp.sum(s_new * cg, axis=1, keepdims=True)
        y_t_ref[rs, :] = jnp.where(mine, ycol, y_t_ref[rs, :])


def _ssm_step(layer, state, xdt_t, dec_t, bm, cm):
    _, n, rows, ns = state.shape
    full = lambda a: pl.BlockSpec(a.shape, lambda b: (0,) * a.ndim)
    seq_row = pl.BlockSpec((1, 1, bm.shape[1]), lambda b: (b, 0, 0))
    bm, cm = bm.reshape(n, 1, -1), cm.reshape(n, 1, -1)
    return pl.pallas_call(
        functools.partial(_ssm_step_kernel, n_groups=SSM_GROUPS),
        grid=(n,),
        in_specs=[pl.BlockSpec((1, 1, rows, ns), lambda b: (layer, b, 0, 0)), full(xdt_t), full(dec_t), seq_row,
                  seq_row],
        out_specs=[pl.BlockSpec((1, rows, ns), lambda b: (b, 0, 0)), pl.BlockSpec((rows, n), lambda b: (0, 0))],
        out_shape=[jax.ShapeDtypeStruct((n, rows, ns), F32), jax.ShapeDtypeStruct((rows, n), F32)],
        compiler_params=_params(("arbitrary",)),
        name="ssm_step",
    )(state, xdt_t, dec_t, bm, cm)


def _ssm_step_out_kernel(y_t_ref, xs_ref, z_ref, dskip_ref, ng_ref, w_ref, x_ref, gate_ref, o_ref):
    y = (y_t_ref[...].T + dskip_ref[...] * xs_ref[...]) * _silu(z_ref[...])
    y = _rms(y, ng_ref[...]).astype(BF16)
    o_ref[...] = x_ref[...] + gate_ref[0] * _dot(y, w_ref[...])


def _ssm_step_out(y_t, xs, z, w, x, gate):
    n, d = x.shape
    return pl.pallas_call(
        _ssm_step_out_kernel,
        out_shape=jax.ShapeDtypeStruct((n, d), F32),
        compiler_params=pltpu.CompilerParams(vmem_limit_bytes=VMEM_LIMIT),
        name="ssm_step_out",
    )(y_t, xs, z, w["d_skip"], w["norm_g"], w["w_out"], x, gate.reshape(1, n, d))


def _final_norm_kernel(x_ref, g_ref, o_ref):
    o_ref[...] = _rms(x_ref[...], g_ref[...])


def _final_norm(x, g):
    t, d = x.shape
    tile = min(FFN_ROW_TILE, t)
    return pl.pallas_call(
        _final_norm_kernel,
        grid=(t // tile,),
        in_specs=[pl.BlockSpec((tile, d), lambda i: (i, 0)), pl.BlockSpec((1, d), lambda i: (0, 0))],
        out_specs=pl.BlockSpec((tile, d), lambda i: (i, 0)),
        out_shape=jax.ShapeDtypeStruct((t, d), F32),
        compiler_params=_params(("arbitrary",)),
        name="final_norm",
    )(x, g.reshape(1, d))


def kernel(x_prompt, x_sample, c_prompt, c_sample, cache_kv_latent, cache_k_rope, page_table, state_ssm, state_conv, w_ada, b_ada, norm_mix_g, norm_ffn_g, norm_final_g, mla_w_in, mla_q_norm_g, mla_kv_norm_g, mla_w_qb, mla_w_kvb, mla_w_o, ssm_w_in, ssm_conv_w, ssm_conv_b, ssm_dt_bias, ssm_a_log, ssm_d, ssm_norm_g, ssm_w_out, ffn_w_gate, ffn_w_up, ffn_w_down, moe_w_router, moe_w_gate, moe_w_up, moe_w_down):
    nb, seq, d = x_prompt.shape
    ns = x_sample.shape[0]
    depth = w_ada.shape[0]
    past_len = page_table.shape[1] * cache_kv_latent.shape[2]
    attn_scale = (QK_NOPE + QK_ROPE) ** -0.5

    xp = x_prompt.reshape(nb * seq, d)
    xs = x_sample.reshape(ns, d)
    mods = _ada(jnp.concatenate([c_prompt, c_sample], axis=0), w_ada, b_ada)
    mods = mods.reshape(depth, nb + ns, 6, d)
    tabs_p = _rope_tables(jnp.arange(seq), attn_scale)
    tabs_s = _rope_tables(jnp.full((ns,), past_len), attn_scale)
    state = state_ssm.reshape(state_ssm.shape[0], ns, -1, D_STATE)

    lat_p, rope_p, lat_s, rope_s = [], [], [], []
    ssm_p, conv_p, ssm_s, conv_s = [], [], [], []
    for i in range(depth):
        j = i // 2
        mp = [mods[i, :nb, k] for k in range(6)]
        ms = [mods[i, nb:, k] for k in range(6)]
        g_mix = norm_mix_g[i].reshape(1, d)
        g_ffn = norm_ffn_g[i].reshape(1, d)
        if i % 2 == 0:
            w = _prep_mla(mla_w_in[j], mla_w_qb[j], mla_w_kvb[j])
            qg = mla_q_norm_g[j].reshape(1, -1)
            kvg = mla_kv_norm_g[j].reshape(1, -1)
            w_o = mla_w_o[j].astype(BF16)
            ckv, kr, q, k, v = _mla_proj(xp, g_mix, mp[0], mp[1], tabs_p, qg, kvg, w, seq, True)
            r3 = lambda a: a.reshape(nb, seq, a.shape[1])
            o = _flash(r3(q), r3(k), r3(v)).reshape(nb * seq, -1)
            xp = _proj_res(o, w_o, xp, mp[2], seq)
            lat_p.append(ckv.reshape(nb, seq, -1))
            rope_p.append(kr.reshape(nb, seq, -1))

            ckv, kr, q = _mla_proj(xs, g_mix, ms[0], ms[1], tabs_s, qg, kvg, w, 1, False)
            kv_rank = ckv.shape[1]
            qlat = _headmat(q, w["wabs"], LANE, kv_rank)
            o_lat = _decode_attn(j, page_table, qlat.reshape(ns, MLA_HEADS, kv_rank), q.reshape(ns, MLA_HEADS, LANE),
                                 ckv.reshape(ns, 1, kv_rank), kr.reshape(ns, 1, QK_ROPE), cache_kv_latent,
                                 cache_k_rope)
            o = _headmat(o_lat.reshape(ns, MLA_HEADS * kv_rank), w["wvbd"], 2 * kv_rank, 2 * V_DIM)
            xs = _proj_res(o, w_o, xs, ms[2], 1)
            lat_s.append(ckv.reshape(ns, 1, -1))
            rope_s.append(kr.reshape(ns, 1, -1))
        else:
            w = _prep_ssm(ssm_w_in[j], ssm_conv_w[j], ssm_conv_b[j], ssm_dt_bias[j], ssm_a_log[j], ssm_d[j],
                          ssm_norm_g[j], ssm_w_out[j])
            z, xin, bm, cm, dt, conv = _ssm_in(xp, g_mix, mp[0], mp[1], w, seq)
            y, h_new = _ssd(xin, bm, cm, dt, z, w, nb, seq)
            xp = _proj_res(y, w["w_out"], xp, mp[2], seq)
            ssm_p.append(h_new.reshape(nb, -1, SSM_HEAD_DIM, D_STATE))
            conv_p.append(conv[:, SUBLANE - (CONV_W - 1):])

            buf_t = state_conv[j].transpose(1, 0, 2)
            z, xin, bm, cm, conv, xdt_t, dec_t = _ssm_step_in(xs, g_mix, ms[0], ms[1], buf_t, w)
            h_new, y_t = _ssm_step(j, state, xdt_t, dec_t, bm, cm)
            xs = _ssm_step_out(y_t, xin, z, w, xs, ms[2])
            ssm_s.append(h_new.reshape(ns, -1, SSM_HEAD_DIM, D_STATE))
            conv_s.append(conv.transpose(1, 0, 2))
        if i % 2 == 0:
            wg = ffn_w_gate[j].astype(BF16)[None]
            wu = ffn_w_up[j].astype(BF16)[None]
            wd = ffn_w_down[j].astype(BF16)[None]
            xp = _ffn(xp, g_ffn, mp[3], mp[4], mp[5], wg, wu, wd, seq)
            xs = _ffn(xs, g_ffn, ms[3], ms[4], ms[5], wg, wu, wd, 1)
        else:
            wg = moe_w_gate[j].astype(BF16)
            wu = moe_w_up[j].astype(BF16)
            wd = moe_w_down[j].astype(BF16)
            cw = _router(xp, g_ffn, mp[3], mp[4], moe_w_router[j], seq)
            xp = _ffn(xp, g_ffn, mp[3], mp[4], mp[5], wg, wu, wd, seq, cw)
            cw = _router(xs, g_ffn, ms[3], ms[4], moe_w_router[j], 1)
            xs = _ffn(xs, g_ffn, ms[3], ms[4], ms[5], wg, wu, wd, 1, cw)
    y_prompt = _final_norm(xp, norm_final_g).reshape(nb, seq, d)
    y_sample = _final_norm(xs, norm_final_g).reshape(ns, 1, d)
    return (y_prompt, y_sample, jnp.stack(lat_p), jnp.stack(rope_p), jnp.stack(lat_s), jnp.stack(rope_s),
            jnp.stack(ssm_p), jnp.stack(conv_p), jnp.stack(ssm_s), jnp.stack(conv_s))
```

```python
import functools
import math

import jax
import jax.numpy as jnp
from jax import lax
from jax.experimental import pallas as pl
from jax.experimental.pallas import tpu as pltpu

F32 = jnp.float32
BF16 = jnp.bfloat16

RMS_EPS = 1e-6
MLA_HEADS = 16
QK_NOPE = 64
QK_ROPE = 32
V_DIM = 64
ROPE_THETA = 10000.0
PAGE_SIZE = 128
SSM_HEAD_DIM = 64
SSM_GROUPS = 4
D_STATE = 128
CONV_W = 4
SSD_CHUNK = 128
N_EXPERTS = 8

LANE = 128
SUBLANE = 8
VMEM_LIMIT = 56 << 20

ROW_TILE = 256
FFN_ROW_TILE = 512
FFN_COL_TILE = 1408
ATTN_TILE = 512
DECODE_PAGES = 64
NEG = -1e30
LOG2E = math.log2(math.e)


def _params(sem, **kw):
    return pltpu.CompilerParams(dimension_semantics=sem, vmem_limit_bytes=VMEM_LIMIT, **kw)


def _dot(a, b):
    return jnp.dot(a, b, preferred_element_type=F32)


def _dot_nt(a, b):
    return lax.dot_general(a, b, (((1,), (1,)), ((), ())), preferred_element_type=F32)


def _dot_tn(a, b):
    return lax.dot_general(a, b, (((0,), (0,)), ((), ())), preferred_element_type=F32)


def _split3(v):
    hi = v.astype(BF16)
    r = v - hi.astype(F32)
    mid = r.astype(BF16)
    lo = (r - mid.astype(F32)).astype(BF16)
    return hi, mid, lo


def _silu(x):
    return x * jax.nn.sigmoid(x)


def _rms(x, g):
    return (x * lax.rsqrt(jnp.mean(x * x, axis=-1, keepdims=True) + RMS_EPS)) * g


def _modulate(x, g, shift, scale):
    return _rms(x, g) * (1.0 + scale) + shift


def _mod_specs(n_rows, tile, rows_per_seq, d):
    if rows_per_seq == 1:
        return pl.BlockSpec((1, tile, d), lambda i, *_: (i, 0, 0))
    tiles_per_seq = rows_per_seq // tile
    return pl.BlockSpec((1, 1, d), lambda i, *_: (i // tiles_per_seq, 0, 0))


def _mod_arr(m, rows_per_seq, tile):
    if rows_per_seq == 1:
        return m.reshape(m.shape[0] // tile, tile, m.shape[1])
    return m.reshape(m.shape[0], 1, m.shape[1])


def _ada_kernel(c_ref, w_ref, b_ref, o_ref):
    s = _silu(c_ref[...]).astype(BF16)
    o_ref[0] = _dot(s, w_ref[0].astype(BF16)) + b_ref[0]


def _ada(c_all, w_ada, b_ada):
    depth, d, d6 = w_ada.shape
    n = c_all.shape[0]
    return pl.pallas_call(
        _ada_kernel,
        grid=(depth, d6 // d),
        in_specs=[pl.BlockSpec((n, d), lambda i, k: (0, 0)),
                  pl.BlockSpec((1, d, d), lambda i, k: (i, 0, k)),
                  pl.BlockSpec((1, 1, d), lambda i, k: (i, 0, k))],
        out_specs=pl.BlockSpec((1, n, d), lambda i, k: (i, 0, k)),
        out_shape=jax.ShapeDtypeStruct((depth, n, d6), F32),
        compiler_params=_params(("arbitrary", "arbitrary")),
        name="adaln",
    )(c_all, w_ada, b_ada.reshape(depth, 1, d6))


def _mla_proj_kernel(x_ref, g_ref, sh_ref, sc_ref, ct_ref, st_ref, cts_ref, sts_ref, wcq_ref, wckv_ref,
                     wkra_ref, wkrb_ref, qg_ref, kvg_ref, wqa_ref, wqb_ref, *rest, heads, with_kv):
    if with_kv:
        wka_ref, wvt_ref, ckv_ref, kr_ref, q_ref, k_ref, vt_ref = rest
    else:
        ckv_ref, kr_ref, q_ref = rest
    h = _modulate(x_ref[...], g_ref[...], sh_ref[0], sc_ref[0]).astype(BF16)
    cq = _rms(_dot(h, wcq_ref[...]), qg_ref[...]).astype(BF16)
    ckv = _rms(_dot(h, wckv_ref[...]), kvg_ref[...])
    kr = _dot(h, wkra_ref[...]) * ct_ref[...] + _dot(h, wkrb_ref[...]) * st_ref[...]
    ckv_ref[...] = ckv
    kr_ref[...] = kr[:, :QK_ROPE]
    rep = lambda t: jnp.concatenate([t] * heads, axis=1)
    q = (_dot(cq, wqa_ref[...]) * rep(cts_ref[...]) + _dot(cq, wqb_ref[...]) * rep(sts_ref[...])).astype(BF16)
    if not with_kv:
        q_ref[...] = q
        return
    ckvb = ckv.astype(BF16)
    k = (_dot(ckvb, wka_ref[...]) + rep(kr)).astype(BF16)
    for hd in range(heads):
        q_ref[hd] = q[:, hd * LANE:(hd + 1) * LANE]
        k_ref[hd] = k[:, hd * LANE:(hd + 1) * LANE]
    vt_ref[...] = _dot_nt(wvt_ref[...], ckvb).astype(BF16)


def _rot_cols(w):
    half = w.shape[-1] // 2
    return jnp.concatenate([-w[..., half:], w[..., :half]], axis=-1)


def _prep_mla(w_in, w_qb, w_kvb):
    d = w_in.shape[0]
    q_rank = w_qb.shape[0]
    kv_rank = w_kvb.shape[0]
    hd = LANE
    pad = hd - QK_ROPE - QK_NOPE
    w_kr = w_in[:, q_rank + kv_rank:]
    zk = jnp.zeros((d, hd - QK_ROPE), F32)
    wq = w_qb.reshape(q_rank, MLA_HEADS, QK_NOPE + QK_ROPE)
    q_nope, q_pe = wq[..., :QK_NOPE], wq[..., QK_NOPE:]
    zq = jnp.zeros((q_rank, MLA_HEADS, pad), F32)
    wkv = w_kvb.reshape(kv_rank, MLA_HEADS, QK_NOPE + V_DIM)
    k_nope, v = wkv[..., :QK_NOPE], wkv[..., QK_NOPE:]
    return dict(
        wcq=w_in[:, :q_rank].astype(BF16),
        wckv=w_in[:, q_rank:q_rank + kv_rank].astype(BF16),
        wkra=jnp.concatenate([w_kr, zk], axis=1).astype(BF16),
        wkrb=jnp.concatenate([_rot_cols(w_kr), zk], axis=1).astype(BF16),
        wqa=jnp.concatenate([q_pe, q_nope, zq], axis=-1).reshape(q_rank, MLA_HEADS * hd).astype(BF16),
        wqb=jnp.concatenate([_rot_cols(q_pe), jnp.zeros((q_rank, MLA_HEADS, hd - QK_ROPE), F32)],
                            axis=-1).reshape(q_rank, MLA_HEADS * hd).astype(BF16),
        wka=jnp.concatenate([jnp.zeros((kv_rank, MLA_HEADS, QK_ROPE), F32), k_nope,
                             jnp.zeros((kv_rank, MLA_HEADS, pad), F32)],
                            axis=-1).reshape(kv_rank, MLA_HEADS * hd).astype(BF16),
        wvt=v.reshape(kv_rank, MLA_HEADS * V_DIM).T.astype(BF16),
        wabs=jnp.concatenate([jnp.zeros((MLA_HEADS, QK_ROPE, kv_rank), F32), k_nope.transpose(1, 2, 0),
                              jnp.zeros((MLA_HEADS, pad, kv_rank), F32)], axis=1).astype(BF16),
        wvbd=_block_diag_pairs(v.transpose(1, 0, 2)).astype(BF16),
    )


def _block_diag_pairs(v):
    h, r, dv = v.shape
    v = v.reshape(h // 2, 2, r, dv)
    z = jnp.zeros((h // 2, r, dv), v.dtype)
    top = jnp.concatenate([v[:, 0], z], axis=-1)
    bot = jnp.concatenate([z, v[:, 1]], axis=-1)
    return jnp.concatenate([top, bot], axis=1)


def _rope_tables(pos, scale):
    half = QK_ROPE // 2
    inv = ROPE_THETA ** (-jnp.arange(half, dtype=F32) / half)
    ang = pos.astype(F32)[:, None] * inv[None, :]
    cos, sin = jnp.cos(ang), jnp.sin(ang)
    n = pos.shape[0]
    ct = jnp.concatenate([cos, cos, jnp.ones((n, LANE - QK_ROPE), F32)], axis=1)
    st = jnp.concatenate([sin, sin, jnp.zeros((n, LANE - QK_ROPE), F32)], axis=1)
    return ct, st, ct * scale, st * scale


def _mla_proj(x, g, shift, scale, tabs, qg, kvg, w, rows_per_seq, with_kv):
    t, d = x.shape
    tile = min(ROW_TILE, t)
    n_tab_tiles = tabs[0].shape[0] // tile
    heads = MLA_HEADS
    kv_rank = w["wckv"].shape[1]
    full = lambda a: pl.BlockSpec(a.shape, lambda i: (0,) * a.ndim)
    row = lambda n: pl.BlockSpec((tile, n), lambda i: (i, 0))
    tab = pl.BlockSpec((tile, LANE), lambda i: (i % n_tab_tiles, 0))
    mod = _mod_specs(t, tile, rows_per_seq, d)
    ws = [w["wcq"], w["wckv"], w["wkra"], w["wkrb"], qg, kvg, w["wqa"], w["wqb"]]
    outs = [jax.ShapeDtypeStruct((t, kv_rank), F32), jax.ShapeDtypeStruct((t, QK_ROPE), F32)]
    out_specs = [row(kv_rank), row(QK_ROPE)]
    if with_kv:
        ws += [w["wka"], w["wvt"]]
        head_major = pl.BlockSpec((heads, tile, LANE), lambda i: (0, i, 0))
        outs += [jax.ShapeDtypeStruct((heads, t, LANE), BF16), jax.ShapeDtypeStruct((heads, t, LANE), BF16),
                 jax.ShapeDtypeStruct((heads * V_DIM, t), BF16)]
        out_specs += [head_major, head_major, pl.BlockSpec((heads * V_DIM, tile), lambda i: (0, i))]
    else:
        outs.append(jax.ShapeDtypeStruct((t, heads * LANE), BF16))
        out_specs.append(row(heads * LANE))
    return pl.pallas_call(
        functools.partial(_mla_proj_kernel, heads=heads, with_kv=with_kv),
        grid=(t // tile,),
        in_specs=[row(d), full(g), mod, mod, tab, tab, tab, tab] + [full(a) for a in ws],
        out_specs=out_specs,
        out_shape=outs,
        compiler_params=_params(("arbitrary",)),
        name="mla_proj",
    )(x, g, _mod_arr(shift, rows_per_seq, tile), _mod_arr(scale, rows_per_seq, tile), *tabs, *ws)


def _flash_kernel(qi_tab, ki_tab, q_ref, k_ref, vt_ref, o_ref, m_sc, l_sc, acc_sc, *, tile, heads):
    t = pl.program_id(1)
    qi = qi_tab[t]
    ki = ki_tab[t]

    @pl.when(ki == 0)
    def _():
        m_sc[...] = jnp.full(m_sc.shape, NEG, F32)
        l_sc[...] = jnp.zeros(l_sc.shape, F32)
        acc_sc[...] = jnp.zeros(acc_sc.shape, F32)

    def sweep(diagonal):
        if diagonal:
            visible = (lax.broadcasted_iota(jnp.int32, (tile, tile), 0)
                       <= lax.broadcasted_iota(jnp.int32, (tile, tile), 1))

        def head(h, carry):
            s = _dot_nt(k_ref[h], q_ref[h])
            if diagonal:
                s = jnp.where(visible, s, NEG)
            m_prev = m_sc[h]
            m_new = jnp.maximum(m_prev, jnp.max(s, axis=0, keepdims=True))
            alpha = jnp.exp2(m_prev - m_new)
            p = jnp.exp2(s - m_new)
            l_new = alpha * l_sc[h] + jnp.sum(p, axis=0, keepdims=True)
            acc = alpha * acc_sc[h] + _dot(vt_ref[h], p.astype(BF16))
            if diagonal:
                o_ref[h] = (acc / l_new).astype(o_ref.dtype)
            else:
                m_sc[h] = m_new
                l_sc[h] = l_new
                acc_sc[h] = acc
            return carry

        lax.fori_loop(0, heads, head, 0)

    @pl.when(ki < qi)
    def _():
        sweep(False)

    @pl.when(ki == qi)
    def _():
        sweep(True)


def _flash(q, k, vt, n_seq, seq_len):
    heads = q.shape[0]
    tile = min(ATTN_TILE, seq_len)
    nq = seq_len // tile
    pairs = [(i, j) for i in range(nq) for j in range(i + 1)]
    qi_tab = jnp.array([p[0] for p in pairs], jnp.int32)
    ki_tab = jnp.array([p[1] for p in pairs], jnp.int32)
    grid_spec = pltpu.PrefetchScalarGridSpec(
        num_scalar_prefetch=2,
        grid=(n_seq, len(pairs)),
        in_specs=[pl.BlockSpec((heads, tile, LANE), lambda b, t, qt, kt: (0, b * nq + qt[t], 0)),
                  pl.BlockSpec((heads, tile, LANE), lambda b, t, qt, kt: (0, b * nq + kt[t], 0)),
                  pl.BlockSpec((heads, V_DIM, tile), lambda b, t, qt, kt: (0, 0, b * nq + kt[t]))],
        out_specs=pl.BlockSpec((heads, V_DIM, tile), lambda b, t, qt, kt: (0, 0, b * nq + qt[t])),
        scratch_shapes=[pltpu.VMEM((heads, 1, tile), F32), pltpu.VMEM((heads, 1, tile), F32),
                        pltpu.VMEM((heads, V_DIM, tile), F32)],
    )
    return pl.pallas_call(
        functools.partial(_flash_kernel, tile=tile, heads=heads),
        grid_spec=grid_spec,
        out_shape=jax.ShapeDtypeStruct(vt.shape, BF16),
        compiler_params=_params(("arbitrary", "arbitrary")),
        name="flash_attn",
    )(qi_tab, ki_tab, q, k, vt)


def _headmat_kernel(a_ref, w_ref, o_ref):
    o_ref[...] = _dot(a_ref[...], w_ref[0]).astype(o_ref.dtype)


def _headmat(a, w, in_w, out_w):
    n, rows = w.shape[0], a.shape[0]
    return pl.pallas_call(
        _headmat_kernel,
        grid=(n,),
        in_specs=[pl.BlockSpec((rows, in_w), lambda i: (0, i)),
                  pl.BlockSpec((1, in_w, out_w), lambda i: (i, 0, 0))],
        out_specs=pl.BlockSpec((rows, out_w), lambda i: (0, i)),
        out_shape=jax.ShapeDtypeStruct((rows, n * out_w), BF16),
        compiler_params=_params(("arbitrary",)),
        name="head_matmul",
    )(a, w)


def _decode_kernel(pt_ref, qlat_ref, q_ref, ckv_ref, kr_ref, *rest, pages):
    lat_refs, rope_refs = rest[:pages], rest[pages:2 * pages]
    o_ref, kcat, rcat, m_sc, l_sc, acc_sc = rest[2 * pages:]
    c = pl.program_id(1)

    @pl.when(c == 0)
    def _():
        m_sc[...] = jnp.full(m_sc.shape, NEG, F32)
        l_sc[...] = jnp.zeros(l_sc.shape, F32)
        acc_sc[...] = jnp.zeros(acc_sc.shape, F32)

    for i in range(pages):
        kcat[i * PAGE_SIZE:(i + 1) * PAGE_SIZE, :] = lat_refs[i][0, 0].astype(BF16)
        rcat[i * PAGE_SIZE:(i + 1) * PAGE_SIZE, :] = rope_refs[i][0, 0].astype(BF16)
    ql = qlat_ref[0]
    qp = q_ref[0][:, :QK_ROPE]
    keys = kcat[...]
    s = _dot_nt(ql, keys) + _dot_nt(qp, rcat[...])
    m = m_sc[...]
    m_new = jnp.maximum(m, jnp.max(s, axis=1, keepdims=True))
    alpha = jnp.exp(m - m_new)
    p = jnp.exp(s - m_new)
    l = alpha * l_sc[...] + jnp.sum(p, axis=1, keepdims=True)
    acc = alpha * acc_sc[...] + _dot(p.astype(BF16), keys)
    m_sc[...], l_sc[...], acc_sc[...] = m_new, l, acc

    @pl.when(c == pl.num_programs(1) - 1)
    def _():
        kl = ckv_ref[0].astype(BF16).astype(F32)
        kp = kr_ref[0].astype(BF16).astype(F32)
        s1 = (jnp.sum(ql.astype(F32) * kl, axis=1, keepdims=True)
              + jnp.sum(qp.astype(F32) * kp, axis=1, keepdims=True))
        m1 = jnp.maximum(m_new, s1)
        a1 = jnp.exp(m_new - m1)
        p1 = jnp.exp(s1 - m1)
        o_ref[0] = ((a1 * acc + p1.astype(BF16).astype(F32) * kl) / (a1 * l + p1)).astype(o_ref.dtype)


def _decode_attn(layer, page_table, qlat, q, ckv, kr, cache_lat, cache_rope):
    b, heads, rank = qlat.shape
    n_pages = page_table.shape[1]
    pages = min(DECODE_PAGES, n_pages)
    lat_spec = lambda i: pl.BlockSpec(
        (1, 1, PAGE_SIZE, rank), lambda b_, c, pt: (layer, pt[b_ * n_pages + c * pages + i], 0, 0))
    rope_spec = lambda i: pl.BlockSpec(
        (1, 1, PAGE_SIZE, QK_ROPE), lambda b_, c, pt: (layer, pt[b_ * n_pages + c * pages + i], 0, 0))
    seq = lambda n, w: pl.BlockSpec((1, n, w), lambda b_, c, pt: (b_, 0, 0))
    grid_spec = pltpu.PrefetchScalarGridSpec(
        num_scalar_prefetch=1,
        grid=(b, n_pages // pages),
        in_specs=[seq(heads, rank), seq(heads, LANE), seq(1, rank), seq(1, QK_ROPE)]
        + [lat_spec(i) for i in range(pages)] + [rope_spec(i) for i in range(pages)],
        out_specs=seq(heads, rank),
        scratch_shapes=[pltpu.VMEM((pages * PAGE_SIZE, rank), BF16), pltpu.VMEM((pages * PAGE_SIZE, QK_ROPE), BF16),
                        pltpu.VMEM((heads, 1), F32), pltpu.VMEM((heads, 1), F32), pltpu.VMEM((heads, rank), F32)],
    )
    return pl.pallas_call(
        functools.partial(_decode_kernel, pages=pages),
        grid_spec=grid_spec,
        out_shape=jax.ShapeDtypeStruct((b, heads, rank), BF16),
        compiler_params=_params(("arbitrary", "arbitrary")),
        name="decode_attn",
    )(page_table.reshape(-1), qlat, q, ckv, kr, *([cache_lat] * pages), *([cache_rope] * pages))


def _proj_res_kernel(a_ref, w_ref, x_ref, gate_ref, o_ref, *, transposed):
    y = _dot_tn(a_ref[...], w_ref[...]) if transposed else _dot(a_ref[...], w_ref[...])
    o_ref[...] = x_ref[...] + gate_ref[0] * y


def _proj_res(a, w, x, gate, rows_per_seq, transposed=False):
    t, d = x.shape
    k = w.shape[0]
    tile = min(FFN_ROW_TILE, t)
    a_spec = pl.BlockSpec((k, tile), lambda i: (0, i)) if transposed else pl.BlockSpec((tile, k), lambda i: (i, 0))
    return pl.pallas_call(
        functools.partial(_proj_res_kernel, transposed=transposed),
        grid=(t // tile,),
        in_specs=[a_spec, pl.BlockSpec((k, d), lambda i: (0, 0)),
                  pl.BlockSpec((tile, d), lambda i: (i, 0)), _mod_specs(t, tile, rows_per_seq, d)],
        out_specs=pl.BlockSpec((tile, d), lambda i: (i, 0)),
        out_shape=jax.ShapeDtypeStruct((t, d), F32),
        compiler_params=_params(("arbitrary",)),
        name="proj_residual",
    )(a, w, x, _mod_arr(gate, rows_per_seq, tile))


def _ffn_kernel(x_ref, g_ref, sh_ref, sc_ref, gate_ref, wg_ref, wu_ref, wd_ref, o_ref, h_sc, acc_sc):
    f = pl.program_id(1)

    @pl.when(f == 0)
    def _():
        h_sc[...] = _modulate(x_ref[...], g_ref[...], sh_ref[0], sc_ref[0]).astype(BF16)
        acc_sc[...] = jnp.zeros(acc_sc.shape, F32)

    h = h_sc[...]
    a = (_silu(_dot(h, wg_ref[...])) * _dot(h, wu_ref[...])).astype(BF16)
    acc_sc[...] += _dot(a, wd_ref[...])

    @pl.when(f == pl.num_programs(1) - 1)
    def _():
        o_ref[...] = x_ref[...] + gate_ref[0] * acc_sc[...]


def _col_tile(ff):
    return FFN_COL_TILE if ff % FFN_COL_TILE == 0 else ff


def _ffn(x, g, shift, scale, gate, wg, wu, wd, rows_per_seq):
    t, d = x.shape
    ff = wg.shape[1]
    tile = min(FFN_ROW_TILE, t)
    tf = _col_tile(ff)
    mod = _mod_specs(t, tile, rows_per_seq, d)
    row = pl.BlockSpec((tile, d), lambda i, f: (i, 0))
    return pl.pallas_call(
        _ffn_kernel,
        grid=(t // tile, ff // tf),
        in_specs=[row, pl.BlockSpec(g.shape, lambda i, f: (0, 0)), mod, mod, mod,
                  pl.BlockSpec((d, tf), lambda i, f: (0, f)), pl.BlockSpec((d, tf), lambda i, f: (0, f)),
                  pl.BlockSpec((tf, d), lambda i, f: (f, 0))],
        out_specs=row,
        out_shape=jax.ShapeDtypeStruct((t, d), F32),
        scratch_shapes=[pltpu.VMEM((tile, d), BF16), pltpu.VMEM((tile, d), F32)],
        compiler_params=_params(("arbitrary", "arbitrary")),
        name="swiglu",
    )(x, g, *[_mod_arr(m, rows_per_seq, tile) for m in (shift, scale, gate)], wg, wu, wd)


_E1, _E2, _R1, _R2, _W1, _W2 = range(6)


def _router_kernel(x_ref, g_ref, sh_ref, sc_ref, wr_ref, meta_ref, cnt_ref, h3_ref, cnt_sc):
    i = pl.program_id(0)

    @pl.when(i == 0)
    def _():
        cnt_sc[...] = jnp.zeros(cnt_sc.shape, F32)

    h = _modulate(x_ref[...], g_ref[...], sh_ref[0], sc_ref[0])
    for j in range(h3_ref.shape[1]):
        h3_ref[:, j, :] = h[:, j * LANE:(j + 1) * LANE]
    hi = h.astype(BF16)
    lo = (h - hi.astype(F32)).astype(BF16)
    w = wr_ref[...]
    whi = w.astype(BF16)
    wlo = (w - whi.astype(F32)).astype(BF16)
    logits = _dot(hi, whi) + _dot(lo, whi) + _dot(hi, wlo)
    lane = lax.broadcasted_iota(jnp.int32, logits.shape, 1).astype(F32)
    logits = jnp.where(lane < N_EXPERTS, logits, NEG)
    p = jnp.exp(logits - jnp.max(logits, axis=1, keepdims=True))
    p = p / jnp.sum(p, axis=1, keepdims=True)
    p1 = jnp.max(p, axis=1, keepdims=True)
    i1 = jnp.min(jnp.where(p == p1, lane, float(LANE)), axis=1, keepdims=True)
    rest = jnp.where(lane == i1, -1.0, p)
    p2 = jnp.max(rest, axis=1, keepdims=True)
    i2 = jnp.min(jnp.where(rest == p2, lane, float(LANE)), axis=1, keepdims=True)
    den = p1 + p2
    hit1, hit2 = lane == i1, lane == i2
    onehot = jnp.where(hit1 | hit2, 1.0, 0.0)
    tile = onehot.shape[0]
    ltri = jnp.where(lax.broadcasted_iota(jnp.int32, (tile, tile), 0)
                     >= lax.broadcasted_iota(jnp.int32, (tile, tile), 1), 1.0, 0.0).astype(BF16)
    incl = _dot(ltri, onehot.astype(BF16))
    before = cnt_sc[...] + incl - onehot
    r1 = jnp.sum(jnp.where(hit1, before, 0.0), axis=1, keepdims=True)
    r2 = jnp.sum(jnp.where(hit2, before, 0.0), axis=1, keepdims=True)
    cnt = cnt_sc[...] + incl[tile - 1:tile, :]
    cnt_sc[...] = cnt
    cnt_ref[...] = jnp.broadcast_to(cnt, cnt_ref.shape)
    rec = jnp.zeros(logits.shape, F32)
    for ln, val in ((_E1, i1), (_E2, i2), (_R1, r1), (_R2, r2), (_W1, p1 / den), (_W2, p2 / den)):
        rec = jnp.where(lane == ln, val, rec)
    meta_ref[...] = rec


def _router(x, g, shift, scale, w_router, rows_per_seq):
    t, d = x.shape
    tile = min(FFN_ROW_TILE, t)
    wr = jnp.concatenate([w_router, jnp.zeros((d, LANE - w_router.shape[1]), F32)], axis=1)
    mod = _mod_specs(t, tile, rows_per_seq, d)
    return pl.pallas_call(
        _router_kernel,
        grid=(t // tile,),
        in_specs=[pl.BlockSpec((tile, d), lambda i: (i, 0)), pl.BlockSpec(g.shape, lambda i: (0, 0)), mod, mod,
                  pl.BlockSpec((d, LANE), lambda i: (0, 0))],
        out_specs=[pl.BlockSpec((tile, LANE), lambda i: (i, 0)), pl.BlockSpec((SUBLANE, LANE), lambda i: (0, 0)),
                   pl.BlockSpec((tile, d // LANE, LANE), lambda i: (i, 0, 0))],
        out_shape=[jax.ShapeDtypeStruct((t, LANE), F32), jax.ShapeDtypeStruct((SUBLANE, LANE), F32),
                   jax.ShapeDtypeStruct((t, d // LANE, LANE), F32)],
        scratch_shapes=[pltpu.VMEM((1, LANE), F32)],
        compiler_params=_params(("arbitrary",)),
        name="router",
    )(x, g, _mod_arr(shift, rows_per_seq, tile), _mod_arr(scale, rows_per_seq, tile), wr)


def _row_copy(src, dst, sem):
    return pltpu.make_async_copy(src, dst, sem)


def _dispatch_kernel(pos_ref, h_ref, init_ref, xs_ref, sem, *, tile):
    del init_ref
    base = pl.program_id(0) * tile * 2

    def issue(t, carry):
        _row_copy(h_ref.at[t], xs_ref.at[pos_ref[base + 2 * t]], sem).start()
        _row_copy(h_ref.at[t], xs_ref.at[pos_ref[base + 2 * t + 1]], sem).start()
        return carry

    def drain(t, carry):
        _row_copy(h_ref.at[0], xs_ref.at[0], sem).wait()
        _row_copy(h_ref.at[0], xs_ref.at[0], sem).wait()
        return carry

    lax.fori_loop(0, tile, issue, 0)
    lax.fori_loop(0, tile, drain, 0)


def _dispatch(pos, h3, n_rows, tile):
    t, chunks, _ = h3.shape
    grid_spec = pltpu.PrefetchScalarGridSpec(
        num_scalar_prefetch=1,
        grid=(t // tile,),
        in_specs=[pl.BlockSpec((tile, chunks, LANE), lambda i, p: (i, 0, 0)), pl.BlockSpec(memory_space=pl.ANY)],
        out_specs=pl.BlockSpec(memory_space=pl.ANY),
        scratch_shapes=[pltpu.SemaphoreType.DMA(())],
    )
    return pl.pallas_call(
        functools.partial(_dispatch_kernel, tile=tile),
        grid_spec=grid_spec,
        out_shape=jax.ShapeDtypeStruct((n_rows, chunks, LANE), F32),
        input_output_aliases={2: 0},
        compiler_params=_params(("arbitrary",), disable_bounds_checks=True),
        name="moe_dispatch",
    )(pos, h3, jnp.zeros((n_rows, chunks, LANE), F32))


def _moe_ffn_kernel(blk_ref, exp_ref, nt_ref, x_ref, wg_ref, wu_ref, wd_ref, y_ref, h_sc, acc_sc):
    i, f = pl.program_id(0), pl.program_id(1)
    chunks = x_ref.shape[1]

    @pl.when(i < nt_ref[0])
    def _():
        @pl.when(f == 0)
        def _():
            h_sc[...] = jnp.concatenate([x_ref[:, j, :] for j in range(chunks)], axis=1).astype(BF16)
            acc_sc[...] = jnp.zeros(acc_sc.shape, F32)

        h = h_sc[...]
        a = (_silu(_dot(h, wg_ref[0])) * _dot(h, wu_ref[0])).astype(BF16)
        acc_sc[...] += _dot(a, wd_ref[0])

        @pl.when(f == pl.num_programs(1) - 1)
        def _():
            for j in range(chunks):
                y_ref[:, j, :] = acc_sc[:, j * LANE:(j + 1) * LANE]


def _moe_ffn(blk_tab, exp_tab, n_tiles, xs, wg, wu, wd, tile):
    n_rows, chunks, _ = xs.shape
    d = chunks * LANE
    ff = wg.shape[2]
    tf = _col_tile(ff)
    nf = ff // tf
    col = lambda i, f, nt: jnp.where(i < nt[0], f, nf - 1)
    rows = pl.BlockSpec((tile, chunks, LANE), lambda i, f, bt, et, nt: (bt[i], 0, 0))
    grid_spec = pltpu.PrefetchScalarGridSpec(
        num_scalar_prefetch=3,
        grid=(n_rows // tile, nf),
        in_specs=[rows,
                  pl.BlockSpec((1, d, tf), lambda i, f, bt, et, nt: (et[i], 0, col(i, f, nt))),
                  pl.BlockSpec((1, d, tf), lambda i, f, bt, et, nt: (et[i], 0, col(i, f, nt))),
                  pl.BlockSpec((1, tf, d), lambda i, f, bt, et, nt: (et[i], col(i, f, nt), 0))],
        out_specs=rows,
        scratch_shapes=[pltpu.VMEM((tile, d), BF16), pltpu.VMEM((tile, d), F32)],
    )
    return pl.pallas_call(
        _moe_ffn_kernel,
        grid_spec=grid_spec,
        out_shape=jax.ShapeDtypeStruct(xs.shape, F32),
        compiler_params=_params(("arbitrary", "arbitrary")),
        name="moe_swiglu",
    )(blk_tab, exp_tab, n_tiles, xs, wg, wu, wd)


def _combine_kernel(pos_ref, meta_ref, x_ref, gate_ref, ys_ref, o_ref, buf, sem, *, tile):
    base = pl.program_id(0) * tile * 2

    def issue(t, carry):
        _row_copy(ys_ref.at[pos_ref[base + 2 * t]], buf.at[0, t], sem).start()
        _row_copy(ys_ref.at[pos_ref[base + 2 * t + 1]], buf.at[1, t], sem).start()
        return carry

    def drain(t, carry):
        _row_copy(ys_ref.at[0], buf.at[0, 0], sem).wait()
        _row_copy(ys_ref.at[0], buf.at[0, 0], sem).wait()
        return carry

    lax.fori_loop(0, tile, issue, 0)
    lax.fori_loop(0, tile, drain, 0)
    meta = meta_ref[...]
    w1, w2 = meta[:, _W1:_W1 + 1], meta[:, _W2:_W2 + 1]
    gate = gate_ref[0]
    for j in range(buf.shape[2]):
        sl = slice(j * LANE, (j + 1) * LANE)
        o_ref[:, sl] = x_ref[:, sl] + gate[:, sl] * (w1 * buf[0, :, j, :] + w2 * buf[1, :, j, :])


def _combine(pos, meta, x, gate, ys, rows_per_seq, tile):
    t, d = x.shape
    chunks = d // LANE
    row = lambda n: pl.BlockSpec((tile, n), lambda i, p: (i, 0))
    grid_spec = pltpu.PrefetchScalarGridSpec(
        num_scalar_prefetch=1,
        grid=(t // tile,),
        in_specs=[row(LANE), row(d), _mod_specs(t, tile, rows_per_seq, d), pl.BlockSpec(memory_space=pl.ANY)],
        out_specs=row(d),
        scratch_shapes=[pltpu.VMEM((2, tile, chunks, LANE), F32), pltpu.SemaphoreType.DMA(())],
    )
    return pl.pallas_call(
        functools.partial(_combine_kernel, tile=tile),
        grid_spec=grid_spec,
        out_shape=jax.ShapeDtypeStruct((t, d), F32),
        compiler_params=_params(("arbitrary",), disable_bounds_checks=True),
        name="moe_combine",
    )(pos, meta, x, _mod_arr(gate, rows_per_seq, tile), ys)


def _moe(x, g, shift, scale, gate, w_router, wg, wu, wd, rows_per_seq):
    t, d = x.shape
    tile = min(FFN_ROW_TILE, t)
    meta, counts, h3 = _router(x, g, shift, scale, w_router, rows_per_seq)
    n_max = 2 * t // tile + N_EXPERTS
    cnt = counts[0, :N_EXPERTS].astype(jnp.int32)
    tiles = (cnt + tile - 1) // tile
    ends = jnp.cumsum(tiles)
    n_tiles = ends[-1]
    blk_tab = jnp.minimum(jnp.arange(n_max, dtype=jnp.int32), n_tiles - 1)
    exp_tab = jnp.sum(blk_tab[:, None] >= ends[None, :], axis=1).astype(jnp.int32)
    offs = (ends - tiles) * tile
    experts = meta[:, _E1:_E2 + 1].astype(jnp.int32)
    ranks = meta[:, _R1:_R2 + 1].astype(jnp.int32)
    pos = (offs[experts] + ranks).reshape(-1)
    xs = _dispatch(pos, h3, n_max * tile, tile)
    ys = _moe_ffn(blk_tab, exp_tab, n_tiles.reshape(1).astype(jnp.int32), xs, wg, wu, wd, tile)
    return _combine(pos, meta, x, gate, ys, rows_per_seq, tile)


def _softplus(x):
    return jnp.maximum(x, 0.0) + jnp.log1p(jnp.exp(-jnp.abs(x)))


def _ssm_in_kernel(x_ref, g_ref, sh_ref, sc_ref, wz_ref, wx_ref, wdt_ref, cw_ref, cb_ref, dtb_ref,
                   z_ref, xs_ref, b_ref, c_ref, dt_ref, conv_ref, ext_sc, *, tile, tiles_per_seq, d_inner, n_heads):
    i = pl.program_id(0)

    @pl.when(i % tiles_per_seq == 0)
    def _():
        ext_sc[0:SUBLANE, :] = jnp.zeros((SUBLANE, ext_sc.shape[1]), F32)

    h = _modulate(x_ref[...], g_ref[...], sh_ref[0], sc_ref[0]).astype(BF16)
    z_ref[...] = _dot(h, wz_ref[...])
    u = _dot(h, wx_ref[...])
    ext_sc[SUBLANE:SUBLANE + tile, :] = u
    cw = cw_ref[...]
    y = cw[0:1] * ext_sc[SUBLANE - 3:SUBLANE - 3 + tile, :]
    y = y + cw[1:2] * ext_sc[SUBLANE - 2:SUBLANE - 2 + tile, :]
    y = y + cw[2:3] * ext_sc[SUBLANE - 1:SUBLANE - 1 + tile, :]
    y = y + cw[3:4] * u + cb_ref[...]
    tail = ext_sc[tile:tile + SUBLANE, :]
    conv_ref[0] = tail
    ext_sc[0:SUBLANE, :] = tail
    xbc = _silu(y)
    gn = (xbc.shape[1] - d_inner) // 2
    xs_ref[...] = xbc[:, :d_inner]
    b_ref[...] = xbc[:, d_inner:d_inner + gn].astype(BF16)
    c_ref[...] = xbc[:, d_inner + gn:].astype(BF16)
    dt = _softplus(_dot(h, wdt_ref[...]) + dtb_ref[...])
    lane = lax.broadcasted_iota(jnp.int32, dt.shape, 1)
    dt_ref[...] = jnp.where(lane < n_heads, dt, 0.0)


def _ssm_in(x, g, shift, scale, w, rows_per_seq):
    t, d = x.shape
    tile = min(ROW_TILE, rows_per_seq)
    d_inner, conv_dim = w["wz"].shape[1], w["wx"].shape[1]
    gn = (conv_dim - d_inner) // 2
    n_seq = t // rows_per_seq
    tiles_per_seq = rows_per_seq // tile
    full = lambda a: pl.BlockSpec(a.shape, lambda i: (0,) * a.ndim)
    row = lambda n: pl.BlockSpec((tile, n), lambda i: (i, 0))
    mod = _mod_specs(t, tile, rows_per_seq, d)
    ws = [w["wz"], w["wx"], w["wdt"], w["conv_w"], w["conv_b"], w["dt_bias"]]
    return pl.pallas_call(
        functools.partial(_ssm_in_kernel, tile=tile, tiles_per_seq=tiles_per_seq, d_inner=d_inner,
                          n_heads=d_inner // SSM_HEAD_DIM),
        grid=(t // tile,),
        in_specs=[row(d), full(g), mod, mod] + [full(a) for a in ws],
        out_specs=[row(d_inner), row(d_inner), row(gn), row(gn), row(LANE),
                   pl.BlockSpec((1, SUBLANE, conv_dim), lambda i: (i // tiles_per_seq, 0, 0))],
        out_shape=[jax.ShapeDtypeStruct((t, d_inner), F32), jax.ShapeDtypeStruct((t, d_inner), F32),
                   jax.ShapeDtypeStruct((t, gn), BF16), jax.ShapeDtypeStruct((t, gn), BF16),
                   jax.ShapeDtypeStruct((t, LANE), F32), jax.ShapeDtypeStruct((n_seq, SUBLANE, conv_dim), F32)],
        scratch_shapes=[pltpu.VMEM((tile + SUBLANE, conv_dim), F32)],
        compiler_params=_params(("arbitrary",)),
        name="ssm_in_conv",
    )(x, g, _mod_arr(shift, rows_per_seq, tile), _mod_arr(scale, rows_per_seq, tile), *ws)


def _ssd_kernel(xs_ref, b_ref, c_ref, dt_ref, z_ref, alog_ref, e_ref, dskip_ref, ng_ref, y_ref, hout_ref,
                st_sc, y_sc, *, cl, n_groups, pairs_per_group):
    c = pl.program_id(1)

    @pl.when(c == 0)
    def _():
        st_sc[...] = jnp.zeros(st_sc.shape, F32)

    dt = dt_ref[0]
    d_a = dt * (-jnp.exp(alog_ref[...]))
    rowi = lax.broadcasted_iota(jnp.int32, (cl, cl), 0)
    coli = lax.broadcasted_iota(jnp.int32, (cl, cl), 1)
    causal = rowi >= coli
    ltri = jnp.where(causal, 1.0, 0.0).astype(BF16)
    cum = sum(_dot(ltri, part) for part in _split3(d_a))
    cum_t = cum.T
    expand = e_ref[...]
    dt_x = sum(_dot(part, expand) for part in _split3(dt))
    cum_x = sum(_dot(part, expand) for part in _split3(cum))
    xs = xs_ref[0]
    xdt = xs * dt_x
    xdt_b = xdt.astype(BF16)
    w_out = (xdt * jnp.exp(cum_x[cl - 1:cl, :] - cum_x)).astype(BF16)
    grow = jnp.exp(cum_x)
    lane = lax.broadcasted_iota(jnp.int32, (cl, LANE), 1)
    first = lax.broadcasted_iota(jnp.int32, (LANE, 1), 0) < SSM_HEAD_DIM
    for g in range(n_groups):
        bg = b_ref[0, :, g * D_STATE:(g + 1) * D_STATE]
        cg = c_ref[0, :, g * D_STATE:(g + 1) * D_STATE]
        cb = _dot_nt(cg, bg)
        for j in range(pairs_per_group):
            pr = g * pairs_per_group + j
            h0, h1 = 2 * pr, 2 * pr + 1
            sl = slice(pr * LANE, (pr + 1) * LANE)
            seg0 = jnp.exp(jnp.where(causal, cum[:, h0:h0 + 1] - cum_t[h0:h0 + 1, :], -jnp.inf))
            seg1 = jnp.exp(jnp.where(causal, cum[:, h1:h1 + 1] - cum_t[h1:h1 + 1, :], -jnp.inf))
            xp = xdt_b[:, sl]
            y_diag = jnp.where(lane < SSM_HEAD_DIM, _dot((cb * seg0).astype(BF16), xp),
                               _dot((cb * seg1).astype(BF16), xp))
            st = st_sc[pr]
            y_sc[:, sl] = y_diag + _dot_nt(cg, st.astype(BF16)) * grow[:, sl]
            decay = jnp.exp(jnp.where(first, cum_t[h0:h0 + 1, cl - 1:cl], cum_t[h1:h1 + 1, cl - 1:cl]))
            st_sc[pr] = st * decay + _dot_tn(w_out[:, sl], bg)
    zz = z_ref[0]
    y = (y_sc[...] + dskip_ref[...] * xs) * _silu(zz)
    y_ref[0] = _rms(y, ng_ref[...]).astype(y_ref.dtype)

    @pl.when(c == pl.num_programs(1) - 1)
    def _():
        hout_ref[0] = st_sc[...]


def _ssd(xs, bm, cm, dt, z, w, n_seq, seq_len):
    d_inner = xs.shape[1]
    gn = bm.shape[1]
    n_pairs = d_inner // LANE
    cl = min(SSD_CHUNK, seq_len)
    nc = seq_len // cl
    r3 = lambda a: a.reshape(n_seq, seq_len, a.shape[1])
    blk = lambda n: pl.BlockSpec((1, cl, n), lambda b, c: (b, c, 0))
    full = lambda a: pl.BlockSpec(a.shape, lambda b, c: (0,) * a.ndim)
    consts = [w["a_log"], w["expand"], w["d_skip"], w["norm_g"]]
    y, h_out = pl.pallas_call(
        functools.partial(_ssd_kernel, cl=cl, n_groups=SSM_GROUPS, pairs_per_group=n_pairs // SSM_GROUPS),
        grid=(n_seq, nc),
        in_specs=[blk(d_inner), blk(gn), blk(gn), blk(LANE), blk(d_inner)] + [full(a) for a in consts],
        out_specs=[blk(d_inner), pl.BlockSpec((1, n_pairs, LANE, D_STATE), lambda b, c: (b, 0, 0, 0))],
        out_shape=[jax.ShapeDtypeStruct((n_seq, seq_len, d_inner), BF16),
                   jax.ShapeDtypeStruct((n_seq, n_pairs, LANE, D_STATE), F32)],
        scratch_shapes=[pltpu.VMEM((n_pairs, LANE, D_STATE), F32), pltpu.VMEM((cl, d_inner), F32)],
        compiler_params=_params(("arbitrary", "arbitrary")),
        name="ssd_scan",
    )(r3(xs), r3(bm), r3(cm), r3(dt), r3(z), *consts)
    return y.reshape(n_seq * seq_len, d_inner), h_out


def _prep_ssm(w_in, conv_w, conv_b, dt_bias, a_log, d_skip, norm_g, w_out):
    d = w_in.shape[0]
    d_inner = w_out.shape[0]
    n_heads = d_inner // SSM_HEAD_DIM
    conv_dim = conv_w.shape[1]
    pad = LANE - n_heads
    w_dt = w_in[:, d_inner + conv_dim:]
    rep = lambda v: jnp.repeat(v, SSM_HEAD_DIM, axis=-1)
    head_of = jnp.arange(d_inner) // SSM_HEAD_DIM
    return dict(
        wz=w_in[:, :d_inner].astype(BF16),
        wx=w_in[:, d_inner:d_inner + conv_dim].astype(BF16),
        wdt=jnp.concatenate([w_dt, jnp.zeros((d, pad), F32)], axis=1).astype(BF16),
        wdt_x=rep(w_dt).astype(BF16),
        conv_w=conv_w, conv_b=conv_b.reshape(1, conv_dim),
        dt_bias=jnp.concatenate([dt_bias, jnp.zeros((pad,), F32)]).reshape(1, LANE),
        dt_bias_x=rep(dt_bias).reshape(1, d_inner),
        a_log=jnp.concatenate([a_log, jnp.zeros((pad,), F32)]).reshape(1, LANE),
        a_log_x=rep(a_log).reshape(1, d_inner),
        expand=(jnp.arange(LANE)[:, None] == head_of[None, :]).astype(BF16),
        d_skip=rep(d_skip).reshape(1, d_inner),
        norm_g=norm_g.reshape(1, d_inner),
        w_out=w_out.astype(BF16),
    )


def _ssm_step_in_kernel(x_ref, g_ref, sh_ref, sc_ref, wz_ref, wx_ref, wdt_ref, cw_ref, cb_ref, dtb_ref, alog_ref,
                        buf_ref, z_ref, xs_ref, b_ref, c_ref, conv_ref, xdt_t_ref, dec_t_ref, *, d_inner):
    h = _modulate(x_ref[...], g_ref[...], sh_ref[0], sc_ref[0]).astype(BF16)
    z_ref[...] = _dot(h, wz_ref[...])
    u = _dot(h, wx_ref[...])
    cw = cw_ref[...]
    y = cw[0:1] * buf_ref[0] + cw[1:2] * buf_ref[1] + cw[2:3] * buf_ref[2] + cw[3:4] * u + cb_ref[...]
    conv_ref[0] = buf_ref[1]
    conv_ref[1] = buf_ref[2]
    conv_ref[2] = u
    xbc = _silu(y)
    gn = (xbc.shape[1] - d_inner) // 2
    xs = xbc[:, :d_inner]
    xs_ref[...] = xs
    b_ref[...] = xbc[:, d_inner:d_inner + gn]
    c_ref[...] = xbc[:, d_inner + gn:]
    dt = _softplus(_dot(h, wdt_ref[...]) + dtb_ref[...])
    xdt_t_ref[...] = (xs * dt).T
    dec_t_ref[...] = jnp.exp(dt * (-jnp.exp(alog_ref[...]))).T


def _ssm_step_in(x, g, shift, scale, buf_t, w):
    n, d = x.shape
    d_inner, conv_dim = w["wz"].shape[1], w["wx"].shape[1]
    gn = (conv_dim - d_inner) // 2
    ws = [w["wz"], w["wx"], w["wdt_x"], w["conv_w"], w["conv_b"], w["dt_bias_x"], w["a_log_x"], buf_t]
    sds = jax.ShapeDtypeStruct
    return pl.pallas_call(
        functools.partial(_ssm_step_in_kernel, d_inner=d_inner),
        out_shape=[sds((n, d_inner), F32), sds((n, d_inner), F32), sds((n, gn), F32), sds((n, gn), F32),
                   sds((CONV_W - 1, n, conv_dim), F32), sds((d_inner, n), F32), sds((d_inner, n), F32)],
        compiler_params=pltpu.CompilerParams(vmem_limit_bytes=VMEM_LIMIT),
        name="ssm_step_in",
    )(x, g, shift.reshape(1, n, d), scale.reshape(1, n, d), *ws)


def _ssm_step_kernel(st_ref, xdt_t_ref, dec_t_ref, b_ref, c_ref, hout_ref, y_t_ref, *, n_groups):
    b = pl.program_id(0)

    @pl.when(b == 0)
    def _():
        y_t_ref[...] = jnp.zeros(y_t_ref.shape, F32)

    rg = st_ref.shape[2] // n_groups
    mine = lax.broadcasted_iota(jnp.int32, (rg, LANE), 1) == b
    for g in range(n_groups):
        rs = slice(g * rg, (g + 1) * rg)
        xcol = jnp.sum(jnp.where(mine, xdt_t_ref[rs, :], 0.0), axis=1, keepdims=True)
        dcol = jnp.sum(jnp.where(mine, dec_t_ref[rs, :], 0.0), axis=1, keepdims=True)
        bg = b_ref[0, :, g * D_STATE:(g + 1) * D_STATE]
        cg = c_ref[0, :, g * D_STATE:(g + 1) * D_STATE]
        s_new = st_ref[0, 0, rs, :] * dcol + xcol * bg
        hout_ref[0, rs, :] = s_new
        ycol = jnp.sum(s_new * cg, axis=1, keepdims=True)
        y_t_ref[rs, :] = jnp.where(mine, ycol, y_t_ref[rs, :])


def _ssm_step(layer, state, xdt_t, dec_t, bm, cm):
    _, n, rows, ns = state.shape
    full = lambda a: pl.BlockSpec(a.shape, lambda b: (0,) * a.ndim)
    seq_row = pl.BlockSpec((1, 1, bm.shape[1]), lambda b: (b, 0, 0))
    bm, cm = bm.reshape(n, 1, -1), cm.reshape(n, 1, -1)
    return pl.pallas_call(
        functools.partial(_ssm_step_kernel, n_groups=SSM_GROUPS),
        grid=(n,),
        in_specs=[pl.BlockSpec((1, 1, rows, ns), lambda b: (layer, b, 0, 0)), full(xdt_t), full(dec_t), seq_row,
                  seq_row],
        out_specs=[pl.BlockSpec((1, rows, ns), lambda b: (b, 0, 0)), pl.BlockSpec((rows, n), lambda b: (0, 0))],
        out_shape=[jax.ShapeDtypeStruct((n, rows, ns), F32), jax.ShapeDtypeStruct((rows, n), F32)],
        compiler_params=_params(("arbitrary",)),
        name="ssm_step",
    )(state, xdt_t, dec_t, bm, cm)


def _ssm_step_out_kernel(y_t_ref, xs_ref, z_ref, dskip_ref, ng_ref, w_ref, x_ref, gate_ref, o_ref):
    y = (y_t_ref[...].T + dskip_ref[...] * xs_ref[...]) * _silu(z_ref[...])
    y = _rms(y, ng_ref[...]).astype(BF16)
    o_ref[...] = x_ref[...] + gate_ref[0] * _dot(y, w_ref[...])


def _ssm_step_out(y_t, xs, z, w, x, gate):
    n, d = x.shape
    return pl.pallas_call(
        _ssm_step_out_kernel,
        out_shape=jax.ShapeDtypeStruct((n, d), F32),
        compiler_params=pltpu.CompilerParams(vmem_limit_bytes=VMEM_LIMIT),
        name="ssm_step_out",
    )(y_t, xs, z, w["d_skip"], w["norm_g"], w["w_out"], x, gate.reshape(1, n, d))


def _final_norm_kernel(x_ref, g_ref, o_ref):
    o_ref[...] = _rms(x_ref[...], g_ref[...])


def _final_norm(x, g):
    t, d = x.shape
    tile = min(FFN_ROW_TILE, t)
    return pl.pallas_call(
        _final_norm_kernel,
        grid=(t // tile,),
        in_specs=[pl.BlockSpec((tile, d), lambda i: (i, 0)), pl.BlockSpec((1, d), lambda i: (0, 0))],
        out_specs=pl.BlockSpec((tile, d), lambda i: (i, 0)),
        out_shape=jax.ShapeDtypeStruct((t, d), F32),
        compiler_params=_params(("arbitrary",)),
        name="final_norm",
    )(x, g.reshape(1, d))


def kernel(x_prompt, x_sample, c_prompt, c_sample, cache_kv_latent, cache_k_rope, page_table, state_ssm, state_conv, w_ada, b_ada, norm_mix_g, norm_ffn_g, norm_final_g, mla_w_in, mla_q_norm_g, mla_kv_norm_g, mla_w_qb, mla_w_kvb, mla_w_o, ssm_w_in, ssm_conv_w, ssm_conv_b, ssm_dt_bias, ssm_a_log, ssm_d, ssm_norm_g, ssm_w_out, ffn_w_gate, ffn_w_up, ffn_w_down, moe_w_router, moe_w_gate, moe_w_up, moe_w_down):
    nb, seq, d = x_prompt.shape
    ns = x_sample.shape[0]
    depth = w_ada.shape[0]
    past_len = page_table.shape[1] * cache_kv_latent.shape[2]
    attn_scale = (QK_NOPE + QK_ROPE) ** -0.5

    xp = x_prompt.reshape(nb * seq, d)
    xs = x_sample.reshape(ns, d)
    mods = _ada(jnp.concatenate([c_prompt, c_sample], axis=0), w_ada, b_ada)
    mods = mods.reshape(depth, nb + ns, 6, d)
    tabs_p = _rope_tables(jnp.arange(seq), attn_scale * LOG2E)
    tabs_s = _rope_tables(jnp.full((ns,), past_len), attn_scale)
    state = state_ssm.reshape(state_ssm.shape[0], ns, -1, D_STATE)

    lat_p, rope_p, lat_s, rope_s = [], [], [], []
    ssm_p, conv_p, ssm_s, conv_s = [], [], [], []
    for i in range(depth):
        j = i // 2
        mp = [mods[i, :nb, k] for k in range(6)]
        ms = [mods[i, nb:, k] for k in range(6)]
        g_mix = norm_mix_g[i].reshape(1, d)
        g_ffn = norm_ffn_g[i].reshape(1, d)
        if i % 2 == 0:
            w = _prep_mla(mla_w_in[j], mla_w_qb[j], mla_w_kvb[j])
            qg = mla_q_norm_g[j].reshape(1, -1)
            kvg = mla_kv_norm_g[j].reshape(1, -1)
            w_o = mla_w_o[j].astype(BF16)
            ckv, kr, q, k, vt = _mla_proj(xp, g_mix, mp[0], mp[1], tabs_p, qg, kvg, w, seq, True)
            o_t = _flash(q, k, vt.reshape(MLA_HEADS, V_DIM, nb * seq), nb, seq)
            xp = _proj_res(o_t.reshape(MLA_HEADS * V_DIM, nb * seq), w_o, xp, mp[2], seq, transposed=True)
            lat_p.append(ckv.reshape(nb, seq, -1))
            rope_p.append(kr.reshape(nb, seq, -1))

            ckv, kr, q = _mla_proj(xs, g_mix, ms[0], ms[1], tabs_s, qg, kvg, w, 1, False)
            kv_rank = ckv.shape[1]
            qlat = _headmat(q, w["wabs"], LANE, kv_rank)
            o_lat = _decode_attn(j, page_table, qlat.reshape(ns, MLA_HEADS, kv_rank), q.reshape(ns, MLA_HEADS, LANE),
                                 ckv.reshape(ns, 1, kv_rank), kr.reshape(ns, 1, QK_ROPE), cache_kv_latent,
                                 cache_k_rope)
            o = _headmat(o_lat.reshape(ns, MLA_HEADS * kv_rank), w["wvbd"], 2 * kv_rank, 2 * V_DIM)
            xs = _proj_res(o, w_o, xs, ms[2], 1)
            lat_s.append(ckv.reshape(ns, 1, -1))
            rope_s.append(kr.reshape(ns, 1, -1))
        else:
            w = _prep_ssm(ssm_w_in[j], ssm_conv_w[j], ssm_conv_b[j], ssm_dt_bias[j], ssm_a_log[j], ssm_d[j],
                          ssm_norm_g[j], ssm_w_out[j])
            z, xin, bm, cm, dt, conv = _ssm_in(xp, g_mix, mp[0], mp[1], w, seq)
            y, h_new = _ssd(xin, bm, cm, dt, z, w, nb, seq)
            xp = _proj_res(y, w["w_out"], xp, mp[2], seq)
            ssm_p.append(h_new.reshape(nb, -1, SSM_HEAD_DIM, D_STATE))
            conv_p.append(conv[:, SUBLANE - (CONV_W - 1):])

            buf_t = state_conv[j].transpose(1, 0, 2)
            z, xin, bm, cm, conv, xdt_t, dec_t = _ssm_step_in(xs, g_mix, ms[0], ms[1], buf_t, w)
            h_new, y_t = _ssm_step(j, state, xdt_t, dec_t, bm, cm)
            xs = _ssm_step_out(y_t, xin, z, w, xs, ms[2])
            ssm_s.append(h_new.reshape(ns, -1, SSM_HEAD_DIM, D_STATE))
            conv_s.append(conv.transpose(1, 0, 2))
        if i % 2 == 0:
            wg = ffn_w_gate[j].astype(BF16)
            wu = ffn_w_up[j].astype(BF16)
            wd = ffn_w_down[j].astype(BF16)
            xp = _ffn(xp, g_ffn, mp[3], mp[4], mp[5], wg, wu, wd, seq)
            xs = _ffn(xs, g_ffn, ms[3], ms[4], ms[5], wg, wu, wd, 1)
        else:
            wg = moe_w_gate[j].astype(BF16)
            wu = moe_w_up[j].astype(BF16)
            wd = moe_w_down[j].astype(BF16)
            xp = _moe(xp, g_ffn, mp[3], mp[4], mp[5], moe_w_router[j], wg, wu, wd, seq)
            xs = _moe(xs, g_ffn, ms[3], ms[4], ms[5], moe_w_router[j], wg, wu, wd, 1)
    y_prompt = _final_norm(xp, norm_final_g).reshape(nb, seq, d)
    y_sample = _final_norm(xs, norm_final_g).reshape(ns, 1, d)
    return (y_prompt, y_sample, jnp.stack(lat_p), jnp.stack(rope_p), jnp.stack(lat_s), jnp.stack(rope_s),
            jnp.stack(ssm_p), jnp.stack(conv_p), jnp.stack(ssm_s), jnp.stack(conv_s))
```

```python
import functools
import math

import jax
import jax.numpy as jnp
from jax import lax
from jax.experimental import pallas as pl
from jax.experimental.pallas import tpu as pltpu

F32 = jnp.float32
BF16 = jnp.bfloat16

RMS_EPS = 1e-6
MLA_HEADS = 16
QK_NOPE = 64
QK_ROPE = 32
V_DIM = 64
ROPE_THETA = 10000.0
PAGE_SIZE = 128
SSM_HEAD_DIM = 64
SSM_GROUPS = 4
D_STATE = 128
CONV_W = 4
SSD_CHUNK = 128
N_EXPERTS = 8

LANE = 128
SUBLANE = 8
VMEM_LIMIT = 56 << 20

ROW_TILE = 256
FFN_ROW_TILE = 512
FFN_COL_TILE = 1408
ATTN_TILE = 512
DECODE_PAGES = 64
NEG = -1e30
LOG2E = math.log2(math.e)


def _params(sem, **kw):
    return pltpu.CompilerParams(dimension_semantics=sem, vmem_limit_bytes=VMEM_LIMIT, **kw)


def _dot(a, b):
    return jnp.dot(a, b, preferred_element_type=F32)


def _dot_nt(a, b):
    return lax.dot_general(a, b, (((1,), (1,)), ((), ())), preferred_element_type=F32)


def _dot_tn(a, b):
    return lax.dot_general(a, b, (((0,), (0,)), ((), ())), preferred_element_type=F32)


def _split3(v):
    hi = v.astype(BF16)
    r = v - hi.astype(F32)
    mid = r.astype(BF16)
    lo = (r - mid.astype(F32)).astype(BF16)
    return hi, mid, lo


def _silu(x):
    return x * jax.nn.sigmoid(x)


def _rms(x, g):
    return (x * lax.rsqrt(jnp.mean(x * x, axis=-1, keepdims=True) + RMS_EPS)) * g


def _modulate(x, g, shift, scale):
    return _rms(x, g) * (1.0 + scale) + shift


def _mod_specs(n_rows, tile, rows_per_seq, d):
    if rows_per_seq == 1:
        return pl.BlockSpec((1, tile, d), lambda i, *_: (i, 0, 0))
    tiles_per_seq = rows_per_seq // tile
    return pl.BlockSpec((1, 1, d), lambda i, *_: (i // tiles_per_seq, 0, 0))


def _mod_arr(m, rows_per_seq, tile):
    if rows_per_seq == 1:
        return m.reshape(m.shape[0] // tile, tile, m.shape[1])
    return m.reshape(m.shape[0], 1, m.shape[1])


def _ada_kernel(c_ref, w_ref, b_ref, o_ref):
    s = _silu(c_ref[...]).astype(BF16)
    o_ref[0] = _dot(s, w_ref[0].astype(BF16)) + b_ref[0]


def _ada(c_all, w_ada, b_ada):
    depth, d, d6 = w_ada.shape
    n = c_all.shape[0]
    return pl.pallas_call(
        _ada_kernel,
        grid=(depth, d6 // d),
        in_specs=[pl.BlockSpec((n, d), lambda i, k: (0, 0)),
                  pl.BlockSpec((1, d, d), lambda i, k: (i, 0, k)),
                  pl.BlockSpec((1, 1, d), lambda i, k: (i, 0, k))],
        out_specs=pl.BlockSpec((1, n, d), lambda i, k: (i, 0, k)),
        out_shape=jax.ShapeDtypeStruct((depth, n, d6), F32),
        compiler_params=_params(("arbitrary", "arbitrary")),
        name="adaln",
    )(c_all, w_ada, b_ada.reshape(depth, 1, d6))


def _mla_proj_kernel(x_ref, g_ref, sh_ref, sc_ref, ct_ref, st_ref, cts_ref, sts_ref, wcq_ref, wckv_ref,
                     wkra_ref, wkrb_ref, qg_ref, kvg_ref, wqa_ref, wqb_ref, *rest, heads, with_kv):
    if with_kv:
        wka_ref, wvt_ref, ckv_ref, kr_ref, q_ref, k_ref, vt_ref = rest
    else:
        ckv_ref, kr_ref, q_ref = rest
    h = _modulate(x_ref[...], g_ref[...], sh_ref[0], sc_ref[0]).astype(BF16)
    cq = _rms(_dot(h, wcq_ref[...]), qg_ref[...]).astype(BF16)
    ckv = _rms(_dot(h, wckv_ref[...]), kvg_ref[...])
    kr = _dot(h, wkra_ref[...]) * ct_ref[...] + _dot(h, wkrb_ref[...]) * st_ref[...]
    ckv_ref[...] = ckv
    kr_ref[...] = kr[:, :QK_ROPE]
    rep = lambda t: jnp.concatenate([t] * heads, axis=1)
    q = (_dot(cq, wqa_ref[...]) * rep(cts_ref[...]) + _dot(cq, wqb_ref[...]) * rep(sts_ref[...])).astype(BF16)
    if not with_kv:
        q_ref[...] = q
        return
    ckvb = ckv.astype(BF16)
    k = (_dot(ckvb, wka_ref[...]) + rep(kr)).astype(BF16)
    for hd in range(heads):
        q_ref[hd] = q[:, hd * LANE:(hd + 1) * LANE]
        k_ref[hd] = k[:, hd * LANE:(hd + 1) * LANE]
    vt_ref[...] = _dot_nt(wvt_ref[...], ckvb).astype(BF16)


def _rot_cols(w):
    half = w.shape[-1] // 2
    return jnp.concatenate([-w[..., half:], w[..., :half]], axis=-1)


def _prep_mla(w_in, w_qb, w_kvb):
    d = w_in.shape[0]
    q_rank = w_qb.shape[0]
    kv_rank = w_kvb.shape[0]
    hd = LANE
    pad = hd - QK_ROPE - QK_NOPE
    w_kr = w_in[:, q_rank + kv_rank:]
    zk = jnp.zeros((d, hd - QK_ROPE), F32)
    wq = w_qb.reshape(q_rank, MLA_HEADS, QK_NOPE + QK_ROPE)
    q_nope, q_pe = wq[..., :QK_NOPE], wq[..., QK_NOPE:]
    zq = jnp.zeros((q_rank, MLA_HEADS, pad), F32)
    wkv = w_kvb.reshape(kv_rank, MLA_HEADS, QK_NOPE + V_DIM)
    k_nope, v = wkv[..., :QK_NOPE], wkv[..., QK_NOPE:]
    return dict(
        wcq=w_in[:, :q_rank].astype(BF16),
        wckv=w_in[:, q_rank:q_rank + kv_rank].astype(BF16),
        wkra=jnp.concatenate([w_kr, zk], axis=1).astype(BF16),
        wkrb=jnp.concatenate([_rot_cols(w_kr), zk], axis=1).astype(BF16),
        wqa=jnp.concatenate([q_pe, q_nope, zq], axis=-1).reshape(q_rank, MLA_HEADS * hd).astype(BF16),
        wqb=jnp.concatenate([_rot_cols(q_pe), jnp.zeros((q_rank, MLA_HEADS, hd - QK_ROPE), F32)],
                            axis=-1).reshape(q_rank, MLA_HEADS * hd).astype(BF16),
        wka=jnp.concatenate([jnp.zeros((kv_rank, MLA_HEADS, QK_ROPE), F32), k_nope,
                             jnp.zeros((kv_rank, MLA_HEADS, pad), F32)],
                            axis=-1).reshape(kv_rank, MLA_HEADS * hd).astype(BF16),
        wvt=v.reshape(kv_rank, MLA_HEADS * V_DIM).T.astype(BF16),
        wabs=jnp.concatenate([jnp.zeros((MLA_HEADS, QK_ROPE, kv_rank), F32), k_nope.transpose(1, 2, 0),
                              jnp.zeros((MLA_HEADS, pad, kv_rank), F32)], axis=1).astype(BF16),
        wvbd=_block_diag_pairs(v.transpose(1, 0, 2)).astype(BF16),
    )


def _block_diag_pairs(v):
    h, r, dv = v.shape
    v = v.reshape(h // 2, 2, r, dv)
    z = jnp.zeros((h // 2, r, dv), v.dtype)
    top = jnp.concatenate([v[:, 0], z], axis=-1)
    bot = jnp.concatenate([z, v[:, 1]], axis=-1)
    return jnp.concatenate([top, bot], axis=1)


def _rope_tables(pos, scale):
    half = QK_ROPE // 2
    inv = ROPE_THETA ** (-jnp.arange(half, dtype=F32) / half)
    ang = pos.astype(F32)[:, None] * inv[None, :]
    cos, sin = jnp.cos(ang), jnp.sin(ang)
    n = pos.shape[0]
    ct = jnp.concatenate([cos, cos, jnp.ones((n, LANE - QK_ROPE), F32)], axis=1)
    st = jnp.concatenate([sin, sin, jnp.zeros((n, LANE - QK_ROPE), F32)], axis=1)
    return ct, st, ct * scale, st * scale


def _mla_proj(x, g, shift, scale, tabs, qg, kvg, w, rows_per_seq, with_kv):
    t, d = x.shape
    tile = min(ROW_TILE, t)
    n_tab_tiles = tabs[0].shape[0] // tile
    heads = MLA_HEADS
    kv_rank = w["wckv"].shape[1]
    full = lambda a: pl.BlockSpec(a.shape, lambda i: (0,) * a.ndim)
    row = lambda n: pl.BlockSpec((tile, n), lambda i: (i, 0))
    tab = pl.BlockSpec((tile, LANE), lambda i: (i % n_tab_tiles, 0))
    mod = _mod_specs(t, tile, rows_per_seq, d)
    ws = [w["wcq"], w["wckv"], w["wkra"], w["wkrb"], qg, kvg, w["wqa"], w["wqb"]]
    outs = [jax.ShapeDtypeStruct((t, kv_rank), F32), jax.ShapeDtypeStruct((t, QK_ROPE), F32)]
    out_specs = [row(kv_rank), row(QK_ROPE)]
    if with_kv:
        ws += [w["wka"], w["wvt"]]
        head_major = pl.BlockSpec((heads, tile, LANE), lambda i: (0, i, 0))
        outs += [jax.ShapeDtypeStruct((heads, t, LANE), BF16), jax.ShapeDtypeStruct((heads, t, LANE), BF16),
                 jax.ShapeDtypeStruct((heads * V_DIM, t), BF16)]
        out_specs += [head_major, head_major, pl.BlockSpec((heads * V_DIM, tile), lambda i: (0, i))]
    else:
        outs.append(jax.ShapeDtypeStruct((t, heads * LANE), BF16))
        out_specs.append(row(heads * LANE))
    return pl.pallas_call(
        functools.partial(_mla_proj_kernel, heads=heads, with_kv=with_kv),
        grid=(t // tile,),
        in_specs=[row(d), full(g), mod, mod, tab, tab, tab, tab] + [full(a) for a in ws],
        out_specs=out_specs,
        out_shape=outs,
        compiler_params=_params(("arbitrary",)),
        name="mla_proj",
    )(x, g, _mod_arr(shift, rows_per_seq, tile), _mod_arr(scale, rows_per_seq, tile), *tabs, *ws)


def _flash_kernel(qi_tab, ki_tab, q_ref, k_ref, vt_ref, o_ref, m_sc, l_sc, acc_sc, s0_sc, s1_sc, *, tile, heads):
    t = pl.program_id(1)
    qi = qi_tab[t]
    ki = ki_tab[t]

    @pl.when(ki == 0)
    def _():
        m_sc[...] = jnp.full(m_sc.shape, NEG, F32)
        l_sc[...] = jnp.zeros(l_sc.shape, F32)
        acc_sc[...] = jnp.zeros(acc_sc.shape, F32)

    def scores(h, s_sc):
        s_sc[...] = _dot_nt(k_ref[h], q_ref[h])

    def sweep(diagonal):
        if diagonal:
            visible = (lax.broadcasted_iota(jnp.int32, (tile, tile), 0)
                       <= lax.broadcasted_iota(jnp.int32, (tile, tile), 1))

        def softmax_pv(h, s_sc):
            s = s_sc[...]
            if diagonal:
                s = jnp.where(visible, s, NEG)
            m_prev = m_sc[h]
            m_new = jnp.maximum(m_prev, jnp.max(s, axis=0, keepdims=True))
            alpha = jnp.exp2(m_prev - m_new)
            p = jnp.exp2(s - m_new)
            l_new = alpha * l_sc[h] + jnp.sum(p, axis=0, keepdims=True)
            acc = alpha * acc_sc[h] + _dot(vt_ref[h], p.astype(BF16))
            if diagonal:
                o_ref[h] = (acc / l_new).astype(o_ref.dtype)
            else:
                m_sc[h] = m_new
                l_sc[h] = l_new
                acc_sc[h] = acc

        scores(0, s0_sc)

        def pair(i, carry):
            h0 = 2 * i
            scores(h0 + 1, s1_sc)
            softmax_pv(h0, s0_sc)
            scores(jnp.minimum(h0 + 2, heads - 1), s0_sc)
            softmax_pv(h0 + 1, s1_sc)
            return carry

        lax.fori_loop(0, heads // 2, pair, 0)

    @pl.when(ki < qi)
    def _():
        sweep(False)

    @pl.when(ki == qi)
    def _():
        sweep(True)


def _flash(q, k, vt, n_seq, seq_len):
    heads = q.shape[0]
    tile = min(ATTN_TILE, seq_len)
    nq = seq_len // tile
    pairs = [(i, j) for i in range(nq) for j in range(i + 1)]
    qi_tab = jnp.array([p[0] for p in pairs], jnp.int32)
    ki_tab = jnp.array([p[1] for p in pairs], jnp.int32)
    grid_spec = pltpu.PrefetchScalarGridSpec(
        num_scalar_prefetch=2,
        grid=(n_seq, len(pairs)),
        in_specs=[pl.BlockSpec((heads, tile, LANE), lambda b, t, qt, kt: (0, b * nq + qt[t], 0)),
                  pl.BlockSpec((heads, tile, LANE), lambda b, t, qt, kt: (0, b * nq + kt[t], 0)),
                  pl.BlockSpec((heads, V_DIM, tile), lambda b, t, qt, kt: (0, 0, b * nq + kt[t]))],
        out_specs=pl.BlockSpec((heads, V_DIM, tile), lambda b, t, qt, kt: (0, 0, b * nq + qt[t])),
        scratch_shapes=[pltpu.VMEM((heads, 1, tile), F32), pltpu.VMEM((heads, 1, tile), F32),
                        pltpu.VMEM((heads, V_DIM, tile), F32), pltpu.VMEM((tile, tile), F32),
                        pltpu.VMEM((tile, tile), F32)],
    )
    return pl.pallas_call(
        functools.partial(_flash_kernel, tile=tile, heads=heads),
        grid_spec=grid_spec,
        out_shape=jax.ShapeDtypeStruct(vt.shape, BF16),
        compiler_params=_params(("arbitrary", "arbitrary")),
        name="flash_attn",
    )(qi_tab, ki_tab, q, k, vt)


def _headmat_kernel(a_ref, w_ref, o_ref):
    o_ref[...] = _dot(a_ref[...], w_ref[0]).astype(o_ref.dtype)


def _headmat(a, w, in_w, out_w):
    n, rows = w.shape[0], a.shape[0]
    return pl.pallas_call(
        _headmat_kernel,
        grid=(n,),
        in_specs=[pl.BlockSpec((rows, in_w), lambda i: (0, i)),
                  pl.BlockSpec((1, in_w, out_w), lambda i: (i, 0, 0))],
        out_specs=pl.BlockSpec((rows, out_w), lambda i: (0, i)),
        out_shape=jax.ShapeDtypeStruct((rows, n * out_w), BF16),
        compiler_params=_params(("arbitrary",)),
        name="head_matmul",
    )(a, w)


def _decode_kernel(pt_ref, qlat_ref, q_ref, ckv_ref, kr_ref, *rest, pages):
    lat_refs, rope_refs = rest[:pages], rest[pages:2 * pages]
    o_ref, kcat, rcat, m_sc, l_sc, acc_sc = rest[2 * pages:]
    c = pl.program_id(1)

    @pl.when(c == 0)
    def _():
        m_sc[...] = jnp.full(m_sc.shape, NEG, F32)
        l_sc[...] = jnp.zeros(l_sc.shape, F32)
        acc_sc[...] = jnp.zeros(acc_sc.shape, F32)

    for i in range(pages):
        kcat[i * PAGE_SIZE:(i + 1) * PAGE_SIZE, :] = lat_refs[i][0, 0].astype(BF16)
        rcat[:, i * PAGE_SIZE:(i + 1) * PAGE_SIZE] = rope_refs[i][0, 0].astype(BF16)
    ql = qlat_ref[0]
    qp = q_ref[0][:, :QK_ROPE]
    keys = kcat[...]
    s = _dot_nt(ql, keys) + _dot(qp, rcat[...])
    m = m_sc[...]
    m_new = jnp.maximum(m, jnp.max(s, axis=1, keepdims=True))
    alpha = jnp.exp(m - m_new)
    p = jnp.exp(s - m_new)
    l = alpha * l_sc[...] + jnp.sum(p, axis=1, keepdims=True)
    acc = alpha * acc_sc[...] + _dot(p.astype(BF16), keys)
    m_sc[...], l_sc[...], acc_sc[...] = m_new, l, acc

    @pl.when(c == pl.num_programs(1) - 1)
    def _():
        kl = ckv_ref[0].astype(BF16).astype(F32)
        kp = kr_ref[0].astype(BF16).astype(F32)
        s1 = (jnp.sum(ql.astype(F32) * kl, axis=1, keepdims=True)
              + jnp.sum(qp.astype(F32) * kp, axis=1, keepdims=True))
        m1 = jnp.maximum(m_new, s1)
        a1 = jnp.exp(m_new - m1)
        p1 = jnp.exp(s1 - m1)
        o_ref[0] = ((a1 * acc + p1.astype(BF16).astype(F32) * kl) / (a1 * l + p1)).astype(o_ref.dtype)


def _decode_attn(layer, page_table, qlat, q, ckv, kr, cache_lat, cache_rope):
    b, heads, rank = qlat.shape
    n_pages = page_table.shape[1]
    pages = min(DECODE_PAGES, n_pages)
    lat_spec = lambda i: pl.BlockSpec(
        (1, 1, PAGE_SIZE, rank), lambda b_, c, pt: (layer, pt[b_ * n_pages + c * pages + i], 0, 0))
    rope_spec = lambda i: pl.BlockSpec(
        (1, 1, QK_ROPE, PAGE_SIZE), lambda b_, c, pt: (layer, pt[b_ * n_pages + c * pages + i], 0, 0))
    seq = lambda n, w: pl.BlockSpec((1, n, w), lambda b_, c, pt: (b_, 0, 0))
    grid_spec = pltpu.PrefetchScalarGridSpec(
        num_scalar_prefetch=1,
        grid=(b, n_pages // pages),
        in_specs=[seq(heads, rank), seq(heads, LANE), seq(1, rank), seq(1, QK_ROPE)]
        + [lat_spec(i) for i in range(pages)] + [rope_spec(i) for i in range(pages)],
        out_specs=seq(heads, rank),
        scratch_shapes=[pltpu.VMEM((pages * PAGE_SIZE, rank), BF16), pltpu.VMEM((QK_ROPE, pages * PAGE_SIZE), BF16),
                        pltpu.VMEM((heads, 1), F32), pltpu.VMEM((heads, 1), F32), pltpu.VMEM((heads, rank), F32)],
    )
    return pl.pallas_call(
        functools.partial(_decode_kernel, pages=pages),
        grid_spec=grid_spec,
        out_shape=jax.ShapeDtypeStruct((b, heads, rank), BF16),
        compiler_params=_params(("arbitrary", "arbitrary")),
        name="decode_attn",
    )(page_table.reshape(-1), qlat, q, ckv, kr, *([cache_lat] * pages), *([cache_rope] * pages))


def _proj_res_kernel(a_ref, w_ref, x_ref, gate_ref, o_ref, *, transposed):
    y = _dot_tn(a_ref[...], w_ref[...]) if transposed else _dot(a_ref[...], w_ref[...])
    o_ref[...] = x_ref[...] + gate_ref[0] * y


def _proj_res(a, w, x, gate, rows_per_seq, transposed=False):
    t, d = x.shape
    k = w.shape[0]
    tile = min(FFN_ROW_TILE, t)
    a_spec = pl.BlockSpec((k, tile), lambda i: (0, i)) if transposed else pl.BlockSpec((tile, k), lambda i: (i, 0))
    return pl.pallas_call(
        functools.partial(_proj_res_kernel, transposed=transposed),
        grid=(t // tile,),
        in_specs=[a_spec, pl.BlockSpec((k, d), lambda i: (0, 0)),
                  pl.BlockSpec((tile, d), lambda i: (i, 0)), _mod_specs(t, tile, rows_per_seq, d)],
        out_specs=pl.BlockSpec((tile, d), lambda i: (i, 0)),
        out_shape=jax.ShapeDtypeStruct((t, d), F32),
        compiler_params=_params(("arbitrary",)),
        name="proj_residual",
    )(a, w, x, _mod_arr(gate, rows_per_seq, tile))


def _ffn_kernel(x_ref, g_ref, sh_ref, sc_ref, gate_ref, wg_ref, wu_ref, wd_ref, o_ref, h_sc, acc_sc):
    f = pl.program_id(1)

    @pl.when(f == 0)
    def _():
        h_sc[...] = _modulate(x_ref[...], g_ref[...], sh_ref[0], sc_ref[0]).astype(BF16)
        acc_sc[...] = jnp.zeros(acc_sc.shape, F32)

    h = h_sc[...]
    a = (_silu(_dot(h, wg_ref[...])) * _dot(h, wu_ref[...])).astype(BF16)
    acc_sc[...] += _dot(a, wd_ref[...])

    @pl.when(f == pl.num_programs(1) - 1)
    def _():
        o_ref[...] = x_ref[...] + gate_ref[0] * acc_sc[...]


def _col_tile(ff):
    return FFN_COL_TILE if ff % FFN_COL_TILE == 0 else ff


def _ffn(x, g, shift, scale, gate, wg, wu, wd, rows_per_seq):
    t, d = x.shape
    ff = wg.shape[1]
    tile = min(FFN_ROW_TILE, t)
    tf = _col_tile(ff)
    mod = _mod_specs(t, tile, rows_per_seq, d)
    row = pl.BlockSpec((tile, d), lambda i, f: (i, 0))
    return pl.pallas_call(
        _ffn_kernel,
        grid=(t // tile, ff // tf),
        in_specs=[row, pl.BlockSpec(g.shape, lambda i, f: (0, 0)), mod, mod, mod,
                  pl.BlockSpec((d, tf), lambda i, f: (0, f)), pl.BlockSpec((d, tf), lambda i, f: (0, f)),
                  pl.BlockSpec((tf, d), lambda i, f: (f, 0))],
        out_specs=row,
        out_shape=jax.ShapeDtypeStruct((t, d), F32),
        scratch_shapes=[pltpu.VMEM((tile, d), BF16), pltpu.VMEM((tile, d), F32)],
        compiler_params=_params(("arbitrary", "arbitrary")),
        name="swiglu",
    )(x, g, *[_mod_arr(m, rows_per_seq, tile) for m in (shift, scale, gate)], wg, wu, wd)


_E1, _E2, _R1, _R2, _W1, _W2 = range(6)


def _router_kernel(x_ref, g_ref, sh_ref, sc_ref, wr_ref, meta_ref, cnt_ref, h3_ref, cnt_sc):
    i = pl.program_id(0)

    @pl.when(i == 0)
    def _():
        cnt_sc[...] = jnp.zeros(cnt_sc.shape, F32)

    h = _modulate(x_ref[...], g_ref[...], sh_ref[0], sc_ref[0])
    for j in range(h3_ref.shape[1]):
        h3_ref[:, j, :] = h[:, j * LANE:(j + 1) * LANE]
    hi = h.astype(BF16)
    lo = (h - hi.astype(F32)).astype(BF16)
    w = wr_ref[...]
    whi = w.astype(BF16)
    wlo = (w - whi.astype(F32)).astype(BF16)
    logits = _dot(hi, whi) + _dot(lo, whi) + _dot(hi, wlo)
    lane = lax.broadcasted_iota(jnp.int32, logits.shape, 1).astype(F32)
    logits = jnp.where(lane < N_EXPERTS, logits, NEG)
    p = jnp.exp(logits - jnp.max(logits, axis=1, keepdims=True))
    p = p / jnp.sum(p, axis=1, keepdims=True)
    p1 = jnp.max(p, axis=1, keepdims=True)
    i1 = jnp.min(jnp.where(p == p1, lane, float(LANE)), axis=1, keepdims=True)
    rest = jnp.where(lane == i1, -1.0, p)
    p2 = jnp.max(rest, axis=1, keepdims=True)
    i2 = jnp.min(jnp.where(rest == p2, lane, float(LANE)), axis=1, keepdims=True)
    den = p1 + p2
    hit1, hit2 = lane == i1, lane == i2
    onehot = jnp.where(hit1 | hit2, 1.0, 0.0)
    tile = onehot.shape[0]
    ltri = jnp.where(lax.broadcasted_iota(jnp.int32, (tile, tile), 0)
                     >= lax.broadcasted_iota(jnp.int32, (tile, tile), 1), 1.0, 0.0).astype(BF16)
    incl = _dot(ltri, onehot.astype(BF16))
    before = cnt_sc[...] + incl - onehot
    r1 = jnp.sum(jnp.where(hit1, before, 0.0), axis=1, keepdims=True)
    r2 = jnp.sum(jnp.where(hit2, before, 0.0), axis=1, keepdims=True)
    cnt = cnt_sc[...] + incl[tile - 1:tile, :]
    cnt_sc[...] = cnt
    cnt_ref[...] = jnp.broadcast_to(cnt, cnt_ref.shape)
    rec = jnp.zeros(logits.shape, F32)
    for ln, val in ((_E1, i1), (_E2, i2), (_R1, r1), (_R2, r2), (_W1, p1 / den), (_W2, p2 / den)):
        rec = jnp.where(lane == ln, val, rec)
    meta_ref[...] = rec


def _router(x, g, shift, scale, w_router, rows_per_seq):
    t, d = x.shape
    tile = min(FFN_ROW_TILE, t)
    wr = jnp.concatenate([w_router, jnp.zeros((d, LANE - w_router.shape[1]), F32)], axis=1)
    mod = _mod_specs(t, tile, rows_per_seq, d)
    return pl.pallas_call(
        _router_kernel,
        grid=(t // tile,),
        in_specs=[pl.BlockSpec((tile, d), lambda i: (i, 0)), pl.BlockSpec(g.shape, lambda i: (0, 0)), mod, mod,
                  pl.BlockSpec((d, LANE), lambda i: (0, 0))],
        out_specs=[pl.BlockSpec((tile, LANE), lambda i: (i, 0)), pl.BlockSpec((SUBLANE, LANE), lambda i: (0, 0)),
                   pl.BlockSpec((tile, d // LANE, LANE), lambda i: (i, 0, 0))],
        out_shape=[jax.ShapeDtypeStruct((t, LANE), F32), jax.ShapeDtypeStruct((SUBLANE, LANE), F32),
                   jax.ShapeDtypeStruct((t, d // LANE, LANE), F32)],
        scratch_shapes=[pltpu.VMEM((1, LANE), F32)],
        compiler_params=_params(("arbitrary",)),
        name="router",
    )(x, g, _mod_arr(shift, rows_per_seq, tile), _mod_arr(scale, rows_per_seq, tile), wr)


def _row_copy(src, dst, sem):
    return pltpu.make_async_copy(src, dst, sem)


def _dispatch_kernel(pos_ref, h_ref, init_ref, xs_ref, sem, *, tile):
    del init_ref
    base = pl.program_id(0) * tile * 2

    def issue(t, carry):
        _row_copy(h_ref.at[t], xs_ref.at[pos_ref[base + 2 * t]], sem).start(priority=0)
        _row_copy(h_ref.at[t], xs_ref.at[pos_ref[base + 2 * t + 1]], sem).start(priority=1)
        return carry

    lax.fori_loop(0, tile, issue, 0)
    for _ in range(2):
        _row_copy(h_ref, xs_ref.at[pl.ds(0, tile)], sem).wait()


def _dispatch(pos, h3, n_rows, tile):
    t, chunks, _ = h3.shape
    grid_spec = pltpu.PrefetchScalarGridSpec(
        num_scalar_prefetch=1,
        grid=(t // tile,),
        in_specs=[pl.BlockSpec((tile, chunks, LANE), lambda i, p: (i, 0, 0)), pl.BlockSpec(memory_space=pl.ANY)],
        out_specs=pl.BlockSpec(memory_space=pl.ANY),
        scratch_shapes=[pltpu.SemaphoreType.DMA(())],
    )
    return pl.pallas_call(
        functools.partial(_dispatch_kernel, tile=tile),
        grid_spec=grid_spec,
        out_shape=jax.ShapeDtypeStruct((n_rows, chunks, LANE), F32),
        input_output_aliases={2: 0},
        compiler_params=_params(("arbitrary",), disable_bounds_checks=True),
        name="moe_dispatch",
    )(pos, h3, jnp.zeros((n_rows, chunks, LANE), F32))


def _moe_ffn_kernel(blk_ref, exp_ref, nt_ref, x_ref, wg_ref, wu_ref, wd_ref, y_ref, h_sc, acc_sc):
    i, f = pl.program_id(0), pl.program_id(1)
    chunks = x_ref.shape[1]

    @pl.when(i < nt_ref[0])
    def _():
        @pl.when(f == 0)
        def _():
            h_sc[...] = jnp.concatenate([x_ref[:, j, :] for j in range(chunks)], axis=1).astype(BF16)
            acc_sc[...] = jnp.zeros(acc_sc.shape, F32)

        h = h_sc[...]
        a = (_silu(_dot(h, wg_ref[0])) * _dot(h, wu_ref[0])).astype(BF16)
        acc_sc[...] += _dot(a, wd_ref[0])

        @pl.when(f == pl.num_programs(1) - 1)
        def _():
            for j in range(chunks):
                y_ref[:, j, :] = acc_sc[:, j * LANE:(j + 1) * LANE]


def _moe_ffn(blk_tab, exp_tab, n_tiles, xs, wg, wu, wd, tile):
    n_rows, chunks, _ = xs.shape
    d = chunks * LANE
    ff = wg.shape[2]
    tf = _col_tile(ff)
    nf = ff // tf
    col = lambda i, f, nt: jnp.where(i < nt[0], f, nf - 1)
    rows = pl.BlockSpec((tile, chunks, LANE), lambda i, f, bt, et, nt: (bt[i], 0, 0))
    grid_spec = pltpu.PrefetchScalarGridSpec(
        num_scalar_prefetch=3,
        grid=(n_rows // tile, nf),
        in_specs=[rows,
                  pl.BlockSpec((1, d, tf), lambda i, f, bt, et, nt: (et[i], 0, col(i, f, nt))),
                  pl.BlockSpec((1, d, tf), lambda i, f, bt, et, nt: (et[i], 0, col(i, f, nt))),
                  pl.BlockSpec((1, tf, d), lambda i, f, bt, et, nt: (et[i], col(i, f, nt), 0))],
        out_specs=rows,
        scratch_shapes=[pltpu.VMEM((tile, d), BF16), pltpu.VMEM((tile, d), F32)],
    )
    return pl.pallas_call(
        _moe_ffn_kernel,
        grid_spec=grid_spec,
        out_shape=jax.ShapeDtypeStruct(xs.shape, F32),
        compiler_params=_params(("arbitrary", "arbitrary")),
        name="moe_swiglu",
    )(blk_tab, exp_tab, n_tiles, xs, wg, wu, wd)


def _combine_kernel(pos_ref, meta_ref, x_ref, gate_ref, ys_ref, o_ref, buf, sem, *, tile):
    i = pl.program_id(0)
    slot = i % 2

    def gather(step, into):
        base = step * tile * 2

        def issue(t, carry):
            _row_copy(ys_ref.at[pos_ref[base + 2 * t]], buf.at[into, 0, t], sem.at[into]).start(priority=0)
            _row_copy(ys_ref.at[pos_ref[base + 2 * t + 1]], buf.at[into, 1, t], sem.at[into]).start(priority=1)
            return carry

        lax.fori_loop(0, tile, issue, 0)

    @pl.when(i == 0)
    def _():
        gather(0, 0)

    @pl.when(i + 1 < pl.num_programs(0))
    def _():
        gather(i + 1, 1 - slot)

    for k in range(2):
        _row_copy(ys_ref.at[pl.ds(0, tile)], buf.at[slot, k], sem.at[slot]).wait()
    meta = meta_ref[...]
    w1, w2 = meta[:, _W1:_W1 + 1], meta[:, _W2:_W2 + 1]
    gate = gate_ref[0]
    for j in range(buf.shape[3]):
        sl = slice(j * LANE, (j + 1) * LANE)
        o_ref[:, sl] = x_ref[:, sl] + gate[:, sl] * (w1 * buf[slot, 0, :, j, :] + w2 * buf[slot, 1, :, j, :])


def _combine(pos, meta, x, gate, ys, rows_per_seq, tile):
    t, d = x.shape
    chunks = d // LANE
    row = lambda n: pl.BlockSpec((tile, n), lambda i, p: (i, 0))
    grid_spec = pltpu.PrefetchScalarGridSpec(
        num_scalar_prefetch=1,
        grid=(t // tile,),
        in_specs=[row(LANE), row(d), _mod_specs(t, tile, rows_per_seq, d), pl.BlockSpec(memory_space=pl.ANY)],
        out_specs=row(d),
        scratch_shapes=[pltpu.VMEM((2, 2, tile, chunks, LANE), F32), pltpu.SemaphoreType.DMA((2,))],
    )
    return pl.pallas_call(
        functools.partial(_combine_kernel, tile=tile),
        grid_spec=grid_spec,
        out_shape=jax.ShapeDtypeStruct((t, d), F32),
        compiler_params=_params(("arbitrary",), disable_bounds_checks=True),
        name="moe_combine",
    )(pos, meta, x, _mod_arr(gate, rows_per_seq, tile), ys)


def _moe(x, g, shift, scale, gate, w_router, wg, wu, wd, rows_per_seq):
    t, d = x.shape
    tile = min(FFN_ROW_TILE, t)
    meta, counts, h3 = _router(x, g, shift, scale, w_router, rows_per_seq)
    n_max = 2 * t // tile + N_EXPERTS
    cnt = counts[0, :N_EXPERTS].astype(jnp.int32)
    tiles = (cnt + tile - 1) // tile
    ends = jnp.cumsum(tiles)
    n_tiles = ends[-1]
    blk_tab = jnp.minimum(jnp.arange(n_max, dtype=jnp.int32), n_tiles - 1)
    exp_tab = jnp.sum(blk_tab[:, None] >= ends[None, :], axis=1).astype(jnp.int32)
    offs = (ends - tiles) * tile
    experts = meta[:, _E1:_E2 + 1].astype(jnp.int32)
    ranks = meta[:, _R1:_R2 + 1].astype(jnp.int32)
    pos = (offs[experts] + ranks).reshape(-1)
    xs = _dispatch(pos, h3, n_max * tile, tile)
    ys = _moe_ffn(blk_tab, exp_tab, n_tiles.reshape(1).astype(jnp.int32), xs, wg, wu, wd, tile)
    return _combine(pos, meta, x, gate, ys, rows_per_seq, tile)


def _softplus(x):
    return jnp.maximum(x, 0.0) + jnp.log1p(jnp.exp(-jnp.abs(x)))


def _ssm_in_kernel(x_ref, g_ref, sh_ref, sc_ref, wz_ref, wx_ref, wdt_ref, cw_ref, cb_ref, dtb_ref,
                   z_ref, xs_ref, b_ref, c_ref, dt_ref, conv_ref, ext_sc, *, tile, tiles_per_seq, d_inner, n_heads):
    i = pl.program_id(0)

    @pl.when(i % tiles_per_seq == 0)
    def _():
        ext_sc[0:SUBLANE, :] = jnp.zeros((SUBLANE, ext_sc.shape[1]), F32)

    h = _modulate(x_ref[...], g_ref[...], sh_ref[0], sc_ref[0]).astype(BF16)
    z_ref[...] = _dot(h, wz_ref[...])
    u = _dot(h, wx_ref[...])
    ext_sc[SUBLANE:SUBLANE + tile, :] = u
    cw = cw_ref[...]
    y = cw[0:1] * ext_sc[SUBLANE - 3:SUBLANE - 3 + tile, :]
    y = y + cw[1:2] * ext_sc[SUBLANE - 2:SUBLANE - 2 + tile, :]
    y = y + cw[2:3] * ext_sc[SUBLANE - 1:SUBLANE - 1 + tile, :]
    y = y + cw[3:4] * u + cb_ref[...]
    tail = ext_sc[tile:tile + SUBLANE, :]
    conv_ref[0] = tail
    ext_sc[0:SUBLANE, :] = tail
    xbc = _silu(y)
    gn = (xbc.shape[1] - d_inner) // 2
    xs_ref[...] = xbc[:, :d_inner]
    b_ref[...] = xbc[:, d_inner:d_inner + gn].astype(BF16)
    c_ref[...] = xbc[:, d_inner + gn:].astype(BF16)
    dt = _softplus(_dot(h, wdt_ref[...]) + dtb_ref[...])
    lane = lax.broadcasted_iota(jnp.int32, dt.shape, 1)
    dt_ref[...] = jnp.where(lane < n_heads, dt, 0.0)


def _ssm_in(x, g, shift, scale, w, rows_per_seq):
    t, d = x.shape
    tile = min(ROW_TILE, rows_per_seq)
    d_inner, conv_dim = w["wz"].shape[1], w["wx"].shape[1]
    gn = (conv_dim - d_inner) // 2
    n_seq = t // rows_per_seq
    tiles_per_seq = rows_per_seq // tile
    full = lambda a: pl.BlockSpec(a.shape, lambda i: (0,) * a.ndim)
    row = lambda n: pl.BlockSpec((tile, n), lambda i: (i, 0))
    mod = _mod_specs(t, tile, rows_per_seq, d)
    ws = [w["wz"], w["wx"], w["wdt"], w["conv_w"], w["conv_b"], w["dt_bias"]]
    return pl.pallas_call(
        functools.partial(_ssm_in_kernel, tile=tile, tiles_per_seq=tiles_per_seq, d_inner=d_inner,
                          n_heads=d_inner // SSM_HEAD_DIM),
        grid=(t // tile,),
        in_specs=[row(d), full(g), mod, mod] + [full(a) for a in ws],
        out_specs=[row(d_inner), row(d_inner), row(gn), row(gn), row(LANE),
                   pl.BlockSpec((1, SUBLANE, conv_dim), lambda i: (i // tiles_per_seq, 0, 0))],
        out_shape=[jax.ShapeDtypeStruct((t, d_inner), F32), jax.ShapeDtypeStruct((t, d_inner), F32),
                   jax.ShapeDtypeStruct((t, gn), BF16), jax.ShapeDtypeStruct((t, gn), BF16),
                   jax.ShapeDtypeStruct((t, LANE), F32), jax.ShapeDtypeStruct((n_seq, SUBLANE, conv_dim), F32)],
        scratch_shapes=[pltpu.VMEM((tile + SUBLANE, conv_dim), F32)],
        compiler_params=_params(("arbitrary",)),
        name="ssm_in_conv",
    )(x, g, _mod_arr(shift, rows_per_seq, tile), _mod_arr(scale, rows_per_seq, tile), *ws)


def _ssd_kernel(xs_ref, b_ref, c_ref, dt_ref, z_ref, alog_ref, e_ref, dskip_ref, ng_ref, y_ref, hout_ref,
                st_sc, y_sc, *, cl, n_groups, pairs_per_group):
    c = pl.program_id(1)

    @pl.when(c == 0)
    def _():
        st_sc[...] = jnp.zeros(st_sc.shape, F32)

    dt = dt_ref[0]
    d_a = dt * (-jnp.exp(alog_ref[...]))
    rowi = lax.broadcasted_iota(jnp.int32, (cl, cl), 0)
    coli = lax.broadcasted_iota(jnp.int32, (cl, cl), 1)
    causal = rowi >= coli
    ltri = jnp.where(causal, 1.0, 0.0).astype(BF16)
    cum = sum(_dot(ltri, part) for part in _split3(d_a))
    cum_t = cum.T
    expand = e_ref[...]
    dt_x = sum(_dot(part, expand) for part in _split3(dt))
    cum_x = sum(_dot(part, expand) for part in _split3(cum))
    xs = xs_ref[0]
    xdt = xs * dt_x
    xdt_b = xdt.astype(BF16)
    w_out = (xdt * jnp.exp(cum_x[cl - 1:cl, :] - cum_x)).astype(BF16)
    grow = jnp.exp(cum_x)
    lane = lax.broadcasted_iota(jnp.int32, (cl, LANE), 1)
    first = lax.broadcasted_iota(jnp.int32, (LANE, 1), 0) < SSM_HEAD_DIM
    for g in range(n_groups):
        bg = b_ref[0, :, g * D_STATE:(g + 1) * D_STATE]
        cg = c_ref[0, :, g * D_STATE:(g + 1) * D_STATE]
        cb = _dot_nt(cg, bg)
        for j in range(pairs_per_group):
            pr = g * pairs_per_group + j
            h0, h1 = 2 * pr, 2 * pr + 1
            sl = slice(pr * LANE, (pr + 1) * LANE)
            seg0 = jnp.exp(jnp.where(causal, cum[:, h0:h0 + 1] - cum_t[h0:h0 + 1, :], -jnp.inf))
            seg1 = jnp.exp(jnp.where(causal, cum[:, h1:h1 + 1] - cum_t[h1:h1 + 1, :], -jnp.inf))
            xp = xdt_b[:, sl]
            y_diag = jnp.where(lane < SSM_HEAD_DIM, _dot((cb * seg0).astype(BF16), xp),
                               _dot((cb * seg1).astype(BF16), xp))
            st = st_sc[pr]
            y_sc[:, sl] = y_diag + _dot_nt(cg, st.astype(BF16)) * grow[:, sl]
            decay = jnp.exp(jnp.where(first, cum_t[h0:h0 + 1, cl - 1:cl], cum_t[h1:h1 + 1, cl - 1:cl]))
            st_sc[pr] = st * decay + _dot_tn(w_out[:, sl], bg)
    zz = z_ref[0]
    y = (y_sc[...] + dskip_ref[...] * xs) * _silu(zz)
    y_ref[0] = _rms(y, ng_ref[...]).astype(y_ref.dtype)

    @pl.when(c == pl.num_programs(1) - 1)
    def _():
        hout_ref[0] = st_sc[...]


def _ssd(xs, bm, cm, dt, z, w, n_seq, seq_len):
    d_inner = xs.shape[1]
    gn = bm.shape[1]
    n_pairs = d_inner // LANE
    cl = min(SSD_CHUNK, seq_len)
    nc = seq_len // cl
    r3 = lambda a: a.reshape(n_seq, seq_len, a.shape[1])
    blk = lambda n: pl.BlockSpec((1, cl, n), lambda b, c: (b, c, 0))
    full = lambda a: pl.BlockSpec(a.shape, lambda b, c: (0,) * a.ndim)
    consts = [w["a_log"], w["expand"], w["d_skip"], w["norm_g"]]
    y, h_out = pl.pallas_call(
        functools.partial(_ssd_kernel, cl=cl, n_groups=SSM_GROUPS, pairs_per_group=n_pairs // SSM_GROUPS),
        grid=(n_seq, nc),
        in_specs=[blk(d_inner), blk(gn), blk(gn), blk(LANE), blk(d_inner)] + [full(a) for a in consts],
        out_specs=[blk(d_inner), pl.BlockSpec((1, n_pairs, LANE, D_STATE), lambda b, c: (b, 0, 0, 0))],
        out_shape=[jax.ShapeDtypeStruct((n_seq, seq_len, d_inner), BF16),
                   jax.ShapeDtypeStruct((n_seq, n_pairs, LANE, D_STATE), F32)],
        scratch_shapes=[pltpu.VMEM((n_pairs, LANE, D_STATE), F32), pltpu.VMEM((cl, d_inner), F32)],
        compiler_params=_params(("arbitrary", "arbitrary")),
        name="ssd_scan",
    )(r3(xs), r3(bm), r3(cm), r3(dt), r3(z), *consts)
    return y.reshape(n_seq * seq_len, d_inner), h_out


def _prep_ssm(w_in, conv_w, conv_b, dt_bias, a_log, d_skip, norm_g, w_out):
    d = w_in.shape[0]
    d_inner = w_out.shape[0]
    n_heads = d_inner // SSM_HEAD_DIM
    conv_dim = conv_w.shape[1]
    pad = LANE - n_heads
    w_dt = w_in[:, d_inner + conv_dim:]
    rep = lambda v: jnp.repeat(v, SSM_HEAD_DIM, axis=-1)
    head_of = jnp.arange(d_inner) // SSM_HEAD_DIM
    return dict(
        wz=w_in[:, :d_inner].astype(BF16),
        wx=w_in[:, d_inner:d_inner + conv_dim].astype(BF16),
        wdt=jnp.concatenate([w_dt, jnp.zeros((d, pad), F32)], axis=1).astype(BF16),
        wdt_x=rep(w_dt).astype(BF16),
        conv_w=conv_w, conv_b=conv_b.reshape(1, conv_dim),
        dt_bias=jnp.concatenate([dt_bias, jnp.zeros((pad,), F32)]).reshape(1, LANE),
        dt_bias_x=rep(dt_bias).reshape(1, d_inner),
        a_log=jnp.concatenate([a_log, jnp.zeros((pad,), F32)]).reshape(1, LANE),
        a_log_x=rep(a_log).reshape(1, d_inner),
        expand=(jnp.arange(LANE)[:, None] == head_of[None, :]).astype(BF16),
        d_skip=rep(d_skip).reshape(1, d_inner),
        norm_g=norm_g.reshape(1, d_inner),
        w_out=w_out.astype(BF16),
    )


def _ssm_step_in_kernel(x_ref, g_ref, sh_ref, sc_ref, wz_ref, wx_ref, wdt_ref, cw_ref, cb_ref, dtb_ref, alog_ref,
                        buf_ref, z_ref, xs_ref, b_ref, c_ref, conv_ref, xdt_t_ref, dec_t_ref, *, d_inner):
    h = _modulate(x_ref[...], g_ref[...], sh_ref[0], sc_ref[0]).astype(BF16)
    z_ref[...] = _dot(h, wz_ref[...])
    u = _dot(h, wx_ref[...])
    cw = cw_ref[...]
    y = cw[0:1] * buf_ref[0] + cw[1:2] * buf_ref[1] + cw[2:3] * buf_ref[2] + cw[3:4] * u + cb_ref[...]
    conv_ref[0] = buf_ref[1]
    conv_ref[1] = buf_ref[2]
    conv_ref[2] = u
    xbc = _silu(y)
    gn = (xbc.shape[1] - d_inner) // 2
    xs = xbc[:, :d_inner]
    xs_ref[...] = xs
    b_ref[...] = xbc[:, d_inner:d_inner + gn]
    c_ref[...] = xbc[:, d_inner + gn:]
    dt = _softplus(_dot(h, wdt_ref[...]) + dtb_ref[...])
    xdt_t_ref[...] = (xs * dt).T
    dec_t_ref[...] = jnp.exp(dt * (-jnp.exp(alog_ref[...]))).T


def _ssm_step_in(x, g, shift, scale, buf_t, w):
    n, d = x.shape
    d_inner, conv_dim = w["wz"].shape[1], w["wx"].shape[1]
    gn = (conv_dim - d_inner) // 2
    ws = [w["wz"], w["wx"], w["wdt_x"], w["conv_w"], w["conv_b"], w["dt_bias_x"], w["a_log_x"], buf_t]
    sds = jax.ShapeDtypeStruct
    return pl.pallas_call(
        functools.partial(_ssm_step_in_kernel, d_inner=d_inner),
        out_shape=[sds((n, d_inner), F32), sds((n, d_inner), F32), sds((n, gn), F32), sds((n, gn), F32),
                   sds((CONV_W - 1, n, conv_dim), F32), sds((d_inner, n), F32), sds((d_inner, n), F32)],
        compiler_params=pltpu.CompilerParams(vmem_limit_bytes=VMEM_LIMIT),
        name="ssm_step_in",
    )(x, g, shift.reshape(1, n, d), scale.reshape(1, n, d), *ws)


def _ssm_step_kernel(st_ref, xdt_t_ref, dec_t_ref, b_ref, c_ref, hout_ref, y_t_ref, *, n_groups):
    b = pl.program_id(0)

    @pl.when(b == 0)
    def _():
        y_t_ref[...] = jnp.zeros(y_t_ref.shape, F32)

    rg = st_ref.shape[2] // n_groups
    mine = lax.broadcasted_iota(jnp.int32, (rg, LANE), 1) == b
    for g in range(n_groups):
        rs = slice(g * rg, (g + 1) * rg)
        xcol = jnp.sum(jnp.where(mine, xdt_t_ref[rs, :], 0.0), axis=1, keepdims=True)
        dcol = jnp.sum(jnp.where(mine, dec_t_ref[rs, :], 0.0), axis=1, keepdims=True)
        bg = b_ref[0, :, g * D_STATE:(g + 1) * D_STATE]
        cg = c_ref[0, :, g * D_STATE:(g + 1) * D_STATE]
        s_new = st_ref[0, 0, rs, :] * dcol + xcol * bg
        hout_ref[0, rs, :] = s_new
        ycol = jnp.sum(s_new * cg, axis=1, keepdims=True)
        y_t_ref[rs, :] = jnp.where(mine, ycol, y_t_ref[rs, :])


def _ssm_step(layer, state, xdt_t, dec_t, bm, cm):
    _, n, rows, ns = state.shape
    full = lambda a: pl.BlockSpec(a.shape, lambda b: (0,) * a.ndim)
    seq_row = pl.BlockSpec((1, 1, bm.shape[1]), lambda b: (b, 0, 0))
    bm, cm = bm.reshape(n, 1, -1), cm.reshape(n, 1, -1)
    return pl.pallas_call(
        functools.partial(_ssm_step_kernel, n_groups=SSM_GROUPS),
        grid=(n,),
        in_specs=[pl.BlockSpec((1, 1, rows, ns), lambda b: (layer, b, 0, 0)), full(xdt_t), full(dec_t), seq_row,
                  seq_row],
        out_specs=[pl.BlockSpec((1, rows, ns), lambda b: (b, 0, 0)), pl.BlockSpec((rows, n), lambda b: (0, 0))],
        out_shape=[jax.ShapeDtypeStruct((n, rows, ns), F32), jax.ShapeDtypeStruct((rows, n), F32)],
        compiler_params=_params(("arbitrary",)),
        name="ssm_step",
    )(state, xdt_t, dec_t, bm, cm)


def _ssm_step_out_kernel(y_t_ref, xs_ref, z_ref, dskip_ref, ng_ref, w_ref, x_ref, gate_ref, o_ref):
    y = (y_t_ref[...].T + dskip_ref[...] * xs_ref[...]) * _silu(z_ref[...])
    y = _rms(y, ng_ref[...]).astype(BF16)
    o_ref[...] = x_ref[...] + gate_ref[0] * _dot(y, w_ref[...])


def _ssm_step_out(y_t, xs, z, w, x, gate):
    n, d = x.shape
    return pl.pallas_call(
        _ssm_step_out_kernel,
        out_shape=jax.ShapeDtypeStruct((n, d), F32),
        compiler_params=pltpu.CompilerParams(vmem_limit_bytes=VMEM_LIMIT),
        name="ssm_step_out",
    )(y_t, xs, z, w["d_skip"], w["norm_g"], w["w_out"], x, gate.reshape(1, n, d))


def _final_norm_kernel(x_ref, g_ref, o_ref):
    o_ref[...] = _rms(x_ref[...], g_ref[...])


def _final_norm(x, g):
    t, d = x.shape
    tile = min(FFN_ROW_TILE, t)
    return pl.pallas_call(
        _final_norm_kernel,
        grid=(t // tile,),
        in_specs=[pl.BlockSpec((tile, d), lambda i: (i, 0)), pl.BlockSpec((1, d), lambda i: (0, 0))],
        out_specs=pl.BlockSpec((tile, d), lambda i: (i, 0)),
        out_shape=jax.ShapeDtypeStruct((t, d), F32),
        compiler_params=_params(("arbitrary",)),
        name="final_norm",
    )(x, g.reshape(1, d))


def kernel(x_prompt, x_sample, c_prompt, c_sample, cache_kv_latent, cache_k_rope, page_table, state_ssm, state_conv, w_ada, b_ada, norm_mix_g, norm_ffn_g, norm_final_g, mla_w_in, mla_q_norm_g, mla_kv_norm_g, mla_w_qb, mla_w_kvb, mla_w_o, ssm_w_in, ssm_conv_w, ssm_conv_b, ssm_dt_bias, ssm_a_log, ssm_d, ssm_norm_g, ssm_w_out, ffn_w_gate, ffn_w_up, ffn_w_down, moe_w_router, moe_w_gate, moe_w_up, moe_w_down):
    nb, seq, d = x_prompt.shape
    ns = x_sample.shape[0]
    depth = w_ada.shape[0]
    past_len = page_table.shape[1] * cache_kv_latent.shape[2]
    attn_scale = (QK_NOPE + QK_ROPE) ** -0.5

    xp = x_prompt.reshape(nb * seq, d)
    xs = x_sample.reshape(ns, d)
    mods = _ada(jnp.concatenate([c_prompt, c_sample], axis=0), w_ada, b_ada)
    mods = mods.reshape(depth, nb + ns, 6, d)
    tabs_p = _rope_tables(jnp.arange(seq), attn_scale * LOG2E)
    tabs_s = _rope_tables(jnp.full((ns,), past_len), attn_scale)
    state = state_ssm.reshape(state_ssm.shape[0], ns, -1, D_STATE)
    cache_rope_t = cache_k_rope.transpose(0, 1, 3, 2)

    lat_p, rope_p, lat_s, rope_s = [], [], [], []
    ssm_p, conv_p, ssm_s, conv_s = [], [], [], []
    for i in range(depth):
        j = i // 2
        mp = [mods[i, :nb, k] for k in range(6)]
        ms = [mods[i, nb:, k] for k in range(6)]
        g_mix = norm_mix_g[i].reshape(1, d)
        g_ffn = norm_ffn_g[i].reshape(1, d)
        if i % 2 == 0:
            w = _prep_mla(mla_w_in[j], mla_w_qb[j], mla_w_kvb[j])
            qg = mla_q_norm_g[j].reshape(1, -1)
            kvg = mla_kv_norm_g[j].reshape(1, -1)
            w_o = mla_w_o[j].astype(BF16)
            ckv, kr, q, k, vt = _mla_proj(xp, g_mix, mp[0], mp[1], tabs_p, qg, kvg, w, seq, True)
            o_t = _flash(q, k, vt.reshape(MLA_HEADS, V_DIM, nb * seq), nb, seq)
            xp = _proj_res(o_t.reshape(MLA_HEADS * V_DIM, nb * seq), w_o, xp, mp[2], seq, transposed=True)
            lat_p.append(ckv.reshape(nb, seq, -1))
            rope_p.append(kr.reshape(nb, seq, -1))

            ckv, kr, q = _mla_proj(xs, g_mix, ms[0], ms[1], tabs_s, qg, kvg, w, 1, False)
            kv_rank = ckv.shape[1]
            qlat = _headmat(q, w["wabs"], LANE, kv_rank)
            o_lat = _decode_attn(j, page_table, qlat.reshape(ns, MLA_HEADS, kv_rank), q.reshape(ns, MLA_HEADS, LANE),
                                 ckv.reshape(ns, 1, kv_rank), kr.reshape(ns, 1, QK_ROPE), cache_kv_latent,
                                 cache_rope_t)
            o = _headmat(o_lat.reshape(ns, MLA_HEADS * kv_rank), w["wvbd"], 2 * kv_rank, 2 * V_DIM)
            xs = _proj_res(o, w_o, xs, ms[2], 1)
            lat_s.append(ckv.reshape(ns, 1, -1))
            rope_s.append(kr.reshape(ns, 1, -1))
        else:
            w = _prep_ssm(ssm_w_in[j], ssm_conv_w[j], ssm_conv_b[j], ssm_dt_bias[j], ssm_a_log[j], ssm_d[j],
                          ssm_norm_g[j], ssm_w_out[j])
            z, xin, bm, cm, dt, conv = _ssm_in(xp, g_mix, mp[0], mp[1], w, seq)
            y, h_new = _ssd(xin, bm, cm, dt, z, w, nb, seq)
            xp = _proj_res(y, w["w_out"], xp, mp[2], seq)
            ssm_p.append(h_new.reshape(nb, -1, SSM_HEAD_DIM, D_STATE))
            conv_p.append(conv[:, SUBLANE - (CONV_W - 1):])

            buf_t = state_conv[j].transpose(1, 0, 2)
            z, xin, bm, cm, conv, xdt_t, dec_t = _ssm_step_in(xs, g_mix, ms[0], ms[1], buf_t, w)
            h_new, y_t = _ssm_step(j, state, xdt_t, dec_t, bm, cm)
            xs = _ssm_step_out(y_t, xin, z, w, xs, ms[2])
            ssm_s.append(h_new.reshape(ns, -1, SSM_HEAD_DIM, D_STATE))
            conv_s.append(conv.transpose(1, 0, 2))
        if i % 2 == 0:
            wg = ffn_w_gate[j].astype(BF16)
            wu = ffn_w_up[j].astype(BF16)
            wd = ffn_w_down[j].astype(BF16)
            xp = _ffn(xp, g_ffn, mp[3], mp[4], mp[5], wg, wu, wd, seq)
            xs = _ffn(xs, g_ffn, ms[3], ms[4], ms[5], wg, wu, wd, 1)
        else:
            wg = moe_w_gate[j].astype(BF16)
            wu = moe_w_up[j].astype(BF16)
            wd = moe_w_down[j].astype(BF16)
            xp = _moe(xp, g_ffn, mp[3], mp[4], mp[5], moe_w_router[j], wg, wu, wd, seq)
            xs = _moe(xs, g_ffn, ms[3], ms[4], ms[5], moe_w_router[j], wg, wu, wd, 1)
    y_prompt = _final_norm(xp, norm_final_g).reshape(nb, seq, d)
    y_sample = _final_norm(xs, norm_final_g).reshape(ns, 1, d)
    return (y_prompt, y_sample, jnp.stack(lat_p), jnp.stack(rope_p), jnp.stack(lat_s), jnp.stack(rope_s),
            jnp.stack(ssm_p), jnp.stack(conv_p), jnp.stack(ssm_s), jnp.stack(conv_s))
```

```python
import functools
import math

import jax
import jax.numpy as jnp
from jax import lax
from jax.experimental import pallas as pl
from jax.experimental.pallas import tpu as pltpu

F32 = jnp.float32
BF16 = jnp.bfloat16

RMS_EPS = 1e-6
MLA_HEADS = 16
QK_NOPE = 64
QK_ROPE = 32
V_DIM = 64
ROPE_THETA = 10000.0
PAGE_SIZE = 128
SSM_HEAD_DIM = 64
SSM_GROUPS = 4
D_STATE = 128
CONV_W = 4
SSD_CHUNK = 128
N_EXPERTS = 8

LANE = 128
SUBLANE = 8
VMEM_LIMIT = 56 << 20

ROW_TILE = 256
FFN_ROW_TILE = 512
FFN_COL_TILE = 1408
ATTN_TILE = 512
NEG = -1e30
LOG2E = math.log2(math.e)


def _params(sem, **kw):
    return pltpu.CompilerParams(dimension_semantics=sem, vmem_limit_bytes=VMEM_LIMIT, **kw)


def _dot(a, b):
    return jnp.dot(a, b, preferred_element_type=F32)


def _dot_nt(a, b):
    return lax.dot_general(a, b, (((1,), (1,)), ((), ())), preferred_element_type=F32)


def _dot_tn(a, b):
    return lax.dot_general(a, b, (((0,), (0,)), ((), ())), preferred_element_type=F32)


def _split3(v):
    hi = v.astype(BF16)
    r = v - hi.astype(F32)
    mid = r.astype(BF16)
    lo = (r - mid.astype(F32)).astype(BF16)
    return hi, mid, lo


def _silu(x):
    return x * jax.nn.sigmoid(x)


def _rms(x, g):
    return (x * lax.rsqrt(jnp.mean(x * x, axis=-1, keepdims=True) + RMS_EPS)) * g


def _modulate(x, g, shift, scale):
    return _rms(x, g) * (1.0 + scale) + shift


def _mod_specs(n_rows, tile, rows_per_seq, d):
    if rows_per_seq == 1:
        return pl.BlockSpec((1, tile, d), lambda i, *_: (i, 0, 0))
    tiles_per_seq = rows_per_seq // tile
    return pl.BlockSpec((1, 1, d), lambda i, *_: (i // tiles_per_seq, 0, 0))


def _mod_arr(m, rows_per_seq, tile):
    if rows_per_seq == 1:
        return m.reshape(m.shape[0] // tile, tile, m.shape[1])
    return m.reshape(m.shape[0], 1, m.shape[1])


def _ada_kernel(c_ref, w_ref, b_ref, o_ref):
    s = _silu(c_ref[...]).astype(BF16)
    o_ref[0] = _dot(s, w_ref[0].astype(BF16)) + b_ref[0]


def _ada(c_all, w_ada, b_ada):
    depth, d, d6 = w_ada.shape
    n = c_all.shape[0]
    return pl.pallas_call(
        _ada_kernel,
        grid=(depth, d6 // d),
        in_specs=[pl.BlockSpec((n, d), lambda i, k: (0, 0)),
                  pl.BlockSpec((1, d, d), lambda i, k: (i, 0, k)),
                  pl.BlockSpec((1, 1, d), lambda i, k: (i, 0, k))],
        out_specs=pl.BlockSpec((1, n, d), lambda i, k: (i, 0, k)),
        out_shape=jax.ShapeDtypeStruct((depth, n, d6), F32),
        compiler_params=_params(("arbitrary", "arbitrary")),
        name="adaln",
    )(c_all, w_ada, b_ada.reshape(depth, 1, d6))


def _mla_proj_kernel(x_ref, g_ref, sh_ref, sc_ref, ct_ref, st_ref, cts_ref, sts_ref, wcq_ref, wckv_ref,
                     wkra_ref, wkrb_ref, qg_ref, kvg_ref, wqa_ref, wqb_ref, *rest, heads, with_kv):
    if with_kv:
        wka_ref, wvt_ref, ckv_ref, kr_ref, q_ref, k_ref, vt_ref = rest
    else:
        ckv_ref, kr_ref, q_ref = rest
    h = _modulate(x_ref[...], g_ref[...], sh_ref[0], sc_ref[0]).astype(BF16)
    cq = _rms(_dot(h, wcq_ref[...]), qg_ref[...]).astype(BF16)
    ckv = _rms(_dot(h, wckv_ref[...]), kvg_ref[...])
    kr = _dot(h, wkra_ref[...]) * ct_ref[...] + _dot(h, wkrb_ref[...]) * st_ref[...]
    ckv_ref[...] = ckv
    kr_ref[...] = kr[:, :QK_ROPE]
    rep = lambda t: jnp.concatenate([t] * heads, axis=1)
    q = (_dot(cq, wqa_ref[...]) * rep(cts_ref[...]) + _dot(cq, wqb_ref[...]) * rep(sts_ref[...])).astype(BF16)
    if not with_kv:
        q_ref[...] = q
        return
    ckvb = ckv.astype(BF16)
    k = (_dot(ckvb, wka_ref[...]) + rep(kr)).astype(BF16)
    for hd in range(heads):
        q_ref[hd] = q[:, hd * LANE:(hd + 1) * LANE]
        k_ref[hd] = k[:, hd * LANE:(hd + 1) * LANE]
    vt_ref[...] = _dot_nt(wvt_ref[...], ckvb).astype(BF16)


def _rot_cols(w):
    half = w.shape[-1] // 2
    return jnp.concatenate([-w[..., half:], w[..., :half]], axis=-1)


def _prep_mla(w_in, w_qb, w_kvb):
    d = w_in.shape[0]
    q_rank = w_qb.shape[0]
    kv_rank = w_kvb.shape[0]
    hd = LANE
    pad = hd - QK_ROPE - QK_NOPE
    w_kr = w_in[:, q_rank + kv_rank:]
    zk = jnp.zeros((d, hd - QK_ROPE), F32)
    wq = w_qb.reshape(q_rank, MLA_HEADS, QK_NOPE + QK_ROPE)
    q_nope, q_pe = wq[..., :QK_NOPE], wq[..., QK_NOPE:]
    zq = jnp.zeros((q_rank, MLA_HEADS, pad), F32)
    wkv = w_kvb.reshape(kv_rank, MLA_HEADS, QK_NOPE + V_DIM)
    k_nope, v = wkv[..., :QK_NOPE], wkv[..., QK_NOPE:]
    return dict(
        wcq=w_in[:, :q_rank].astype(BF16),
        wckv=w_in[:, q_rank:q_rank + kv_rank].astype(BF16),
        wkra=jnp.concatenate([w_kr, zk], axis=1).astype(BF16),
        wkrb=jnp.concatenate([_rot_cols(w_kr), zk], axis=1).astype(BF16),
        wqa=jnp.concatenate([q_pe, q_nope, zq], axis=-1).reshape(q_rank, MLA_HEADS * hd).astype(BF16),
        wqb=jnp.concatenate([_rot_cols(q_pe), jnp.zeros((q_rank, MLA_HEADS, hd - QK_ROPE), F32)],
                            axis=-1).reshape(q_rank, MLA_HEADS * hd).astype(BF16),
        wka=jnp.concatenate([jnp.zeros((kv_rank, MLA_HEADS, QK_ROPE), F32), k_nope,
                             jnp.zeros((kv_rank, MLA_HEADS, pad), F32)],
                            axis=-1).reshape(kv_rank, MLA_HEADS * hd).astype(BF16),
        wvt=v.reshape(kv_rank, MLA_HEADS * V_DIM).T.astype(BF16),
        wabs=jnp.concatenate([jnp.zeros((MLA_HEADS, QK_ROPE, kv_rank), F32), k_nope.transpose(1, 2, 0),
                              jnp.zeros((MLA_HEADS, pad, kv_rank), F32)], axis=1).astype(BF16),
        wvbd=_block_diag_pairs(v.transpose(1, 0, 2)).astype(BF16),
    )


def _block_diag_pairs(v):
    h, r, dv = v.shape
    v = v.reshape(h // 2, 2, r, dv)
    z = jnp.zeros((h // 2, r, dv), v.dtype)
    top = jnp.concatenate([v[:, 0], z], axis=-1)
    bot = jnp.concatenate([z, v[:, 1]], axis=-1)
    return jnp.concatenate([top, bot], axis=1)


def _rope_tables(pos, scale):
    half = QK_ROPE // 2
    inv = ROPE_THETA ** (-jnp.arange(half, dtype=F32) / half)
    ang = pos.astype(F32)[:, None] * inv[None, :]
    cos, sin = jnp.cos(ang), jnp.sin(ang)
    n = pos.shape[0]
    ct = jnp.concatenate([cos, cos, jnp.ones((n, LANE - QK_ROPE), F32)], axis=1)
    st = jnp.concatenate([sin, sin, jnp.zeros((n, LANE - QK_ROPE), F32)], axis=1)
    return ct, st, ct * scale, st * scale


def _mla_proj(x, g, shift, scale, tabs, qg, kvg, w, rows_per_seq, with_kv):
    t, d = x.shape
    tile = min(ROW_TILE, t)
    n_tab_tiles = tabs[0].shape[0] // tile
    heads = MLA_HEADS
    kv_rank = w["wckv"].shape[1]
    full = lambda a: pl.BlockSpec(a.shape, lambda i: (0,) * a.ndim)
    row = lambda n: pl.BlockSpec((tile, n), lambda i: (i, 0))
    tab = pl.BlockSpec((tile, LANE), lambda i: (i % n_tab_tiles, 0))
    mod = _mod_specs(t, tile, rows_per_seq, d)
    ws = [w["wcq"], w["wckv"], w["wkra"], w["wkrb"], qg, kvg, w["wqa"], w["wqb"]]
    outs = [jax.ShapeDtypeStruct((t, kv_rank), F32), jax.ShapeDtypeStruct((t, QK_ROPE), F32)]
    out_specs = [row(kv_rank), row(QK_ROPE)]
    if with_kv:
        ws += [w["wka"], w["wvt"]]
        head_major = pl.BlockSpec((heads, tile, LANE), lambda i: (0, i, 0))
        outs += [jax.ShapeDtypeStruct((heads, t, LANE), BF16), jax.ShapeDtypeStruct((heads, t, LANE), BF16),
                 jax.ShapeDtypeStruct((heads * V_DIM, t), BF16)]
        out_specs += [head_major, head_major, pl.BlockSpec((heads * V_DIM, tile), lambda i: (0, i))]
    else:
        outs.append(jax.ShapeDtypeStruct((t, heads * LANE), BF16))
        out_specs.append(row(heads * LANE))
    return pl.pallas_call(
        functools.partial(_mla_proj_kernel, heads=heads, with_kv=with_kv),
        grid=(t // tile,),
        in_specs=[row(d), full(g), mod, mod, tab, tab, tab, tab] + [full(a) for a in ws],
        out_specs=out_specs,
        out_shape=outs,
        compiler_params=_params(("arbitrary",)),
        name="mla_proj",
    )(x, g, _mod_arr(shift, rows_per_seq, tile), _mod_arr(scale, rows_per_seq, tile), *tabs, *ws)


def _flash_kernel(qi_tab, ki_tab, q_ref, k_ref, vt_ref, o_ref, m_sc, l_sc, acc_sc, s0_sc, s1_sc, *, tile, heads):
    t = pl.program_id(1)
    qi = qi_tab[t]
    ki = ki_tab[t]

    @pl.when(ki == 0)
    def _():
        m_sc[...] = jnp.full(m_sc.shape, NEG, F32)
        l_sc[...] = jnp.zeros(l_sc.shape, F32)
        acc_sc[...] = jnp.zeros(acc_sc.shape, F32)

    def scores(h, s_sc):
        s_sc[...] = _dot_nt(k_ref[h], q_ref[h])

    def sweep(diagonal):
        if diagonal:
            visible = (lax.broadcasted_iota(jnp.int32, (tile, tile), 0)
                       <= lax.broadcasted_iota(jnp.int32, (tile, tile), 1))

        def softmax_pv(h, s_sc):
            s = s_sc[...]
            if diagonal:
                s = jnp.where(visible, s, NEG)
            m_prev = m_sc[h]
            m_new = jnp.maximum(m_prev, jnp.max(s, axis=0, keepdims=True))
            alpha = jnp.exp2(m_prev - m_new)
            p = jnp.exp2(s - m_new)
            l_new = alpha * l_sc[h] + jnp.sum(p, axis=0, keepdims=True)
            acc = alpha * acc_sc[h] + _dot(vt_ref[h], p.astype(BF16))
            if diagonal:
                o_ref[h] = (acc / l_new).astype(o_ref.dtype)
            else:
                m_sc[h] = m_new
                l_sc[h] = l_new
                acc_sc[h] = acc

        scores(0, s0_sc)

        def pair(i, carry):
            h0 = 2 * i
            scores(h0 + 1, s1_sc)
            softmax_pv(h0, s0_sc)
            scores(jnp.minimum(h0 + 2, heads - 1), s0_sc)
            softmax_pv(h0 + 1, s1_sc)
            return carry

        lax.fori_loop(0, heads // 2, pair, 0)

    @pl.when(ki < qi)
    def _():
        sweep(False)

    @pl.when(ki == qi)
    def _():
        sweep(True)


def _flash(q, k, vt, n_seq, seq_len):
    heads = q.shape[0]
    tile = min(ATTN_TILE, seq_len)
    nq = seq_len // tile
    pairs = [(i, j) for i in range(nq) for j in range(i + 1)]
    qi_tab = jnp.array([p[0] for p in pairs], jnp.int32)
    ki_tab = jnp.array([p[1] for p in pairs], jnp.int32)
    grid_spec = pltpu.PrefetchScalarGridSpec(
        num_scalar_prefetch=2,
        grid=(n_seq, len(pairs)),
        in_specs=[pl.BlockSpec((heads, tile, LANE), lambda b, t, qt, kt: (0, b * nq + qt[t], 0)),
                  pl.BlockSpec((heads, tile, LANE), lambda b, t, qt, kt: (0, b * nq + kt[t], 0)),
                  pl.BlockSpec((heads, V_DIM, tile), lambda b, t, qt, kt: (0, 0, b * nq + kt[t]))],
        out_specs=pl.BlockSpec((heads, V_DIM, tile), lambda b, t, qt, kt: (0, 0, b * nq + qt[t])),
        scratch_shapes=[pltpu.VMEM((heads, 1, tile), F32), pltpu.VMEM((heads, 1, tile), F32),
                        pltpu.VMEM((heads, V_DIM, tile), F32), pltpu.VMEM((tile, tile), F32),
                        pltpu.VMEM((tile, tile), F32)],
    )
    return pl.pallas_call(
        functools.partial(_flash_kernel, tile=tile, heads=heads),
        grid_spec=grid_spec,
        out_shape=jax.ShapeDtypeStruct(vt.shape, BF16),
        compiler_params=_params(("arbitrary", "arbitrary")),
        name="flash_attn",
    )(qi_tab, ki_tab, q, k, vt)


def _headmat_kernel(a_ref, w_ref, o_ref):
    o_ref[...] = _dot(a_ref[...], w_ref[0]).astype(o_ref.dtype)


def _headmat(a, w, in_w, out_w):
    n, rows = w.shape[0], a.shape[0]
    return pl.pallas_call(
        _headmat_kernel,
        grid=(n,),
        in_specs=[pl.BlockSpec((rows, in_w), lambda i: (0, i)),
                  pl.BlockSpec((1, in_w, out_w), lambda i: (i, 0, 0))],
        out_specs=pl.BlockSpec((rows, out_w), lambda i: (0, i)),
        out_shape=jax.ShapeDtypeStruct((rows, n * out_w), BF16),
        compiler_params=_params(("arbitrary",)),
        name="head_matmul",
    )(a, w)


def _decode_kernel(pt_ref, qlat_ref, q_ref, ckv_ref, kr_ref, lat_hbm, rope_hbm, o_ref, latbuf, ropebuf, kcat, rcat,
                   sem, *, layer, pages):
    b = pl.program_id(0)
    slot = b % 2

    def fetch(seq, into):
        def issue(i, carry):
            pg = pt_ref[seq * pages + i]
            pltpu.make_async_copy(lat_hbm.at[layer, pg], latbuf.at[into, i], sem.at[into]).start(priority=0)
            pltpu.make_async_copy(rope_hbm.at[layer, pg], ropebuf.at[into, i], sem.at[into]).start(priority=1)
            return carry

        lax.fori_loop(0, pages, issue, 0)

    @pl.when(b == 0)
    def _():
        fetch(0, 0)

    @pl.when(b + 1 < pl.num_programs(0))
    def _():
        fetch(b + 1, 1 - slot)

    pltpu.make_async_copy(lat_hbm.at[layer, pl.ds(0, pages)], latbuf.at[slot], sem.at[slot]).wait()
    pltpu.make_async_copy(rope_hbm.at[layer, pl.ds(0, pages)], ropebuf.at[slot], sem.at[slot]).wait()
    for i in range(pages):
        kcat[i * PAGE_SIZE:(i + 1) * PAGE_SIZE, :] = latbuf[slot, i].astype(BF16)
        rcat[:, i * PAGE_SIZE:(i + 1) * PAGE_SIZE] = ropebuf[slot, i].astype(BF16)
    ql = qlat_ref[0]
    qp = q_ref[0][:, :QK_ROPE]
    keys = kcat[...]
    s = _dot_nt(ql, keys) + _dot(qp, rcat[...])
    kl = ckv_ref[0].astype(BF16).astype(F32)
    kp = kr_ref[0].astype(BF16).astype(F32)
    s1 = (jnp.sum(ql.astype(F32) * kl, axis=1, keepdims=True)
          + jnp.sum(qp.astype(F32) * kp, axis=1, keepdims=True))
    m = jnp.maximum(jnp.max(s, axis=1, keepdims=True), s1)
    p = jnp.exp(s - m)
    p1 = jnp.exp(s1 - m)
    l = jnp.sum(p, axis=1, keepdims=True) + p1
    acc = _dot(p.astype(BF16), keys) + p1.astype(BF16).astype(F32) * kl
    o_ref[0] = (acc / l).astype(o_ref.dtype)


def _decode_attn(layer, page_table, qlat, q, ckv, kr, cache_lat, cache_rope):
    b, heads, rank = qlat.shape
    pages = page_table.shape[1]
    seq = lambda n, w: pl.BlockSpec((1, n, w), lambda b_, pt: (b_, 0, 0))
    hbm = pl.BlockSpec(memory_space=pl.ANY)
    grid_spec = pltpu.PrefetchScalarGridSpec(
        num_scalar_prefetch=1,
        grid=(b,),
        in_specs=[seq(heads, rank), seq(heads, LANE), seq(1, rank), seq(1, QK_ROPE), hbm, hbm],
        out_specs=seq(heads, rank),
        scratch_shapes=[pltpu.VMEM((2, pages, PAGE_SIZE, rank), F32), pltpu.VMEM((2, pages, QK_ROPE, PAGE_SIZE), F32),
                        pltpu.VMEM((pages * PAGE_SIZE, rank), BF16), pltpu.VMEM((QK_ROPE, pages * PAGE_SIZE), BF16),
                        pltpu.SemaphoreType.DMA((2,))],
    )
    return pl.pallas_call(
        functools.partial(_decode_kernel, layer=layer, pages=pages),
        grid_spec=grid_spec,
        out_shape=jax.ShapeDtypeStruct((b, heads, rank), BF16),
        compiler_params=_params(("arbitrary",)),
        name="decode_attn",
    )(page_table.reshape(-1), qlat, q, ckv, kr, cache_lat, cache_rope)


def _proj_res_kernel(a_ref, w_ref, x_ref, gate_ref, o_ref, *, transposed):
    y = _dot_tn(a_ref[...], w_ref[...]) if transposed else _dot(a_ref[...], w_ref[...])
    o_ref[...] = x_ref[...] + gate_ref[0] * y


def _proj_res(a, w, x, gate, rows_per_seq, transposed=False):
    t, d = x.shape
    k = w.shape[0]
    tile = min(FFN_ROW_TILE, t)
    a_spec = pl.BlockSpec((k, tile), lambda i: (0, i)) if transposed else pl.BlockSpec((tile, k), lambda i: (i, 0))
    return pl.pallas_call(
        functools.partial(_proj_res_kernel, transposed=transposed),
        grid=(t // tile,),
        in_specs=[a_spec, pl.BlockSpec((k, d), lambda i: (0, 0)),
                  pl.BlockSpec((tile, d), lambda i: (i, 0)), _mod_specs(t, tile, rows_per_seq, d)],
        out_specs=pl.BlockSpec((tile, d), lambda i: (i, 0)),
        out_shape=jax.ShapeDtypeStruct((t, d), F32),
        compiler_params=_params(("arbitrary",)),
        name="proj_residual",
    )(a, w, x, _mod_arr(gate, rows_per_seq, tile))


def _ffn_kernel(x_ref, g_ref, sh_ref, sc_ref, gate_ref, wg_ref, wu_ref, wd_ref, o_ref, h_sc, acc_sc):
    f = pl.program_id(1)

    @pl.when(f == 0)
    def _():
        h_sc[...] = _modulate(x_ref[...], g_ref[...], sh_ref[0], sc_ref[0]).astype(BF16)
        acc_sc[...] = jnp.zeros(acc_sc.shape, F32)

    h = h_sc[...]
    a = (_silu(_dot(h, wg_ref[...])) * _dot(h, wu_ref[...])).astype(BF16)
    acc_sc[...] += _dot(a, wd_ref[...])

    @pl.when(f == pl.num_programs(1) - 1)
    def _():
        o_ref[...] = x_ref[...] + gate_ref[0] * acc_sc[...]


def _col_tile(ff):
    return FFN_COL_TILE if ff % FFN_COL_TILE == 0 else ff


def _ffn(x, g, shift, scale, gate, wg, wu, wd, rows_per_seq):
    t, d = x.shape
    ff = wg.shape[1]
    tile = min(FFN_ROW_TILE, t)
    tf = _col_tile(ff)
    mod = _mod_specs(t, tile, rows_per_seq, d)
    row = pl.BlockSpec((tile, d), lambda i, f: (i, 0))
    return pl.pallas_call(
        _ffn_kernel,
        grid=(t // tile, ff // tf),
        in_specs=[row, pl.BlockSpec(g.shape, lambda i, f: (0, 0)), mod, mod, mod,
                  pl.BlockSpec((d, tf), lambda i, f: (0, f)), pl.BlockSpec((d, tf), lambda i, f: (0, f)),
                  pl.BlockSpec((tf, d), lambda i, f: (f, 0))],
        out_specs=row,
        out_shape=jax.ShapeDtypeStruct((t, d), F32),
        scratch_shapes=[pltpu.VMEM((tile, d), BF16), pltpu.VMEM((tile, d), F32)],
        compiler_params=_params(("arbitrary", "arbitrary")),
        name="swiglu",
    )(x, g, *[_mod_arr(m, rows_per_seq, tile) for m in (shift, scale, gate)], wg, wu, wd)


_E1, _E2, _R1, _R2, _W1, _W2 = range(6)


def _router_kernel(x_ref, g_ref, sh_ref, sc_ref, wr_ref, meta_ref, cnt_ref, h3_ref, cnt_sc):
    i = pl.program_id(0)

    @pl.when(i == 0)
    def _():
        cnt_sc[...] = jnp.zeros(cnt_sc.shape, F32)

    h = _modulate(x_ref[...], g_ref[...], sh_ref[0], sc_ref[0])
    for j in range(h3_ref.shape[1]):
        h3_ref[:, j, :] = h[:, j * LANE:(j + 1) * LANE]
    hi = h.astype(BF16)
    lo = (h - hi.astype(F32)).astype(BF16)
    w = wr_ref[...]
    whi = w.astype(BF16)
    wlo = (w - whi.astype(F32)).astype(BF16)
    logits = _dot(hi, whi) + _dot(lo, whi) + _dot(hi, wlo)
    lane = lax.broadcasted_iota(jnp.int32, logits.shape, 1).astype(F32)
    logits = jnp.where(lane < N_EXPERTS, logits, NEG)
    p = jnp.exp(logits - jnp.max(logits, axis=1, keepdims=True))
    p = p / jnp.sum(p, axis=1, keepdims=True)
    p1 = jnp.max(p, axis=1, keepdims=True)
    i1 = jnp.min(jnp.where(p == p1, lane, float(LANE)), axis=1, keepdims=True)
    rest = jnp.where(lane == i1, -1.0, p)
    p2 = jnp.max(rest, axis=1, keepdims=True)
    i2 = jnp.min(jnp.where(rest == p2, lane, float(LANE)), axis=1, keepdims=True)
    den = p1 + p2
    hit1, hit2 = lane == i1, lane == i2
    onehot = jnp.where(hit1 | hit2, 1.0, 0.0)
    tile = onehot.shape[0]
    ltri = jnp.where(lax.broadcasted_iota(jnp.int32, (tile, tile), 0)
                     >= lax.broadcasted_iota(jnp.int32, (tile, tile), 1), 1.0, 0.0).astype(BF16)
    incl = _dot(ltri, onehot.astype(BF16))
    before = cnt_sc[...] + incl - onehot
    r1 = jnp.sum(jnp.where(hit1, before, 0.0), axis=1, keepdims=True)
    r2 = jnp.sum(jnp.where(hit2, before, 0.0), axis=1, keepdims=True)
    cnt = cnt_sc[...] + incl[tile - 1:tile, :]
    cnt_sc[...] = cnt
    cnt_ref[...] = jnp.broadcast_to(cnt, cnt_ref.shape)
    rec = jnp.zeros(logits.shape, F32)
    for ln, val in ((_E1, i1), (_E2, i2), (_R1, r1), (_R2, r2), (_W1, p1 / den), (_W2, p2 / den)):
        rec = jnp.where(lane == ln, val, rec)
    meta_ref[...] = rec


def _router(x, g, shift, scale, w_router, rows_per_seq):
    t, d = x.shape
    tile = min(FFN_ROW_TILE, t)
    wr = jnp.concatenate([w_router, jnp.zeros((d, LANE - w_router.shape[1]), F32)], axis=1)
    mod = _mod_specs(t, tile, rows_per_seq, d)
    return pl.pallas_call(
        _router_kernel,
        grid=(t // tile,),
        in_specs=[pl.BlockSpec((tile, d), lambda i: (i, 0)), pl.BlockSpec(g.shape, lambda i: (0, 0)), mod, mod,
                  pl.BlockSpec((d, LANE), lambda i: (0, 0))],
        out_specs=[pl.BlockSpec((tile, LANE), lambda i: (i, 0)), pl.BlockSpec((SUBLANE, LANE), lambda i: (0, 0)),
                   pl.BlockSpec((tile, d // LANE, LANE), lambda i: (i, 0, 0))],
        out_shape=[jax.ShapeDtypeStruct((t, LANE), F32), jax.ShapeDtypeStruct((SUBLANE, LANE), F32),
                   jax.ShapeDtypeStruct((t, d // LANE, LANE), F32)],
        scratch_shapes=[pltpu.VMEM((1, LANE), F32)],
        compiler_params=_params(("arbitrary",)),
        name="router",
    )(x, g, _mod_arr(shift, rows_per_seq, tile), _mod_arr(scale, rows_per_seq, tile), wr)


def _row_copy(src, dst, sem):
    return pltpu.make_async_copy(src, dst, sem)


def _dispatch_kernel(pos_ref, h_ref, init_ref, xs_ref, sem, *, tile):
    del init_ref
    base = pl.program_id(0) * tile * 2

    def issue(t, carry):
        _row_copy(h_ref.at[t], xs_ref.at[pos_ref[base + 2 * t]], sem).start(priority=0)
        _row_copy(h_ref.at[t], xs_ref.at[pos_ref[base + 2 * t + 1]], sem).start(priority=1)
        return carry

    lax.fori_loop(0, tile, issue, 0)
    for _ in range(2):
        _row_copy(h_ref, xs_ref.at[pl.ds(0, tile)], sem).wait()


def _dispatch(pos, h3, n_rows, tile):
    t, chunks, _ = h3.shape
    grid_spec = pltpu.PrefetchScalarGridSpec(
        num_scalar_prefetch=1,
        grid=(t // tile,),
        in_specs=[pl.BlockSpec((tile, chunks, LANE), lambda i, p: (i, 0, 0)), pl.BlockSpec(memory_space=pl.ANY)],
        out_specs=pl.BlockSpec(memory_space=pl.ANY),
        scratch_shapes=[pltpu.SemaphoreType.DMA(())],
    )
    return pl.pallas_call(
        functools.partial(_dispatch_kernel, tile=tile),
        grid_spec=grid_spec,
        out_shape=jax.ShapeDtypeStruct((n_rows, chunks, LANE), F32),
        input_output_aliases={2: 0},
        compiler_params=_params(("arbitrary",), disable_bounds_checks=True),
        name="moe_dispatch",
    )(pos, h3, jnp.zeros((n_rows, chunks, LANE), F32))


def _moe_ffn_kernel(blk_ref, exp_ref, nt_ref, x_ref, wg_ref, wu_ref, wd_ref, y_ref, h_sc, acc_sc):
    i, f = pl.program_id(0), pl.program_id(1)
    chunks = x_ref.shape[1]

    @pl.when(i < nt_ref[0])
    def _():
        @pl.when(f == 0)
        def _():
            h_sc[...] = jnp.concatenate([x_ref[:, j, :] for j in range(chunks)], axis=1).astype(BF16)
            acc_sc[...] = jnp.zeros(acc_sc.shape, F32)

        h = h_sc[...]
        a = (_silu(_dot(h, wg_ref[0])) * _dot(h, wu_ref[0])).astype(BF16)
        acc_sc[...] += _dot(a, wd_ref[0])

        @pl.when(f == pl.num_programs(1) - 1)
        def _():
            for j in range(chunks):
                y_ref[:, j, :] = acc_sc[:, j * LANE:(j + 1) * LANE]


def _moe_ffn(blk_tab, exp_tab, n_tiles, xs, wg, wu, wd, tile):
    n_rows, chunks, _ = xs.shape
    d = chunks * LANE
    ff = wg.shape[2]
    tf = _col_tile(ff)
    nf = ff // tf
    col = lambda i, f, nt: jnp.where(i < nt[0], f, nf - 1)
    rows = pl.BlockSpec((tile, chunks, LANE), lambda i, f, bt, et, nt: (bt[i], 0, 0))
    grid_spec = pltpu.PrefetchScalarGridSpec(
        num_scalar_prefetch=3,
        grid=(n_rows // tile, nf),
        in_specs=[rows,
                  pl.BlockSpec((1, d, tf), lambda i, f, bt, et, nt: (et[i], 0, col(i, f, nt))),
                  pl.BlockSpec((1, d, tf), lambda i, f, bt, et, nt: (et[i], 0, col(i, f, nt))),
                  pl.BlockSpec((1, tf, d), lambda i, f, bt, et, nt: (et[i], col(i, f, nt), 0))],
        out_specs=rows,
        scratch_shapes=[pltpu.VMEM((tile, d), BF16), pltpu.VMEM((tile, d), F32)],
    )
    return pl.pallas_call(
        _moe_ffn_kernel,
        grid_spec=grid_spec,
        out_shape=jax.ShapeDtypeStruct(xs.shape, F32),
        compiler_params=_params(("arbitrary", "arbitrary")),
        name="moe_swiglu",
    )(blk_tab, exp_tab, n_tiles, xs, wg, wu, wd)


def _combine_kernel(pos_ref, meta_ref, x_ref, gate_ref, ys_ref, o_ref, buf, sem, *, tile):
    i = pl.program_id(0)
    slot = i % 2

    def gather(step, into):
        base = step * tile * 2

        def issue(t, carry):
            _row_copy(ys_ref.at[pos_ref[base + 2 * t]], buf.at[into, 0, t], sem.at[into]).start(priority=0)
            _row_copy(ys_ref.at[pos_ref[base + 2 * t + 1]], buf.at[into, 1, t], sem.at[into]).start(priority=1)
            return carry

        lax.fori_loop(0, tile, issue, 0)

    @pl.when(i == 0)
    def _():
        gather(0, 0)

    @pl.when(i + 1 < pl.num_programs(0))
    def _():
        gather(i + 1, 1 - slot)

    for k in range(2):
        _row_copy(ys_ref.at[pl.ds(0, tile)], buf.at[slot, k], sem.at[slot]).wait()
    meta = meta_ref[...]
    w1, w2 = meta[:, _W1:_W1 + 1], meta[:, _W2:_W2 + 1]
    gate = gate_ref[0]
    for j in range(buf.shape[3]):
        sl = slice(j * LANE, (j + 1) * LANE)
        o_ref[:, sl] = x_ref[:, sl] + gate[:, sl] * (w1 * buf[slot, 0, :, j, :] + w2 * buf[slot, 1, :, j, :])


def _combine(pos, meta, x, gate, ys, rows_per_seq, tile):
    t, d = x.shape
    chunks = d // LANE
    row = lambda n: pl.BlockSpec((tile, n), lambda i, p: (i, 0))
    grid_spec = pltpu.PrefetchScalarGridSpec(
        num_scalar_prefetch=1,
        grid=(t // tile,),
        in_specs=[row(LANE), row(d), _mod_specs(t, tile, rows_per_seq, d), pl.BlockSpec(memory_space=pl.ANY)],
        out_specs=row(d),
        scratch_shapes=[pltpu.VMEM((2, 2, tile, chunks, LANE), F32), pltpu.SemaphoreType.DMA((2,))],
    )
    return pl.pallas_call(
        functools.partial(_combine_kernel, tile=tile),
        grid_spec=grid_spec,
        out_shape=jax.ShapeDtypeStruct((t, d), F32),
        compiler_params=_params(("arbitrary",), disable_bounds_checks=True),
        name="moe_combine",
    )(pos, meta, x, _mod_arr(gate, rows_per_seq, tile), ys)


def _moe(x, g, shift, scale, gate, w_router, wg, wu, wd, layer, rows_per_seq):
    t, d = x.shape
    tile = min(FFN_ROW_TILE, t)
    meta, counts, h3 = _router(x, g, shift, scale, w_router, rows_per_seq)
    n_max = 2 * t // tile + N_EXPERTS
    cnt = counts[0, :N_EXPERTS].astype(jnp.int32)
    tiles = (cnt + tile - 1) // tile
    ends = jnp.cumsum(tiles)
    n_tiles = ends[-1]
    blk_tab = jnp.minimum(jnp.arange(n_max, dtype=jnp.int32), n_tiles - 1)
    exp_tab = jnp.sum(blk_tab[:, None] >= ends[None, :], axis=1).astype(jnp.int32) + layer * N_EXPERTS
    offs = (ends - tiles) * tile
    experts = meta[:, _E1:_E2 + 1].astype(jnp.int32)
    ranks = meta[:, _R1:_R2 + 1].astype(jnp.int32)
    pos = (offs[experts] + ranks).reshape(-1)
    xs = _dispatch(pos, h3, n_max * tile, tile)
    ys = _moe_ffn(blk_tab, exp_tab, n_tiles.reshape(1).astype(jnp.int32), xs, wg, wu, wd, tile)
    return _combine(pos, meta, x, gate, ys, rows_per_seq, tile)


def _softplus(x):
    return jnp.maximum(x, 0.0) + jnp.log1p(jnp.exp(-jnp.abs(x)))


def _ssm_in_kernel(x_ref, g_ref, sh_ref, sc_ref, wz_ref, wx_ref, wdt_ref, cw_ref, cb_ref, dtb_ref,
                   z_ref, xs_ref, b_ref, c_ref, dt_ref, conv_ref, ext_sc, *, tile, tiles_per_seq, d_inner, n_heads):
    i = pl.program_id(0)

    @pl.when(i % tiles_per_seq == 0)
    def _():
        ext_sc[0:SUBLANE, :] = jnp.zeros((SUBLANE, ext_sc.shape[1]), F32)

    h = _modulate(x_ref[...], g_ref[...], sh_ref[0], sc_ref[0]).astype(BF16)
    z_ref[...] = _dot(h, wz_ref[...])
    u = _dot(h, wx_ref[...])
    ext_sc[SUBLANE:SUBLANE + tile, :] = u
    cw = cw_ref[...]
    y = cw[0:1] * ext_sc[SUBLANE - 3:SUBLANE - 3 + tile, :]
    y = y + cw[1:2] * ext_sc[SUBLANE - 2:SUBLANE - 2 + tile, :]
    y = y + cw[2:3] * ext_sc[SUBLANE - 1:SUBLANE - 1 + tile, :]
    y = y + cw[3:4] * u + cb_ref[...]
    tail = ext_sc[tile:tile + SUBLANE, :]
    conv_ref[0] = tail
    ext_sc[0:SUBLANE, :] = tail
    xbc = _silu(y)
    gn = (xbc.shape[1] - d_inner) // 2
    xs_ref[...] = xbc[:, :d_inner]
    b_ref[...] = xbc[:, d_inner:d_inner + gn].astype(BF16)
    c_ref[...] = xbc[:, d_inner + gn:].astype(BF16)
    dt = _softplus(_dot(h, wdt_ref[...]) + dtb_ref[...])
    lane = lax.broadcasted_iota(jnp.int32, dt.shape, 1)
    dt_ref[...] = jnp.where(lane < n_heads, dt, 0.0)


def _ssm_in(x, g, shift, scale, w, rows_per_seq):
    t, d = x.shape
    tile = min(ROW_TILE, rows_per_seq)
    d_inner, conv_dim = w["wz"].shape[1], w["wx"].shape[1]
    gn = (conv_dim - d_inner) // 2
    n_seq = t // rows_per_seq
    tiles_per_seq = rows_per_seq // tile
    full = lambda a: pl.BlockSpec(a.shape, lambda i: (0,) * a.ndim)
    row = lambda n: pl.BlockSpec((tile, n), lambda i: (i, 0))
    mod = _mod_specs(t, tile, rows_per_seq, d)
    ws = [w["wz"], w["wx"], w["wdt"], w["conv_w"], w["conv_b"], w["dt_bias"]]
    return pl.pallas_call(
        functools.partial(_ssm_in_kernel, tile=tile, tiles_per_seq=tiles_per_seq, d_inner=d_inner,
                          n_heads=d_inner // SSM_HEAD_DIM),
        grid=(t // tile,),
        in_specs=[row(d), full(g), mod, mod] + [full(a) for a in ws],
        out_specs=[row(d_inner), row(d_inner), row(gn), row(gn), row(LANE),
                   pl.BlockSpec((1, SUBLANE, conv_dim), lambda i: (i // tiles_per_seq, 0, 0))],
        out_shape=[jax.ShapeDtypeStruct((t, d_inner), F32), jax.ShapeDtypeStruct((t, d_inner), F32),
                   jax.ShapeDtypeStruct((t, gn), BF16), jax.ShapeDtypeStruct((t, gn), BF16),
                   jax.ShapeDtypeStruct((t, LANE), F32), jax.ShapeDtypeStruct((n_seq, SUBLANE, conv_dim), F32)],
        scratch_shapes=[pltpu.VMEM((tile + SUBLANE, conv_dim), F32)],
        compiler_params=_params(("arbitrary",)),
        name="ssm_in_conv",
    )(x, g, _mod_arr(shift, rows_per_seq, tile), _mod_arr(scale, rows_per_seq, tile), *ws)


def _ssd_kernel(xs_ref, b_ref, c_ref, dt_ref, z_ref, alog_ref, e_ref, dskip_ref, ng_ref, y_ref, hout_ref,
                st_sc, y_sc, *, cl, n_groups, pairs_per_group):
    c = pl.program_id(1)

    @pl.when(c == 0)
    def _():
        st_sc[...] = jnp.zeros(st_sc.shape, F32)

    dt = dt_ref[0]
    d_a = dt * (-jnp.exp(alog_ref[...]))
    rowi = lax.broadcasted_iota(jnp.int32, (cl, cl), 0)
    coli = lax.broadcasted_iota(jnp.int32, (cl, cl), 1)
    causal = rowi >= coli
    ltri = jnp.where(causal, 1.0, 0.0).astype(BF16)
    cum = sum(_dot(ltri, part) for part in _split3(d_a))
    cum_t = cum.T
    expand = e_ref[...]
    dt_x = sum(_dot(part, expand) for part in _split3(dt))
    cum_x = sum(_dot(part, expand) for part in _split3(cum))
    xs = xs_ref[0]
    xdt = xs * dt_x
    xdt_b = xdt.astype(BF16)
    w_out = (xdt * jnp.exp(cum_x[cl - 1:cl, :] - cum_x)).astype(BF16)
    grow = jnp.exp(cum_x)
    lane = lax.broadcasted_iota(jnp.int32, (cl, LANE), 1)
    first = lax.broadcasted_iota(jnp.int32, (LANE, 1), 0) < SSM_HEAD_DIM
    for g in range(n_groups):
        bg = b_ref[0, :, g * D_STATE:(g + 1) * D_STATE]
        cg = c_ref[0, :, g * D_STATE:(g + 1) * D_STATE]
        cb = _dot_nt(cg, bg)
        for j in range(pairs_per_group):
            pr = g * pairs_per_group + j
            h0, h1 = 2 * pr, 2 * pr + 1
            sl = slice(pr * LANE, (pr + 1) * LANE)
            seg0 = jnp.exp(jnp.where(causal, cum[:, h0:h0 + 1] - cum_t[h0:h0 + 1, :], -jnp.inf))
            seg1 = jnp.exp(jnp.where(causal, cum[:, h1:h1 + 1] - cum_t[h1:h1 + 1, :], -jnp.inf))
            xp = xdt_b[:, sl]
            y_diag = jnp.where(lane < SSM_HEAD_DIM, _dot((cb * seg0).astype(BF16), xp),
                               _dot((cb * seg1).astype(BF16), xp))
            st = st_sc[pr]
            y_sc[:, sl] = y_diag + _dot_nt(cg, st.astype(BF16)) * grow[:, sl]
            decay = jnp.exp(jnp.where(first, cum_t[h0:h0 + 1, cl - 1:cl], cum_t[h1:h1 + 1, cl - 1:cl]))
            st_sc[pr] = st * decay + _dot_tn(w_out[:, sl], bg)
    zz = z_ref[0]
    y = (y_sc[...] + dskip_ref[...] * xs) * _silu(zz)
    y_ref[0] = _rms(y, ng_ref[...]).astype(y_ref.dtype)

    @pl.when(c == pl.num_programs(1) - 1)
    def _():
        hout_ref[0] = st_sc[...]


def _ssd(xs, bm, cm, dt, z, w, n_seq, seq_len):
    d_inner = xs.shape[1]
    gn = bm.shape[1]
    n_pairs = d_inner // LANE
    cl = min(SSD_CHUNK, seq_len)
    nc = seq_len // cl
    r3 = lambda a: a.reshape(n_seq, seq_len, a.shape[1])
    blk = lambda n: pl.BlockSpec((1, cl, n), lambda b, c: (b, c, 0))
    full = lambda a: pl.BlockSpec(a.shape, lambda b, c: (0,) * a.ndim)
    consts = [w["a_log"], w["expand"], w["d_skip"], w["norm_g"]]
    y, h_out = pl.pallas_call(
        functools.partial(_ssd_kernel, cl=cl, n_groups=SSM_GROUPS, pairs_per_group=n_pairs // SSM_GROUPS),
        grid=(n_seq, nc),
        in_specs=[blk(d_inner), blk(gn), blk(gn), blk(LANE), blk(d_inner)] + [full(a) for a in consts],
        out_specs=[blk(d_inner), pl.BlockSpec((1, n_pairs, LANE, D_STATE), lambda b, c: (b, 0, 0, 0))],
        out_shape=[jax.ShapeDtypeStruct((n_seq, seq_len, d_inner), BF16),
                   jax.ShapeDtypeStruct((n_seq, n_pairs, LANE, D_STATE), F32)],
        scratch_shapes=[pltpu.VMEM((n_pairs, LANE, D_STATE), F32), pltpu.VMEM((cl, d_inner), F32)],
        compiler_params=_params(("arbitrary", "arbitrary")),
        name="ssd_scan",
    )(r3(xs), r3(bm), r3(cm), r3(dt), r3(z), *consts)
    return y.reshape(n_seq * seq_len, d_inner), h_out


def _prep_ssm(w_in, conv_w, conv_b, dt_bias, a_log, d_skip, norm_g, w_out):
    d = w_in.shape[0]
    d_inner = w_out.shape[0]
    n_heads = d_inner // SSM_HEAD_DIM
    conv_dim = conv_w.shape[1]
    pad = LANE - n_heads
    w_dt = w_in[:, d_inner + conv_dim:]
    rep = lambda v: jnp.repeat(v, SSM_HEAD_DIM, axis=-1)
    head_of = jnp.arange(d_inner) // SSM_HEAD_DIM
    return dict(
        wz=w_in[:, :d_inner].astype(BF16),
        wx=w_in[:, d_inner:d_inner + conv_dim].astype(BF16),
        wdt=jnp.concatenate([w_dt, jnp.zeros((d, pad), F32)], axis=1).astype(BF16),
        wdt_x=rep(w_dt).astype(BF16),
        conv_w=conv_w, conv_b=conv_b.reshape(1, conv_dim),
        dt_bias=jnp.concatenate([dt_bias, jnp.zeros((pad,), F32)]).reshape(1, LANE),
        dt_bias_x=rep(dt_bias).reshape(1, d_inner),
        a_log=jnp.concatenate([a_log, jnp.zeros((pad,), F32)]).reshape(1, LANE),
        a_log_x=rep(a_log).reshape(1, d_inner),
        expand=(jnp.arange(LANE)[:, None] == head_of[None, :]).astype(BF16),
        d_skip=rep(d_skip).reshape(1, d_inner),
        norm_g=norm_g.reshape(1, d_inner),
        w_out=w_out.astype(BF16),
    )


def _ssm_step_in_kernel(x_ref, g_ref, sh_ref, sc_ref, wz_ref, wx_ref, wdt_ref, cw_ref, cb_ref, dtb_ref, alog_ref,
                        buf_ref, z_ref, xs_ref, b_ref, c_ref, conv_ref, xdt_t_ref, dec_t_ref, *, d_inner):
    h = _modulate(x_ref[...], g_ref[...], sh_ref[0], sc_ref[0]).astype(BF16)
    z_ref[...] = _dot(h, wz_ref[...])
    u = _dot(h, wx_ref[...])
    cw = cw_ref[...]
    y = cw[0:1] * buf_ref[0] + cw[1:2] * buf_ref[1] + cw[2:3] * buf_ref[2] + cw[3:4] * u + cb_ref[...]
    conv_ref[0] = buf_ref[1]
    conv_ref[1] = buf_ref[2]
    conv_ref[2] = u
    xbc = _silu(y)
    gn = (xbc.shape[1] - d_inner) // 2
    xs = xbc[:, :d_inner]
    xs_ref[...] = xs
    b_ref[...] = xbc[:, d_inner:d_inner + gn]
    c_ref[...] = xbc[:, d_inner + gn:]
    dt = _softplus(_dot(h, wdt_ref[...]) + dtb_ref[...])
    xdt_t_ref[...] = (xs * dt).T
    dec_t_ref[...] = jnp.exp(dt * (-jnp.exp(alog_ref[...]))).T


def _ssm_step_in(x, g, shift, scale, buf_t, w):
    n, d = x.shape
    d_inner, conv_dim = w["wz"].shape[1], w["wx"].shape[1]
    gn = (conv_dim - d_inner) // 2
    ws = [w["wz"], w["wx"], w["wdt_x"], w["conv_w"], w["conv_b"], w["dt_bias_x"], w["a_log_x"], buf_t]
    sds = jax.ShapeDtypeStruct
    return pl.pallas_call(
        functools.partial(_ssm_step_in_kernel, d_inner=d_inner),
        out_shape=[sds((n, d_inner), F32), sds((n, d_inner), F32), sds((n, gn), F32), sds((n, gn), F32),
                   sds((CONV_W - 1, n, conv_dim), F32), sds((d_inner, n), F32), sds((d_inner, n), F32)],
        compiler_params=pltpu.CompilerParams(vmem_limit_bytes=VMEM_LIMIT),
        name="ssm_step_in",
    )(x, g, shift.reshape(1, n, d), scale.reshape(1, n, d), *ws)


def _ssm_step_kernel(st_ref, xdt_t_ref, dec_t_ref, b_ref, c_ref, *rest, n_groups):
    hout_ref, y_t_ref = rest[-2:]
    b = pl.program_id(0)

    @pl.when(b == 0)
    def _():
        y_t_ref[...] = jnp.zeros(y_t_ref.shape, F32)

    rg = st_ref.shape[2] // n_groups
    mine = lax.broadcasted_iota(jnp.int32, (rg, LANE), 1) == b
    for g in range(n_groups):
        rs = slice(g * rg, (g + 1) * rg)
        xcol = jnp.sum(jnp.where(mine, xdt_t_ref[rs, :], 0.0), axis=1, keepdims=True)
        dcol = jnp.sum(jnp.where(mine, dec_t_ref[rs, :], 0.0), axis=1, keepdims=True)
        bg = b_ref[0, :, g * D_STATE:(g + 1) * D_STATE]
        cg = c_ref[0, :, g * D_STATE:(g + 1) * D_STATE]
        s_new = st_ref[0, 0, rs, :] * dcol + xcol * bg
        hout_ref[0, 0, rs, :] = s_new
        ycol = jnp.sum(s_new * cg, axis=1, keepdims=True)
        y_t_ref[rs, :] = jnp.where(mine, ycol, y_t_ref[rs, :])


def _ssm_step(layer, state, xdt_t, dec_t, bm, cm, new_state=None):
    _, n, rows, ns = state.shape
    full = lambda a: pl.BlockSpec(a.shape, lambda b: (0,) * a.ndim)
    seq_row = pl.BlockSpec((1, 1, bm.shape[1]), lambda b: (b, 0, 0))
    slab = pl.BlockSpec((1, 1, rows, ns), lambda b: (layer, b, 0, 0))
    args = [state, xdt_t, dec_t, bm.reshape(n, 1, -1), cm.reshape(n, 1, -1)]
    in_specs = [slab, full(xdt_t), full(dec_t), seq_row, seq_row]
    aliases = {}
    if new_state is not None:
        args.append(new_state)
        in_specs.append(pl.BlockSpec(memory_space=pl.ANY))
        aliases = {len(args) - 1: 0}
    return pl.pallas_call(
        functools.partial(_ssm_step_kernel, n_groups=SSM_GROUPS),
        grid=(n,),
        in_specs=in_specs,
        out_specs=[slab, pl.BlockSpec((rows, n), lambda b: (0, 0))],
        out_shape=[jax.ShapeDtypeStruct(state.shape, F32), jax.ShapeDtypeStruct((rows, n), F32)],
        input_output_aliases=aliases,
        compiler_params=_params(("arbitrary",)),
        name="ssm_step",
    )(*args)


def _ssm_step_out_kernel(y_t_ref, xs_ref, z_ref, dskip_ref, ng_ref, w_ref, x_ref, gate_ref, o_ref):
    y = (y_t_ref[...].T + dskip_ref[...] * xs_ref[...]) * _silu(z_ref[...])
    y = _rms(y, ng_ref[...]).astype(BF16)
    o_ref[...] = x_ref[...] + gate_ref[0] * _dot(y, w_ref[...])


def _ssm_step_out(y_t, xs, z, w, x, gate):
    n, d = x.shape
    return pl.pallas_call(
        _ssm_step_out_kernel,
        out_shape=jax.ShapeDtypeStruct((n, d), F32),
        compiler_params=pltpu.CompilerParams(vmem_limit_bytes=VMEM_LIMIT),
        name="ssm_step_out",
    )(y_t, xs, z, w["d_skip"], w["norm_g"], w["w_out"], x, gate.reshape(1, n, d))


def _final_norm_kernel(x_ref, g_ref, o_ref):
    o_ref[...] = _rms(x_ref[...], g_ref[...])


def _final_norm(x, g):
    t, d = x.shape
    tile = min(FFN_ROW_TILE, t)
    return pl.pallas_call(
        _final_norm_kernel,
        grid=(t // tile,),
        in_specs=[pl.BlockSpec((tile, d), lambda i: (i, 0)), pl.BlockSpec((1, d), lambda i: (0, 0))],
        out_specs=pl.BlockSpec((tile, d), lambda i: (i, 0)),
        out_shape=jax.ShapeDtypeStruct((t, d), F32),
        compiler_params=_params(("arbitrary",)),
        name="final_norm",
    )(x, g.reshape(1, d))


def kernel(x_prompt, x_sample, c_prompt, c_sample, cache_kv_latent, cache_k_rope, page_table, state_ssm, state_conv, w_ada, b_ada, norm_mix_g, norm_ffn_g, norm_final_g, mla_w_in, mla_q_norm_g, mla_kv_norm_g, mla_w_qb, mla_w_kvb, mla_w_o, ssm_w_in, ssm_conv_w, ssm_conv_b, ssm_dt_bias, ssm_a_log, ssm_d, ssm_norm_g, ssm_w_out, ffn_w_gate, ffn_w_up, ffn_w_down, moe_w_router, moe_w_gate, moe_w_up, moe_w_down):
    nb, seq, d = x_prompt.shape
    ns = x_sample.shape[0]
    depth = w_ada.shape[0]
    past_len = page_table.shape[1] * cache_kv_latent.shape[2]
    attn_scale = (QK_NOPE + QK_ROPE) ** -0.5

    xp = x_prompt.reshape(nb * seq, d)
    xs = x_sample.reshape(ns, d)
    mods = _ada(jnp.concatenate([c_prompt, c_sample], axis=0), w_ada, b_ada)
    mods = mods.reshape(depth, nb + ns, 6, d)
    tabs_p = _rope_tables(jnp.arange(seq), attn_scale * LOG2E)
    tabs_s = _rope_tables(jnp.full((ns,), past_len), attn_scale)
    state = state_ssm.reshape(state_ssm.shape[0], ns, -1, D_STATE)
    cache_rope_t = cache_k_rope.transpose(0, 1, 3, 2)
    moe_w = [w.astype(BF16).reshape((-1,) + w.shape[2:]) for w in (moe_w_gate, moe_w_up, moe_w_down)]

    lat_p, rope_p, lat_s, rope_s = [], [], [], []
    ssm_p, conv_p, ssm_s, conv_s = [], [], None, []
    for i in range(depth):
        j = i // 2
        mp = [mods[i, :nb, k] for k in range(6)]
        ms = [mods[i, nb:, k] for k in range(6)]
        g_mix = norm_mix_g[i].reshape(1, d)
        g_ffn = norm_ffn_g[i].reshape(1, d)
        if i % 2 == 0:
            w = _prep_mla(mla_w_in[j], mla_w_qb[j], mla_w_kvb[j])
            qg = mla_q_norm_g[j].reshape(1, -1)
            kvg = mla_kv_norm_g[j].reshape(1, -1)
            w_o = mla_w_o[j].astype(BF16)
            ckv, kr, q, k, vt = _mla_proj(xp, g_mix, mp[0], mp[1], tabs_p, qg, kvg, w, seq, True)
            o_t = _flash(q, k, vt.reshape(MLA_HEADS, V_DIM, nb * seq), nb, seq)
            xp = _proj_res(o_t.reshape(MLA_HEADS * V_DIM, nb * seq), w_o, xp, mp[2], seq, transposed=True)
            lat_p.append(ckv.reshape(nb, seq, -1))
            rope_p.append(kr.reshape(nb, seq, -1))

            ckv, kr, q = _mla_proj(xs, g_mix, ms[0], ms[1], tabs_s, qg, kvg, w, 1, False)
            kv_rank = ckv.shape[1]
            qlat = _headmat(q, w["wabs"], LANE, kv_rank)
            o_lat = _decode_attn(j, page_table, qlat.reshape(ns, MLA_HEADS, kv_rank), q.reshape(ns, MLA_HEADS, LANE),
                                 ckv.reshape(ns, 1, kv_rank), kr.reshape(ns, 1, QK_ROPE), cache_kv_latent,
                                 cache_rope_t)
            o = _headmat(o_lat.reshape(ns, MLA_HEADS * kv_rank), w["wvbd"], 2 * kv_rank, 2 * V_DIM)
            xs = _proj_res(o, w_o, xs, ms[2], 1)
            lat_s.append(ckv.reshape(ns, 1, -1))
            rope_s.append(kr.reshape(ns, 1, -1))
        else:
            w = _prep_ssm(ssm_w_in[j], ssm_conv_w[j], ssm_conv_b[j], ssm_dt_bias[j], ssm_a_log[j], ssm_d[j],
                          ssm_norm_g[j], ssm_w_out[j])
            z, xin, bm, cm, dt, conv = _ssm_in(xp, g_mix, mp[0], mp[1], w, seq)
            y, h_new = _ssd(xin, bm, cm, dt, z, w, nb, seq)
            xp = _proj_res(y, w["w_out"], xp, mp[2], seq)
            ssm_p.append(h_new.reshape(nb, -1, SSM_HEAD_DIM, D_STATE))
            conv_p.append(conv[:, SUBLANE - (CONV_W - 1):])

            buf_t = state_conv[j].transpose(1, 0, 2)
            z, xin, bm, cm, conv, xdt_t, dec_t = _ssm_step_in(xs, g_mix, ms[0], ms[1], buf_t, w)
            ssm_s, y_t = _ssm_step(j, state, xdt_t, dec_t, bm, cm, ssm_s)
            xs = _ssm_step_out(y_t, xin, z, w, xs, ms[2])
            conv_s.append(conv.transpose(1, 0, 2))
        if i % 2 == 0:
            wg = ffn_w_gate[j].astype(BF16)
            wu = ffn_w_up[j].astype(BF16)
            wd = ffn_w_down[j].astype(BF16)
            xp = _ffn(xp, g_ffn, mp[3], mp[4], mp[5], wg, wu, wd, seq)
            xs = _ffn(xs, g_ffn, ms[3], ms[4], ms[5], wg, wu, wd, 1)
        else:
            xp = _moe(xp, g_ffn, mp[3], mp[4], mp[5], moe_w_router[j], *moe_w, j, seq)
            xs = _moe(xs, g_ffn, ms[3], ms[4], ms[5], moe_w_router[j], *moe_w, j, 1)
    y_prompt = _final_norm(xp, norm_final_g).reshape(nb, seq, d)
    y_sample = _final_norm(xs, norm_final_g).reshape(ns, 1, d)
    return (y_prompt, y_sample, jnp.stack(lat_p), jnp.stack(rope_p), jnp.stack(lat_s), jnp.stack(rope_s),
            jnp.stack(ssm_p), jnp.stack(conv_p), ssm_s.reshape(state_ssm.shape), jnp.stack(conv_s))
```

```python
import functools
import math

import jax
import jax.numpy as jnp
from jax import lax
from jax.experimental import pallas as pl
from jax.experimental.pallas import tpu as pltpu

F32 = jnp.float32
BF16 = jnp.bfloat16

RMS_EPS = 1e-6
MLA_HEADS = 16
QK_NOPE = 64
QK_ROPE = 32
V_DIM = 64
ROPE_THETA = 10000.0
PAGE_SIZE = 128
SSM_HEAD_DIM = 64
SSM_GROUPS = 4
D_STATE = 128
CONV_W = 4
SSD_CHUNK = 128
N_EXPERTS = 8

LANE = 128
SUBLANE = 8
VMEM_LIMIT = 56 << 20

ROW_TILE = 256
FFN_ROW_TILE = 512
DENSE_FFN_ROW_TILE = 1024
FFN_COL_TILE = 1408
ATTN_TILE = 512
NEG = -1e30
LOG2E = math.log2(math.e)


def _params(sem, **kw):
    return pltpu.CompilerParams(dimension_semantics=sem, vmem_limit_bytes=VMEM_LIMIT, **kw)


def _dot(a, b):
    return jnp.dot(a, b, preferred_element_type=F32)


def _dot_nt(a, b):
    return lax.dot_general(a, b, (((1,), (1,)), ((), ())), preferred_element_type=F32)


def _dot_tn(a, b):
    return lax.dot_general(a, b, (((0,), (0,)), ((), ())), preferred_element_type=F32)


def _split3(v):
    hi = v.astype(BF16)
    r = v - hi.astype(F32)
    mid = r.astype(BF16)
    lo = (r - mid.astype(F32)).astype(BF16)
    return hi, mid, lo


def _silu(x):
    return x * jax.nn.sigmoid(x)


def _rms(x, g):
    return (x * lax.rsqrt(jnp.mean(x * x, axis=-1, keepdims=True) + RMS_EPS)) * g


def _modulate(x, g, shift, scale):
    return _rms(x, g) * (1.0 + scale) + shift


def _mod_specs(n_rows, tile, rows_per_seq, d):
    if rows_per_seq == 1:
        return pl.BlockSpec((1, tile, d), lambda i, *_: (i, 0, 0))
    tiles_per_seq = rows_per_seq // tile
    return pl.BlockSpec((1, 1, d), lambda i, *_: (i // tiles_per_seq, 0, 0))


def _mod_arr(m, rows_per_seq, tile):
    if rows_per_seq == 1:
        return m.reshape(m.shape[0] // tile, tile, m.shape[1])
    return m.reshape(m.shape[0], 1, m.shape[1])


def _ada_kernel(c_ref, w_ref, b_ref, o_ref):
    s = _silu(c_ref[...]).astype(BF16)
    o_ref[0] = _dot(s, w_ref[0].astype(BF16)) + b_ref[0]


def _ada(c_all, w_ada, b_ada):
    depth, d, d6 = w_ada.shape
    n = c_all.shape[0]
    return pl.pallas_call(
        _ada_kernel,
        grid=(depth, d6 // d),
        in_specs=[pl.BlockSpec((n, d), lambda i, k: (0, 0)),
                  pl.BlockSpec((1, d, d), lambda i, k: (i, 0, k)),
                  pl.BlockSpec((1, 1, d), lambda i, k: (i, 0, k))],
        out_specs=pl.BlockSpec((1, n, d), lambda i, k: (i, 0, k)),
        out_shape=jax.ShapeDtypeStruct((depth, n, d6), F32),
        compiler_params=_params(("arbitrary", "arbitrary")),
        name="adaln",
    )(c_all, w_ada, b_ada.reshape(depth, 1, d6))


def _mla_proj_kernel(x_ref, g_ref, sh_ref, sc_ref, ct_ref, st_ref, cts_ref, sts_ref, wcq_ref, wckv_ref,
                     wkra_ref, wkrb_ref, qg_ref, kvg_ref, wqa_ref, wqb_ref, *rest, heads, with_kv):
    if with_kv:
        wka_ref, wvt_ref, ckv_ref, kr_ref, q_ref, k_ref, vt_ref = rest
    else:
        ckv_ref, kr_ref, q_ref = rest
    h = _modulate(x_ref[...], g_ref[...], sh_ref[0], sc_ref[0]).astype(BF16)
    cq = _rms(_dot(h, wcq_ref[...]), qg_ref[...]).astype(BF16)
    ckv = _rms(_dot(h, wckv_ref[...]), kvg_ref[...])
    kr = _dot(h, wkra_ref[...]) * ct_ref[...] + _dot(h, wkrb_ref[...]) * st_ref[...]
    ckv_ref[...] = ckv
    kr_ref[...] = kr[:, :QK_ROPE]
    rep = lambda t: jnp.concatenate([t] * heads, axis=1)
    q = (_dot(cq, wqa_ref[...]) * rep(cts_ref[...]) + _dot(cq, wqb_ref[...]) * rep(sts_ref[...])).astype(BF16)
    if not with_kv:
        q_ref[...] = q
        return
    ckvb = ckv.astype(BF16)
    k = (_dot(ckvb, wka_ref[...]) + rep(kr)).astype(BF16)
    for hd in range(heads):
        q_ref[hd] = q[:, hd * LANE:(hd + 1) * LANE]
        k_ref[hd] = k[:, hd * LANE:(hd + 1) * LANE]
    vt_ref[...] = _dot_nt(wvt_ref[...], ckvb).astype(BF16)


def _rot_cols(w):
    half = w.shape[-1] // 2
    return jnp.concatenate([-w[..., half:], w[..., :half]], axis=-1)


def _prep_mla(w_in, w_qb, w_kvb):
    d = w_in.shape[0]
    q_rank = w_qb.shape[0]
    kv_rank = w_kvb.shape[0]
    hd = LANE
    pad = hd - QK_ROPE - QK_NOPE
    w_kr = w_in[:, q_rank + kv_rank:]
    zk = jnp.zeros((d, hd - QK_ROPE), F32)
    wq = w_qb.reshape(q_rank, MLA_HEADS, QK_NOPE + QK_ROPE)
    q_nope, q_pe = wq[..., :QK_NOPE], wq[..., QK_NOPE:]
    zq = jnp.zeros((q_rank, MLA_HEADS, pad), F32)
    wkv = w_kvb.reshape(kv_rank, MLA_HEADS, QK_NOPE + V_DIM)
    k_nope, v = wkv[..., :QK_NOPE], wkv[..., QK_NOPE:]
    return dict(
        wcq=w_in[:, :q_rank].astype(BF16),
        wckv=w_in[:, q_rank:q_rank + kv_rank].astype(BF16),
        wkra=jnp.concatenate([w_kr, zk], axis=1).astype(BF16),
        wkrb=jnp.concatenate([_rot_cols(w_kr), zk], axis=1).astype(BF16),
        wqa=jnp.concatenate([q_pe, q_nope, zq], axis=-1).reshape(q_rank, MLA_HEADS * hd).astype(BF16),
        wqb=jnp.concatenate([_rot_cols(q_pe), jnp.zeros((q_rank, MLA_HEADS, hd - QK_ROPE), F32)],
                            axis=-1).reshape(q_rank, MLA_HEADS * hd).astype(BF16),
        wka=jnp.concatenate([jnp.zeros((kv_rank, MLA_HEADS, QK_ROPE), F32), k_nope,
                             jnp.zeros((kv_rank, MLA_HEADS, pad), F32)],
                            axis=-1).reshape(kv_rank, MLA_HEADS * hd).astype(BF16),
        wvt=v.reshape(kv_rank, MLA_HEADS * V_DIM).T.astype(BF16),
        wabs=jnp.concatenate([jnp.zeros((MLA_HEADS, QK_ROPE, kv_rank), F32), k_nope.transpose(1, 2, 0),
                              jnp.zeros((MLA_HEADS, pad, kv_rank), F32)], axis=1).astype(BF16),
        wvbd=_block_diag_pairs(v.transpose(1, 0, 2)).astype(BF16),
    )


def _block_diag_pairs(v):
    h, r, dv = v.shape
    v = v.reshape(h // 2, 2, r, dv)
    z = jnp.zeros((h // 2, r, dv), v.dtype)
    top = jnp.concatenate([v[:, 0], z], axis=-1)
    bot = jnp.concatenate([z, v[:, 1]], axis=-1)
    return jnp.concatenate([top, bot], axis=1)


def _rope_tables(pos, scale):
    half = QK_ROPE // 2
    inv = ROPE_THETA ** (-jnp.arange(half, dtype=F32) / half)
    ang = pos.astype(F32)[:, None] * inv[None, :]
    cos, sin = jnp.cos(ang), jnp.sin(ang)
    n = pos.shape[0]
    ct = jnp.concatenate([cos, cos, jnp.ones((n, LANE - QK_ROPE), F32)], axis=1)
    st = jnp.concatenate([sin, sin, jnp.zeros((n, LANE - QK_ROPE), F32)], axis=1)
    return ct, st, ct * scale, st * scale


def _mla_proj(x, g, shift, scale, tabs, qg, kvg, w, rows_per_seq, with_kv):
    t, d = x.shape
    tile = min(ROW_TILE, t)
    n_tab_tiles = tabs[0].shape[0] // tile
    heads = MLA_HEADS
    kv_rank = w["wckv"].shape[1]
    full = lambda a: pl.BlockSpec(a.shape, lambda i: (0,) * a.ndim)
    row = lambda n: pl.BlockSpec((tile, n), lambda i: (i, 0))
    tab = pl.BlockSpec((tile, LANE), lambda i: (i % n_tab_tiles, 0))
    mod = _mod_specs(t, tile, rows_per_seq, d)
    ws = [w["wcq"], w["wckv"], w["wkra"], w["wkrb"], qg, kvg, w["wqa"], w["wqb"]]
    outs = [jax.ShapeDtypeStruct((t, kv_rank), F32), jax.ShapeDtypeStruct((t, QK_ROPE), F32)]
    out_specs = [row(kv_rank), row(QK_ROPE)]
    if with_kv:
        ws += [w["wka"], w["wvt"]]
        head_major = pl.BlockSpec((heads, tile, LANE), lambda i: (0, i, 0))
        outs += [jax.ShapeDtypeStruct((heads, t, LANE), BF16), jax.ShapeDtypeStruct((heads, t, LANE), BF16),
                 jax.ShapeDtypeStruct((heads * V_DIM, t), BF16)]
        out_specs += [head_major, head_major, pl.BlockSpec((heads * V_DIM, tile), lambda i: (0, i))]
    else:
        outs.append(jax.ShapeDtypeStruct((t, heads * LANE), BF16))
        out_specs.append(row(heads * LANE))
    return pl.pallas_call(
        functools.partial(_mla_proj_kernel, heads=heads, with_kv=with_kv),
        grid=(t // tile,),
        in_specs=[row(d), full(g), mod, mod, tab, tab, tab, tab] + [full(a) for a in ws],
        out_specs=out_specs,
        out_shape=outs,
        compiler_params=_params(("arbitrary",)),
        name="mla_proj",
    )(x, g, _mod_arr(shift, rows_per_seq, tile), _mod_arr(scale, rows_per_seq, tile), *tabs, *ws)


def _flash_kernel(qi_tab, ki_tab, q_ref, k_ref, vt_ref, o_ref, m_sc, l_sc, acc_sc, s0_sc, s1_sc, *, tile, heads):
    t = pl.program_id(1)
    qi = qi_tab[t]
    ki = ki_tab[t]

    @pl.when(ki == 0)
    def _():
        m_sc[...] = jnp.full(m_sc.shape, NEG, F32)
        l_sc[...] = jnp.zeros(l_sc.shape, F32)
        acc_sc[...] = jnp.zeros(acc_sc.shape, F32)

    def scores(h, s_sc):
        s_sc[...] = _dot_nt(k_ref[h], q_ref[h])

    def sweep(diagonal):
        if diagonal:
            visible = (lax.broadcasted_iota(jnp.int32, (tile, tile), 0)
                       <= lax.broadcasted_iota(jnp.int32, (tile, tile), 1))

        def softmax_pv(h, s_sc):
            s = s_sc[...]
            if diagonal:
                s = jnp.where(visible, s, NEG)
            m_prev = m_sc[h]
            m_new = jnp.maximum(m_prev, jnp.max(s, axis=0, keepdims=True))
            alpha = jnp.exp2(m_prev - m_new)
            p = jnp.exp2(s - m_new)
            l_new = alpha * l_sc[h] + jnp.sum(p, axis=0, keepdims=True)
            acc = alpha * acc_sc[h] + _dot(vt_ref[h], p.astype(BF16))
            if diagonal:
                o_ref[h] = (acc / l_new).astype(o_ref.dtype)
            else:
                m_sc[h] = m_new
                l_sc[h] = l_new
                acc_sc[h] = acc

        scores(0, s0_sc)

        def pair(i, carry):
            h0 = 2 * i
            scores(h0 + 1, s1_sc)
            softmax_pv(h0, s0_sc)
            scores(jnp.minimum(h0 + 2, heads - 1), s0_sc)
            softmax_pv(h0 + 1, s1_sc)
            return carry

        lax.fori_loop(0, heads // 2, pair, 0)

    @pl.when(ki < qi)
    def _():
        sweep(False)

    @pl.when(ki == qi)
    def _():
        sweep(True)


def _flash(q, k, vt, n_seq, seq_len):
    heads = q.shape[0]
    tile = min(ATTN_TILE, seq_len)
    nq = seq_len // tile
    pairs = [(i, j) for i in range(nq) for j in range(i + 1)]
    qi_tab = jnp.array([p[0] for p in pairs], jnp.int32)
    ki_tab = jnp.array([p[1] for p in pairs], jnp.int32)
    grid_spec = pltpu.PrefetchScalarGridSpec(
        num_scalar_prefetch=2,
        grid=(n_seq, len(pairs)),
        in_specs=[pl.BlockSpec((heads, tile, LANE), lambda b, t, qt, kt: (0, b * nq + qt[t], 0)),
                  pl.BlockSpec((heads, tile, LANE), lambda b, t, qt, kt: (0, b * nq + kt[t], 0)),
                  pl.BlockSpec((heads, V_DIM, tile), lambda b, t, qt, kt: (0, 0, b * nq + kt[t]))],
        out_specs=pl.BlockSpec((heads, V_DIM, tile), lambda b, t, qt, kt: (0, 0, b * nq + qt[t])),
        scratch_shapes=[pltpu.VMEM((heads, 1, tile), F32), pltpu.VMEM((heads, 1, tile), F32),
                        pltpu.VMEM((heads, V_DIM, tile), F32), pltpu.VMEM((tile, tile), F32),
                        pltpu.VMEM((tile, tile), F32)],
    )
    return pl.pallas_call(
        functools.partial(_flash_kernel, tile=tile, heads=heads),
        grid_spec=grid_spec,
        out_shape=jax.ShapeDtypeStruct(vt.shape, BF16),
        compiler_params=_params(("arbitrary", "arbitrary")),
        name="flash_attn",
    )(qi_tab, ki_tab, q, k, vt)


def _headmat_kernel(a_ref, w_ref, o_ref):
    o_ref[...] = _dot(a_ref[...], w_ref[0]).astype(o_ref.dtype)


def _headmat(a, w, in_w, out_w):
    n, rows = w.shape[0], a.shape[0]
    return pl.pallas_call(
        _headmat_kernel,
        grid=(n,),
        in_specs=[pl.BlockSpec((rows, in_w), lambda i: (0, i)),
                  pl.BlockSpec((1, in_w, out_w), lambda i: (i, 0, 0))],
        out_specs=pl.BlockSpec((rows, out_w), lambda i: (0, i)),
        out_shape=jax.ShapeDtypeStruct((rows, n * out_w), BF16),
        compiler_params=_params(("arbitrary",)),
        name="head_matmul",
    )(a, w)


def _decode_kernel(pt_ref, qlat_ref, q_ref, ckv_ref, kr_ref, lat_hbm, rope_hbm, o_ref, latbuf, ropebuf, kcat, rcat,
                   sem, *, layer, pages):
    b = pl.program_id(0)
    slot = b % 2

    def fetch(seq, into):
        def issue(i, carry):
            pg = pt_ref[seq * pages + i]
            pltpu.make_async_copy(lat_hbm.at[layer, pg], latbuf.at[into, i], sem.at[into]).start(priority=0)
            pltpu.make_async_copy(rope_hbm.at[layer, pg], ropebuf.at[into, i], sem.at[into]).start(priority=1)
            return carry

        lax.fori_loop(0, pages, issue, 0)

    @pl.when(b == 0)
    def _():
        fetch(0, 0)

    @pl.when(b + 1 < pl.num_programs(0))
    def _():
        fetch(b + 1, 1 - slot)

    pltpu.make_async_copy(lat_hbm.at[layer, pl.ds(0, pages)], latbuf.at[slot], sem.at[slot]).wait()
    pltpu.make_async_copy(rope_hbm.at[layer, pl.ds(0, pages)], ropebuf.at[slot], sem.at[slot]).wait()
    for i in range(pages):
        kcat[i * PAGE_SIZE:(i + 1) * PAGE_SIZE, :] = latbuf[slot, i].astype(BF16)
        rcat[:, i * PAGE_SIZE:(i + 1) * PAGE_SIZE] = ropebuf[slot, i].astype(BF16)
    ql = qlat_ref[0]
    qp = q_ref[0][:, :QK_ROPE]
    keys = kcat[...]
    s = _dot_nt(ql, keys) + _dot(qp, rcat[...])
    kl = ckv_ref[0].astype(BF16).astype(F32)
    kp = kr_ref[0].astype(BF16).astype(F32)
    s1 = (jnp.sum(ql.astype(F32) * kl, axis=1, keepdims=True)
          + jnp.sum(qp.astype(F32) * kp, axis=1, keepdims=True))
    m = jnp.maximum(jnp.max(s, axis=1, keepdims=True), s1)
    p = jnp.exp(s - m)
    p1 = jnp.exp(s1 - m)
    l = jnp.sum(p, axis=1, keepdims=True) + p1
    acc = _dot(p.astype(BF16), keys) + p1.astype(BF16).astype(F32) * kl
    o_ref[0] = (acc / l).astype(o_ref.dtype)


def _decode_attn(layer, page_table, qlat, q, ckv, kr, cache_lat, cache_rope):
    b, heads, rank = qlat.shape
    pages = page_table.shape[1]
    seq = lambda n, w: pl.BlockSpec((1, n, w), lambda b_, pt: (b_, 0, 0))
    hbm = pl.BlockSpec(memory_space=pl.ANY)
    grid_spec = pltpu.PrefetchScalarGridSpec(
        num_scalar_prefetch=1,
        grid=(b,),
        in_specs=[seq(heads, rank), seq(heads, LANE), seq(1, rank), seq(1, QK_ROPE), hbm, hbm],
        out_specs=seq(heads, rank),
        scratch_shapes=[pltpu.VMEM((2, pages, PAGE_SIZE, rank), F32), pltpu.VMEM((2, pages, QK_ROPE, PAGE_SIZE), F32),
                        pltpu.VMEM((pages * PAGE_SIZE, rank), BF16), pltpu.VMEM((QK_ROPE, pages * PAGE_SIZE), BF16),
                        pltpu.SemaphoreType.DMA((2,))],
    )
    return pl.pallas_call(
        functools.partial(_decode_kernel, layer=layer, pages=pages),
        grid_spec=grid_spec,
        out_shape=jax.ShapeDtypeStruct((b, heads, rank), BF16),
        compiler_params=_params(("arbitrary",)),
        name="decode_attn",
    )(page_table.reshape(-1), qlat, q, ckv, kr, cache_lat, cache_rope)


def _proj_res_kernel(a_ref, w_ref, x_ref, gate_ref, o_ref, *, transposed):
    y = _dot_tn(a_ref[...], w_ref[...]) if transposed else _dot(a_ref[...], w_ref[...])
    o_ref[...] = x_ref[...] + gate_ref[0] * y


def _proj_res(a, w, x, gate, rows_per_seq, transposed=False):
    t, d = x.shape
    k = w.shape[0]
    tile = min(FFN_ROW_TILE, t)
    a_spec = pl.BlockSpec((k, tile), lambda i: (0, i)) if transposed else pl.BlockSpec((tile, k), lambda i: (i, 0))
    return pl.pallas_call(
        functools.partial(_proj_res_kernel, transposed=transposed),
        grid=(t // tile,),
        in_specs=[a_spec, pl.BlockSpec((k, d), lambda i: (0, 0)),
                  pl.BlockSpec((tile, d), lambda i: (i, 0)), _mod_specs(t, tile, rows_per_seq, d)],
        out_specs=pl.BlockSpec((tile, d), lambda i: (i, 0)),
        out_shape=jax.ShapeDtypeStruct((t, d), F32),
        compiler_params=_params(("arbitrary",)),
        name="proj_residual",
    )(a, w, x, _mod_arr(gate, rows_per_seq, tile))


def _ffn_kernel(x_ref, g_ref, sh_ref, sc_ref, gate_ref, wg_ref, wu_ref, wd_ref, o_ref, h_sc, acc_sc):
    f = pl.program_id(1)

    @pl.when(f == 0)
    def _():
        h_sc[...] = _modulate(x_ref[...], g_ref[...], sh_ref[0], sc_ref[0]).astype(BF16)
        acc_sc[...] = jnp.zeros(acc_sc.shape, F32)

    h = h_sc[...]
    a = (_silu(_dot(h, wg_ref[...])) * _dot(h, wu_ref[...])).astype(BF16)
    acc_sc[...] += _dot(a, wd_ref[...])

    @pl.when(f == pl.num_programs(1) - 1)
    def _():
        o_ref[...] = x_ref[...] + gate_ref[0] * acc_sc[...]


def _col_tile(ff):
    return FFN_COL_TILE if ff % FFN_COL_TILE == 0 else ff


def _ffn(x, g, shift, scale, gate, wg, wu, wd, rows_per_seq):
    t, d = x.shape
    ff = wg.shape[1]
    tile = min(DENSE_FFN_ROW_TILE, t if rows_per_seq == 1 else rows_per_seq)
    tf = _col_tile(ff)
    mod = _mod_specs(t, tile, rows_per_seq, d)
    row = pl.BlockSpec((tile, d), lambda i, f: (i, 0))
    return pl.pallas_call(
        _ffn_kernel,
        grid=(t // tile, ff // tf),
        in_specs=[row, pl.BlockSpec(g.shape, lambda i, f: (0, 0)), mod, mod, mod,
                  pl.BlockSpec((d, tf), lambda i, f: (0, f)), pl.BlockSpec((d, tf), lambda i, f: (0, f)),
                  pl.BlockSpec((tf, d), lambda i, f: (f, 0))],
        out_specs=row,
        out_shape=jax.ShapeDtypeStruct((t, d), F32),
        scratch_shapes=[pltpu.VMEM((tile, d), BF16), pltpu.VMEM((tile, d), F32)],
        compiler_params=_params(("arbitrary", "arbitrary")),
        name="swiglu",
    )(x, g, *[_mod_arr(m, rows_per_seq, tile) for m in (shift, scale, gate)], wg, wu, wd)


_E1, _E2, _R1, _R2, _W1, _W2 = range(6)


def _router_kernel(x_ref, g_ref, sh_ref, sc_ref, wr_ref, meta_ref, cnt_ref, h3_ref, cnt_sc):
    i = pl.program_id(0)

    @pl.when(i == 0)
    def _():
        cnt_sc[...] = jnp.zeros(cnt_sc.shape, F32)

    h = _modulate(x_ref[...], g_ref[...], sh_ref[0], sc_ref[0])
    for j in range(h3_ref.shape[1]):
        h3_ref[:, j, :] = h[:, j * LANE:(j + 1) * LANE]
    hi = h.astype(BF16)
    lo = (h - hi.astype(F32)).astype(BF16)
    w = wr_ref[...]
    whi = w.astype(BF16)
    wlo = (w - whi.astype(F32)).astype(BF16)
    logits = _dot(hi, whi) + _dot(lo, whi) + _dot(hi, wlo)
    lane = lax.broadcasted_iota(jnp.int32, logits.shape, 1).astype(F32)
    logits = jnp.where(lane < N_EXPERTS, logits, NEG)
    p = jnp.exp(logits - jnp.max(logits, axis=1, keepdims=True))
    p = p / jnp.sum(p, axis=1, keepdims=True)
    p1 = jnp.max(p, axis=1, keepdims=True)
    i1 = jnp.min(jnp.where(p == p1, lane, float(LANE)), axis=1, keepdims=True)
    rest = jnp.where(lane == i1, -1.0, p)
    p2 = jnp.max(rest, axis=1, keepdims=True)
    i2 = jnp.min(jnp.where(rest == p2, lane, float(LANE)), axis=1, keepdims=True)
    den = p1 + p2
    hit1, hit2 = lane == i1, lane == i2
    onehot = jnp.where(hit1 | hit2, 1.0, 0.0)
    tile = onehot.shape[0]
    ltri = jnp.where(lax.broadcasted_iota(jnp.int32, (tile, tile), 0)
                     >= lax.broadcasted_iota(jnp.int32, (tile, tile), 1), 1.0, 0.0).astype(BF16)
    incl = _dot(ltri, onehot.astype(BF16))
    before = cnt_sc[...] + incl - onehot
    r1 = jnp.sum(jnp.where(hit1, before, 0.0), axis=1, keepdims=True)
    r2 = jnp.sum(jnp.where(hit2, before, 0.0), axis=1, keepdims=True)
    cnt = cnt_sc[...] + incl[tile - 1:tile, :]
    cnt_sc[...] = cnt
    cnt_ref[...] = jnp.broadcast_to(cnt, cnt_ref.shape)
    rec = jnp.zeros(logits.shape, F32)
    for ln, val in ((_E1, i1), (_E2, i2), (_R1, r1), (_R2, r2), (_W1, p1 / den), (_W2, p2 / den)):
        rec = jnp.where(lane == ln, val, rec)
    meta_ref[...] = rec


def _router(x, g, shift, scale, w_router, rows_per_seq):
    t, d = x.shape
    tile = min(FFN_ROW_TILE, t)
    wr = jnp.concatenate([w_router, jnp.zeros((d, LANE - w_router.shape[1]), F32)], axis=1)
    mod = _mod_specs(t, tile, rows_per_seq, d)
    return pl.pallas_call(
        _router_kernel,
        grid=(t // tile,),
        in_specs=[pl.BlockSpec((tile, d), lambda i: (i, 0)), pl.BlockSpec(g.shape, lambda i: (0, 0)), mod, mod,
                  pl.BlockSpec((d, LANE), lambda i: (0, 0))],
        out_specs=[pl.BlockSpec((tile, LANE), lambda i: (i, 0)), pl.BlockSpec((SUBLANE, LANE), lambda i: (0, 0)),
                   pl.BlockSpec((tile, d // LANE, LANE), lambda i: (i, 0, 0))],
        out_shape=[jax.ShapeDtypeStruct((t, LANE), F32), jax.ShapeDtypeStruct((SUBLANE, LANE), F32),
                   jax.ShapeDtypeStruct((t, d // LANE, LANE), F32)],
        scratch_shapes=[pltpu.VMEM((1, LANE), F32)],
        compiler_params=_params(("arbitrary",)),
        name="router",
    )(x, g, _mod_arr(shift, rows_per_seq, tile), _mod_arr(scale, rows_per_seq, tile), wr)


def _row_copy(src, dst, sem):
    return pltpu.make_async_copy(src, dst, sem)


def _dispatch_kernel(pos_ref, tail_ref, h_ref, xs_ref, zero_sc, sem, *, tile):
    i = pl.program_id(0)
    base = i * tile * 2

    @pl.when(i == 0)
    def _():
        zero_sc[...] = jnp.zeros(zero_sc.shape, F32)
        for e in range(N_EXPERTS):
            fill = _row_copy(zero_sc, xs_ref.at[pl.ds(tail_ref[e], tile)], sem)
            fill.start()
            fill.wait()

    def issue(t, carry):
        _row_copy(h_ref.at[t], xs_ref.at[pos_ref[base + 2 * t]], sem).start(priority=0)
        _row_copy(h_ref.at[t], xs_ref.at[pos_ref[base + 2 * t + 1]], sem).start(priority=1)
        return carry

    lax.fori_loop(0, tile, issue, 0)
    for _ in range(2):
        _row_copy(h_ref, xs_ref.at[pl.ds(0, tile)], sem).wait()


def _dispatch(pos, tails, h3, n_rows, tile):
    t, chunks, _ = h3.shape
    grid_spec = pltpu.PrefetchScalarGridSpec(
        num_scalar_prefetch=2,
        grid=(t // tile,),
        in_specs=[pl.BlockSpec((tile, chunks, LANE), lambda i, p, tl: (i, 0, 0))],
        out_specs=pl.BlockSpec(memory_space=pl.ANY),
        scratch_shapes=[pltpu.VMEM((tile, chunks, LANE), F32), pltpu.SemaphoreType.DMA(())],
    )
    return pl.pallas_call(
        functools.partial(_dispatch_kernel, tile=tile),
        grid_spec=grid_spec,
        out_shape=jax.ShapeDtypeStruct((n_rows, chunks, LANE), F32),
        compiler_params=_params(("arbitrary",), disable_bounds_checks=True),
        name="moe_dispatch",
    )(pos, tails, h3)


def _moe_ffn_kernel(blk_ref, exp_ref, nt_ref, x_ref, wg_ref, wu_ref, wd_ref, y_ref, h_sc, acc_sc):
    i, f = pl.program_id(0), pl.program_id(1)
    chunks = x_ref.shape[1]

    @pl.when(i < nt_ref[0])
    def _():
        @pl.when(f == 0)
        def _():
            h_sc[...] = jnp.concatenate([x_ref[:, j, :] for j in range(chunks)], axis=1).astype(BF16)
            acc_sc[...] = jnp.zeros(acc_sc.shape, F32)

        h = h_sc[...]
        a = (_silu(_dot(h, wg_ref[0])) * _dot(h, wu_ref[0])).astype(BF16)
        acc_sc[...] += _dot(a, wd_ref[0])

        @pl.when(f == pl.num_programs(1) - 1)
        def _():
            for j in range(chunks):
                y_ref[:, j, :] = acc_sc[:, j * LANE:(j + 1) * LANE]


def _moe_ffn(blk_tab, exp_tab, n_tiles, xs, wg, wu, wd, tile):
    n_rows, chunks, _ = xs.shape
    d = chunks * LANE
    ff = wg.shape[2]
    tf = _col_tile(ff)
    nf = ff // tf
    col = lambda i, f, nt: jnp.where(i < nt[0], f, nf - 1)
    rows = pl.BlockSpec((tile, chunks, LANE), lambda i, f, bt, et, nt: (bt[i], 0, 0))
    grid_spec = pltpu.PrefetchScalarGridSpec(
        num_scalar_prefetch=3,
        grid=(n_rows // tile, nf),
        in_specs=[rows,
                  pl.BlockSpec((1, d, tf), lambda i, f, bt, et, nt: (et[i], 0, col(i, f, nt))),
                  pl.BlockSpec((1, d, tf), lambda i, f, bt, et, nt: (et[i], 0, col(i, f, nt))),
                  pl.BlockSpec((1, tf, d), lambda i, f, bt, et, nt: (et[i], col(i, f, nt), 0))],
        out_specs=rows,
        scratch_shapes=[pltpu.VMEM((tile, d), BF16), pltpu.VMEM((tile, d), F32)],
    )
    return pl.pallas_call(
        _moe_ffn_kernel,
        grid_spec=grid_spec,
        out_shape=jax.ShapeDtypeStruct(xs.shape, F32),
        compiler_params=_params(("arbitrary", "arbitrary")),
        name="moe_swiglu",
    )(blk_tab, exp_tab, n_tiles, xs, wg, wu, wd)


def _combine_kernel(pos_ref, meta_ref, x_ref, gate_ref, ys_ref, o_ref, buf, sem, *, tile):
    i = pl.program_id(0)
    slot = i % 2

    def gather(step, into):
        base = step * tile * 2

        def issue(t, carry):
            _row_copy(ys_ref.at[pos_ref[base + 2 * t]], buf.at[into, 0, t], sem.at[into]).start(priority=0)
            _row_copy(ys_ref.at[pos_ref[base + 2 * t + 1]], buf.at[into, 1, t], sem.at[into]).start(priority=1)
            return carry

        lax.fori_loop(0, tile, issue, 0)

    @pl.when(i == 0)
    def _():
        gather(0, 0)

    @pl.when(i + 1 < pl.num_programs(0))
    def _():
        gather(i + 1, 1 - slot)

    for k in range(2):
        _row_copy(ys_ref.at[pl.ds(0, tile)], buf.at[slot, k], sem.at[slot]).wait()
    meta = meta_ref[...]
    w1, w2 = meta[:, _W1:_W1 + 1], meta[:, _W2:_W2 + 1]
    gate = gate_ref[0]
    for j in range(buf.shape[3]):
        sl = slice(j * LANE, (j + 1) * LANE)
        o_ref[:, sl] = x_ref[:, sl] + gate[:, sl] * (w1 * buf[slot, 0, :, j, :] + w2 * buf[slot, 1, :, j, :])


def _combine(pos, meta, x, gate, ys, rows_per_seq, tile):
    t, d = x.shape
    chunks = d // LANE
    row = lambda n: pl.BlockSpec((tile, n), lambda i, p: (i, 0))
    grid_spec = pltpu.PrefetchScalarGridSpec(
        num_scalar_prefetch=1,
        grid=(t // tile,),
        in_specs=[row(LANE), row(d), _mod_specs(t, tile, rows_per_seq, d), pl.BlockSpec(memory_space=pl.ANY)],
        out_specs=row(d),
        scratch_shapes=[pltpu.VMEM((2, 2, tile, chunks, LANE), F32), pltpu.SemaphoreType.DMA((2,))],
    )
    return pl.pallas_call(
        functools.partial(_combine_kernel, tile=tile),
        grid_spec=grid_spec,
        out_shape=jax.ShapeDtypeStruct((t, d), F32),
        compiler_params=_params(("arbitrary",), disable_bounds_checks=True),
        name="moe_combine",
    )(pos, meta, x, _mod_arr(gate, rows_per_seq, tile), ys)


def _moe(x, g, shift, scale, gate, w_router, wg, wu, wd, layer, rows_per_seq):
    t, d = x.shape
    tile = min(FFN_ROW_TILE, t)
    meta, counts, h3 = _router(x, g, shift, scale, w_router, rows_per_seq)
    n_max = 2 * t // tile + N_EXPERTS + 1
    cnt = counts[0, :N_EXPERTS].astype(jnp.int32)
    tiles = (cnt + tile - 1) // tile
    ends = jnp.cumsum(tiles)
    n_tiles = ends[-1]
    blk_tab = jnp.minimum(jnp.arange(n_max, dtype=jnp.int32), n_tiles - 1)
    exp_tab = jnp.sum(blk_tab[:, None] >= ends[None, :], axis=1).astype(jnp.int32) + layer * N_EXPERTS
    offs = (ends - tiles) * tile
    experts = meta[:, _E1:_E2 + 1].astype(jnp.int32)
    ranks = meta[:, _R1:_R2 + 1].astype(jnp.int32)
    pos = (offs[experts] + ranks).reshape(-1)
    xs = _dispatch(pos, offs + cnt, h3, n_max * tile, tile)
    ys = _moe_ffn(blk_tab, exp_tab, n_tiles.reshape(1).astype(jnp.int32), xs, wg, wu, wd, tile)
    return _combine(pos, meta, x, gate, ys, rows_per_seq, tile)


def _softplus(x):
    return jnp.maximum(x, 0.0) + jnp.log1p(jnp.exp(-jnp.abs(x)))


def _ssm_in_kernel(x_ref, g_ref, sh_ref, sc_ref, wz_ref, wx_ref, wdt_ref, cw_ref, cb_ref, dtb_ref,
                   z_ref, xs_ref, b_ref, c_ref, dt_ref, conv_ref, ext_sc, *, tile, tiles_per_seq, d_inner, n_heads):
    i = pl.program_id(0)

    @pl.when(i % tiles_per_seq == 0)
    def _():
        ext_sc[0:SUBLANE, :] = jnp.zeros((SUBLANE, ext_sc.shape[1]), F32)

    h = _modulate(x_ref[...], g_ref[...], sh_ref[0], sc_ref[0]).astype(BF16)
    z_ref[...] = _dot(h, wz_ref[...])
    u = _dot(h, wx_ref[...])
    ext_sc[SUBLANE:SUBLANE + tile, :] = u
    cw = cw_ref[...]
    y = cw[0:1] * ext_sc[SUBLANE - 3:SUBLANE - 3 + tile, :]
    y = y + cw[1:2] * ext_sc[SUBLANE - 2:SUBLANE - 2 + tile, :]
    y = y + cw[2:3] * ext_sc[SUBLANE - 1:SUBLANE - 1 + tile, :]
    y = y + cw[3:4] * u + cb_ref[...]
    tail = ext_sc[tile:tile + SUBLANE, :]
    conv_ref[0] = tail
    ext_sc[0:SUBLANE, :] = tail
    xbc = _silu(y)
    gn = (xbc.shape[1] - d_inner) // 2
    xs_ref[...] = xbc[:, :d_inner]
    b_ref[...] = xbc[:, d_inner:d_inner + gn].astype(BF16)
    c_ref[...] = xbc[:, d_inner + gn:].astype(BF16)
    dt = _softplus(_dot(h, wdt_ref[...]) + dtb_ref[...])
    lane = lax.broadcasted_iota(jnp.int32, dt.shape, 1)
    dt_ref[...] = jnp.where(lane < n_heads, dt, 0.0)


def _ssm_in(x, g, shift, scale, w, rows_per_seq):
    t, d = x.shape
    tile = min(ROW_TILE, rows_per_seq)
    d_inner, conv_dim = w["wz"].shape[1], w["wx"].shape[1]
    gn = (conv_dim - d_inner) // 2
    n_seq = t // rows_per_seq
    tiles_per_seq = rows_per_seq // tile
    full = lambda a: pl.BlockSpec(a.shape, lambda i: (0,) * a.ndim)
    row = lambda n: pl.BlockSpec((tile, n), lambda i: (i, 0))
    mod = _mod_specs(t, tile, rows_per_seq, d)
    ws = [w["wz"], w["wx"], w["wdt"], w["conv_w"], w["conv_b"], w["dt_bias"]]
    return pl.pallas_call(
        functools.partial(_ssm_in_kernel, tile=tile, tiles_per_seq=tiles_per_seq, d_inner=d_inner,
                          n_heads=d_inner // SSM_HEAD_DIM),
        grid=(t // tile,),
        in_specs=[row(d), full(g), mod, mod] + [full(a) for a in ws],
        out_specs=[row(d_inner), row(d_inner), row(gn), row(gn), row(LANE),
                   pl.BlockSpec((1, SUBLANE, conv_dim), lambda i: (i // tiles_per_seq, 0, 0))],
        out_shape=[jax.ShapeDtypeStruct((t, d_inner), F32), jax.ShapeDtypeStruct((t, d_inner), F32),
                   jax.ShapeDtypeStruct((t, gn), BF16), jax.ShapeDtypeStruct((t, gn), BF16),
                   jax.ShapeDtypeStruct((t, LANE), F32), jax.ShapeDtypeStruct((n_seq, SUBLANE, conv_dim), F32)],
        scratch_shapes=[pltpu.VMEM((tile + SUBLANE, conv_dim), F32)],
        compiler_params=_params(("arbitrary",)),
        name="ssm_in_conv",
    )(x, g, _mod_arr(shift, rows_per_seq, tile), _mod_arr(scale, rows_per_seq, tile), *ws)


def _ssd_kernel(xs_ref, b_ref, c_ref, dt_ref, z_ref, alog_ref, e_ref, dskip_ref, ng_ref, y_ref, hout_ref,
                st_sc, y_sc, *, cl, n_groups, pairs_per_group):
    c = pl.program_id(1)

    @pl.when(c == 0)
    def _():
        st_sc[...] = jnp.zeros(st_sc.shape, F32)

    dt = dt_ref[0]
    d_a = dt * (-jnp.exp(alog_ref[...]))
    rowi = lax.broadcasted_iota(jnp.int32, (cl, cl), 0)
    coli = lax.broadcasted_iota(jnp.int32, (cl, cl), 1)
    causal = rowi >= coli
    ltri = jnp.where(causal, 1.0, 0.0).astype(BF16)
    cum = sum(_dot(ltri, part) for part in _split3(d_a))
    cum_t = cum.T
    expand = e_ref[...]
    dt_x = sum(_dot(part, expand) for part in _split3(dt))
    cum_x = sum(_dot(part, expand) for part in _split3(cum))
    xs = xs_ref[0]
    xdt = xs * dt_x
    xdt_b = xdt.astype(BF16)
    w_out = (xdt * jnp.exp(cum_x[cl - 1:cl, :] - cum_x)).astype(BF16)
    grow = jnp.exp(cum_x)
    lane = lax.broadcasted_iota(jnp.int32, (cl, LANE), 1)
    first = lax.broadcasted_iota(jnp.int32, (LANE, 1), 0) < SSM_HEAD_DIM
    for g in range(n_groups):
        bg = b_ref[0, :, g * D_STATE:(g + 1) * D_STATE]
        cg = c_ref[0, :, g * D_STATE:(g + 1) * D_STATE]
        cb = _dot_nt(cg, bg)
        for j in range(pairs_per_group):
            pr = g * pairs_per_group + j
            h0, h1 = 2 * pr, 2 * pr + 1
            sl = slice(pr * LANE, (pr + 1) * LANE)
            seg0 = jnp.exp(jnp.where(causal, cum[:, h0:h0 + 1] - cum_t[h0:h0 + 1, :], -jnp.inf))
            seg1 = jnp.exp(jnp.where(causal, cum[:, h1:h1 + 1] - cum_t[h1:h1 + 1, :], -jnp.inf))
            xp = xdt_b[:, sl]
            y_diag = jnp.where(lane < SSM_HEAD_DIM, _dot((cb * seg0).astype(BF16), xp),
                               _dot((cb * seg1).astype(BF16), xp))
            st = st_sc[pr]
            y_sc[:, sl] = y_diag + _dot_nt(cg, st.astype(BF16)) * grow[:, sl]
            decay = jnp.exp(jnp.where(first, cum_t[h0:h0 + 1, cl - 1:cl], cum_t[h1:h1 + 1, cl - 1:cl]))
            st_sc[pr] = st * decay + _dot_tn(w_out[:, sl], bg)
    zz = z_ref[0]
    y = (y_sc[...] + dskip_ref[...] * xs) * _silu(zz)
    y_ref[0] = _rms(y, ng_ref[...]).astype(y_ref.dtype)

    @pl.when(c == pl.num_programs(1) - 1)
    def _():
        hout_ref[0] = st_sc[...]


def _ssd(xs, bm, cm, dt, z, w, n_seq, seq_len):
    d_inner = xs.shape[1]
    gn = bm.shape[1]
    n_pairs = d_inner // LANE
    cl = min(SSD_CHUNK, seq_len)
    nc = seq_len // cl
    r3 = lambda a: a.reshape(n_seq, seq_len, a.shape[1])
    blk = lambda n: pl.BlockSpec((1, cl, n), lambda b, c: (b, c, 0))
    full = lambda a: pl.BlockSpec(a.shape, lambda b, c: (0,) * a.ndim)
    consts = [w["a_log"], w["expand"], w["d_skip"], w["norm_g"]]
    y, h_out = pl.pallas_call(
        functools.partial(_ssd_kernel, cl=cl, n_groups=SSM_GROUPS, pairs_per_group=n_pairs // SSM_GROUPS),
        grid=(n_seq, nc),
        in_specs=[blk(d_inner), blk(gn), blk(gn), blk(LANE), blk(d_inner)] + [full(a) for a in consts],
        out_specs=[blk(d_inner), pl.BlockSpec((1, n_pairs, LANE, D_STATE), lambda b, c: (b, 0, 0, 0))],
        out_shape=[jax.ShapeDtypeStruct((n_seq, seq_len, d_inner), BF16),
                   jax.ShapeDtypeStruct((n_seq, n_pairs, LANE, D_STATE), F32)],
        scratch_shapes=[pltpu.VMEM((n_pairs, LANE, D_STATE), F32), pltpu.VMEM((cl, d_inner), F32)],
        compiler_params=_params(("arbitrary", "arbitrary")),
        name="ssd_scan",
    )(r3(xs), r3(bm), r3(cm), r3(dt), r3(z), *consts)
    return y.reshape(n_seq * seq_len, d_inner), h_out


def _prep_ssm(w_in, conv_w, conv_b, dt_bias, a_log, d_skip, norm_g, w_out):
    d = w_in.shape[0]
    d_inner = w_out.shape[0]
    n_heads = d_inner // SSM_HEAD_DIM
    conv_dim = conv_w.shape[1]
    pad = LANE - n_heads
    w_dt = w_in[:, d_inner + conv_dim:]
    rep = lambda v: jnp.repeat(v, SSM_HEAD_DIM, axis=-1)
    head_of = jnp.arange(d_inner) // SSM_HEAD_DIM
    return dict(
        wz=w_in[:, :d_inner].astype(BF16),
        wx=w_in[:, d_inner:d_inner + conv_dim].astype(BF16),
        wdt=jnp.concatenate([w_dt, jnp.zeros((d, pad), F32)], axis=1).astype(BF16),
        wdt_x=rep(w_dt).astype(BF16),
        conv_w=conv_w, conv_b=conv_b.reshape(1, conv_dim),
        dt_bias=jnp.concatenate([dt_bias, jnp.zeros((pad,), F32)]).reshape(1, LANE),
        dt_bias_x=rep(dt_bias).reshape(1, d_inner),
        a_log=jnp.concatenate([a_log, jnp.zeros((pad,), F32)]).reshape(1, LANE),
        a_log_x=rep(a_log).reshape(1, d_inner),
        expand=(jnp.arange(LANE)[:, None] == head_of[None, :]).astype(BF16),
        d_skip=rep(d_skip).reshape(1, d_inner),
        norm_g=norm_g.reshape(1, d_inner),
        w_out=w_out.astype(BF16),
    )


def _ssm_step_in_kernel(x_ref, g_ref, sh_ref, sc_ref, wz_ref, wx_ref, wdt_ref, cw_ref, cb_ref, dtb_ref, alog_ref,
                        buf_ref, z_ref, xs_ref, b_ref, c_ref, conv_ref, xdt_t_ref, dec_t_ref, *, d_inner):
    h = _modulate(x_ref[...], g_ref[...], sh_ref[0], sc_ref[0]).astype(BF16)
    z_ref[...] = _dot(h, wz_ref[...])
    u = _dot(h, wx_ref[...])
    cw = cw_ref[...]
    y = cw[0:1] * buf_ref[0] + cw[1:2] * buf_ref[1] + cw[2:3] * buf_ref[2] + cw[3:4] * u + cb_ref[...]
    conv_ref[0] = buf_ref[1]
    conv_ref[1] = buf_ref[2]
    conv_ref[2] = u
    xbc = _silu(y)
    gn = (xbc.shape[1] - d_inner) // 2
    xs = xbc[:, :d_inner]
    xs_ref[...] = xs
    b_ref[...] = xbc[:, d_inner:d_inner + gn]
    c_ref[...] = xbc[:, d_inner + gn:]
    dt = _softplus(_dot(h, wdt_ref[...]) + dtb_ref[...])
    xdt_t_ref[...] = (xs * dt).T
    dec_t_ref[...] = jnp.exp(dt * (-jnp.exp(alog_ref[...]))).T


def _ssm_step_in(x, g, shift, scale, buf_t, w):
    n, d = x.shape
    d_inner, conv_dim = w["wz"].shape[1], w["wx"].shape[1]
    gn = (conv_dim - d_inner) // 2
    ws = [w["wz"], w["wx"], w["wdt_x"], w["conv_w"], w["conv_b"], w["dt_bias_x"], w["a_log_x"], buf_t]
    sds = jax.ShapeDtypeStruct
    return pl.pallas_call(
        functools.partial(_ssm_step_in_kernel, d_inner=d_inner),
        out_shape=[sds((n, d_inner), F32), sds((n, d_inner), F32), sds((n, gn), F32), sds((n, gn), F32),
                   sds((CONV_W - 1, n, conv_dim), F32), sds((d_inner, n), F32), sds((d_inner, n), F32)],
        compiler_params=pltpu.CompilerParams(vmem_limit_bytes=VMEM_LIMIT),
        name="ssm_step_in",
    )(x, g, shift.reshape(1, n, d), scale.reshape(1, n, d), *ws)


def _ssm_step_kernel(st_ref, xdt_t_ref, dec_t_ref, b_ref, c_ref, *rest, n_groups):
    hout_ref, y_t_ref = rest[-2:]
    b = pl.program_id(0)

    @pl.when(b == 0)
    def _():
        y_t_ref[...] = jnp.zeros(y_t_ref.shape, F32)

    rg = st_ref.shape[2] // n_groups
    mine = lax.broadcasted_iota(jnp.int32, (rg, LANE), 1) == b
    for g in range(n_groups):
        rs = slice(g * rg, (g + 1) * rg)
        xcol = jnp.sum(jnp.where(mine, xdt_t_ref[rs, :], 0.0), axis=1, keepdims=True)
        dcol = jnp.sum(jnp.where(mine, dec_t_ref[rs, :], 0.0), axis=1, keepdims=True)
        bg = b_ref[0, :, g * D_STATE:(g + 1) * D_STATE]
        cg = c_ref[0, :, g * D_STATE:(g + 1) * D_STATE]
        s_new = st_ref[0, 0, rs, :] * dcol + xcol * bg
        hout_ref[0, 0, rs, :] = s_new
        ycol = jnp.sum(s_new * cg, axis=1, keepdims=True)
        y_t_ref[rs, :] = jnp.where(mine, ycol, y_t_ref[rs, :])


def _ssm_step(layer, state, xdt_t, dec_t, bm, cm, new_state=None):
    _, n, rows, ns = state.shape
    full = lambda a: pl.BlockSpec(a.shape, lambda b: (0,) * a.ndim)
    seq_row = pl.BlockSpec((1, 1, bm.shape[1]), lambda b: (b, 0, 0))
    slab = pl.BlockSpec((1, 1, rows, ns), lambda b: (layer, b, 0, 0))
    args = [state, xdt_t, dec_t, bm.reshape(n, 1, -1), cm.reshape(n, 1, -1)]
    in_specs = [slab, full(xdt_t), full(dec_t), seq_row, seq_row]
    aliases = {}
    if new_state is not None:
        args.append(new_state)
        in_specs.append(pl.BlockSpec(memory_space=pl.ANY))
        aliases = {len(args) - 1: 0}
    return pl.pallas_call(
        functools.partial(_ssm_step_kernel, n_groups=SSM_GROUPS),
        grid=(n,),
        in_specs=in_specs,
        out_specs=[slab, pl.BlockSpec((rows, n), lambda b: (0, 0))],
        out_shape=[jax.ShapeDtypeStruct(state.shape, F32), jax.ShapeDtypeStruct((rows, n), F32)],
        input_output_aliases=aliases,
        compiler_params=_params(("arbitrary",)),
        name="ssm_step",
    )(*args)


def _ssm_step_out_kernel(y_t_ref, xs_ref, z_ref, dskip_ref, ng_ref, w_ref, x_ref, gate_ref, o_ref):
    y = (y_t_ref[...].T + dskip_ref[...] * xs_ref[...]) * _silu(z_ref[...])
    y = _rms(y, ng_ref[...]).astype(BF16)
    o_ref[...] = x_ref[...] + gate_ref[0] * _dot(y, w_ref[...])


def _ssm_step_out(y_t, xs, z, w, x, gate):
    n, d = x.shape
    return pl.pallas_call(
        _ssm_step_out_kernel,
        out_shape=jax.ShapeDtypeStruct((n, d), F32),
        compiler_params=pltpu.CompilerParams(vmem_limit_bytes=VMEM_LIMIT),
        name="ssm_step_out",
    )(y_t, xs, z, w["d_skip"], w["norm_g"], w["w_out"], x, gate.reshape(1, n, d))


def _final_norm_kernel(x_ref, g_ref, o_ref):
    o_ref[...] = _rms(x_ref[...], g_ref[...])


def _final_norm(x, g):
    t, d = x.shape
    tile = min(FFN_ROW_TILE, t)
    return pl.pallas_call(
        _final_norm_kernel,
        grid=(t // tile,),
        in_specs=[pl.BlockSpec((tile, d), lambda i: (i, 0)), pl.BlockSpec((1, d), lambda i: (0, 0))],
        out_specs=pl.BlockSpec((tile, d), lambda i: (i, 0)),
        out_shape=jax.ShapeDtypeStruct((t, d), F32),
        compiler_params=_params(("arbitrary",)),
        name="final_norm",
    )(x, g.reshape(1, d))


def kernel(x_prompt, x_sample, c_prompt, c_sample, cache_kv_latent, cache_k_rope, page_table, state_ssm, state_conv, w_ada, b_ada, norm_mix_g, norm_ffn_g, norm_final_g, mla_w_in, mla_q_norm_g, mla_kv_norm_g, mla_w_qb, mla_w_kvb, mla_w_o, ssm_w_in, ssm_conv_w, ssm_conv_b, ssm_dt_bias, ssm_a_log, ssm_d, ssm_norm_g, ssm_w_out, ffn_w_gate, ffn_w_up, ffn_w_down, moe_w_router, moe_w_gate, moe_w_up, moe_w_down):
    nb, seq, d = x_prompt.shape
    ns = x_sample.shape[0]
    depth = w_ada.shape[0]
    past_len = page_table.shape[1] * cache_kv_latent.shape[2]
    attn_scale = (QK_NOPE + QK_ROPE) ** -0.5

    xp = x_prompt.reshape(nb * seq, d)
    xs = x_sample.reshape(ns, d)
    mods = _ada(jnp.concatenate([c_prompt, c_sample], axis=0), w_ada, b_ada)
    mods = mods.reshape(depth, nb + ns, 6, d)
    tabs_p = _rope_tables(jnp.arange(seq), attn_scale * LOG2E)
    tabs_s = _rope_tables(jnp.full((ns,), past_len), attn_scale)
    state = state_ssm.reshape(state_ssm.shape[0], ns, -1, D_STATE)
    cache_rope_t = cache_k_rope.transpose(0, 1, 3, 2)
    moe_w = [w.astype(BF16).reshape((-1,) + w.shape[2:]) for w in (moe_w_gate, moe_w_up, moe_w_down)]

    lat_p, rope_p, lat_s, rope_s = [], [], [], []
    ssm_p, conv_p, ssm_s, conv_s = [], [], None, []
    for i in range(depth):
        j = i // 2
        mp = [mods[i, :nb, k] for k in range(6)]
        ms = [mods[i, nb:, k] for k in range(6)]
        g_mix = norm_mix_g[i].reshape(1, d)
        g_ffn = norm_ffn_g[i].reshape(1, d)
        if i % 2 == 0:
            w = _prep_mla(mla_w_in[j], mla_w_qb[j], mla_w_kvb[j])
            qg = mla_q_norm_g[j].reshape(1, -1)
            kvg = mla_kv_norm_g[j].reshape(1, -1)
            w_o = mla_w_o[j].astype(BF16)
            ckv, kr, q, k, vt = _mla_proj(xp, g_mix, mp[0], mp[1], tabs_p, qg, kvg, w, seq, True)
            o_t = _flash(q, k, vt.reshape(MLA_HEADS, V_DIM, nb * seq), nb, seq)
            xp = _proj_res(o_t.reshape(MLA_HEADS * V_DIM, nb * seq), w_o, xp, mp[2], seq, transposed=True)
            lat_p.append(ckv.reshape(nb, seq, -1))
            rope_p.append(kr.reshape(nb, seq, -1))

            ckv, kr, q = _mla_proj(xs, g_mix, ms[0], ms[1], tabs_s, qg, kvg, w, 1, False)
            kv_rank = ckv.shape[1]
            qlat = _headmat(q, w["wabs"], LANE, kv_rank)
            o_lat = _decode_attn(j, page_table, qlat.reshape(ns, MLA_HEADS, kv_rank), q.reshape(ns, MLA_HEADS, LANE),
                                 ckv.reshape(ns, 1, kv_rank), kr.reshape(ns, 1, QK_ROPE), cache_kv_latent,
                                 cache_rope_t)
            o = _headmat(o_lat.reshape(ns, MLA_HEADS * kv_rank), w["wvbd"], 2 * kv_rank, 2 * V_DIM)
            xs = _proj_res(o, w_o, xs, ms[2], 1)
            lat_s.append(ckv.reshape(ns, 1, -1))
            rope_s.append(kr.reshape(ns, 1, -1))
        else:
            w = _prep_ssm(ssm_w_in[j], ssm_conv_w[j], ssm_conv_b[j], ssm_dt_bias[j], ssm_a_log[j], ssm_d[j],
                          ssm_norm_g[j], ssm_w_out[j])
            z, xin, bm, cm, dt, conv = _ssm_in(xp, g_mix, mp[0], mp[1], w, seq)
            y, h_new = _ssd(xin, bm, cm, dt, z, w, nb, seq)
            xp = _proj_res(y, w["w_out"], xp, mp[2], seq)
            ssm_p.append(h_new.reshape(nb, -1, SSM_HEAD_DIM, D_STATE))
            conv_p.append(conv[:, SUBLANE - (CONV_W - 1):])

            buf_t = state_conv[j].transpose(1, 0, 2)
            z, xin, bm, cm, conv, xdt_t, dec_t = _ssm_step_in(xs, g_mix, ms[0], ms[1], buf_t, w)
            ssm_s, y_t = _ssm_step(j, state, xdt_t, dec_t, bm, cm, ssm_s)
            xs = _ssm_step_out(y_t, xin, z, w, xs, ms[2])
            conv_s.append(conv.transpose(1, 0, 2))
        if i % 2 == 0:
            wg = ffn_w_gate[j].astype(BF16)
            wu = ffn_w_up[j].astype(BF16)
            wd = ffn_w_down[j].astype(BF16)
            xp = _ffn(xp, g_ffn, mp[3], mp[4], mp[5], wg, wu, wd, seq)
            xs = _ffn(xs, g_ffn, ms[3], ms[4], ms[5], wg, wu, wd, 1)
        else:
            xp = _moe(xp, g_ffn, mp[3], mp[4], mp[5], moe_w_router[j], *moe_w, j, seq)
            xs = _moe(xs, g_ffn, ms[3], ms[4], ms[5], moe_w_router[j], *moe_w, j, 1)
    y_prompt = _final_norm(xp, norm_final_g).reshape(nb, seq, d)
    y_sample = _final_norm(xs, norm_final_g).reshape(ns, 1, d)
    return (y_prompt, y_sample, jnp.stack(lat_p), jnp.stack(rope_p), jnp.stack(lat_s), jnp.stack(rope_s),
            jnp.stack(ssm_p), jnp.stack(conv_p), ssm_s.reshape(state_ssm.shape), jnp.stack(conv_s))
```

```python
import functools
import math

import jax
import jax.numpy as jnp
from jax import lax
from jax.experimental import pallas as pl
from jax.experimental.pallas import tpu as pltpu

F32 = jnp.float32
BF16 = jnp.bfloat16

RMS_EPS = 1e-6
MLA_HEADS = 16
QK_NOPE = 64
QK_ROPE = 32
V_DIM = 64
ROPE_THETA = 10000.0
PAGE_SIZE = 128
SSM_HEAD_DIM = 64
SSM_GROUPS = 4
D_STATE = 128
CONV_W = 4
SSD_CHUNK = 128
N_EXPERTS = 8

LANE = 128
SUBLANE = 8
VMEM_LIMIT = 56 << 20

ROW_TILE = 256
FFN_ROW_TILE = 512
DENSE_FFN_ROW_TILE = 1024
FFN_COL_TILE = 1408
ATTN_TILE = 512
NEG = -1e30
LOG2E = math.log2(math.e)


def _params(sem, **kw):
    return pltpu.CompilerParams(dimension_semantics=sem, vmem_limit_bytes=VMEM_LIMIT, **kw)


def _dot(a, b):
    return jnp.dot(a, b, preferred_element_type=F32)


def _dot_nt(a, b):
    return lax.dot_general(a, b, (((1,), (1,)), ((), ())), preferred_element_type=F32)


def _dot_tn(a, b):
    return lax.dot_general(a, b, (((0,), (0,)), ((), ())), preferred_element_type=F32)


def _split3(v):
    hi = v.astype(BF16)
    r = v - hi.astype(F32)
    mid = r.astype(BF16)
    lo = (r - mid.astype(F32)).astype(BF16)
    return hi, mid, lo


def _silu(x):
    return x * jax.nn.sigmoid(x)


def _rms(x, g):
    return (x * lax.rsqrt(jnp.mean(x * x, axis=-1, keepdims=True) + RMS_EPS)) * g


def _modulate(x, g, shift, scale):
    return _rms(x, g) * (1.0 + scale) + shift


def _mod_specs(n_rows, tile, rows_per_seq, d):
    if rows_per_seq == 1:
        return pl.BlockSpec((1, tile, d), lambda i, *_: (i, 0, 0))
    tiles_per_seq = rows_per_seq // tile
    return pl.BlockSpec((1, 1, d), lambda i, *_: (i // tiles_per_seq, 0, 0))


def _mod_arr(m, rows_per_seq, tile):
    if rows_per_seq == 1:
        return m.reshape(m.shape[0] // tile, tile, m.shape[1])
    return m.reshape(m.shape[0], 1, m.shape[1])


def _ada_kernel(c_ref, w_ref, b_ref, o_ref):
    s = _silu(c_ref[...]).astype(BF16)
    o_ref[0] = _dot(s, w_ref[0].astype(BF16)) + b_ref[0]


def _ada(c_all, w_ada, b_ada):
    depth, d, d6 = w_ada.shape
    n = c_all.shape[0]
    return pl.pallas_call(
        _ada_kernel,
        grid=(depth, d6 // d),
        in_specs=[pl.BlockSpec((n, d), lambda i, k: (0, 0)),
                  pl.BlockSpec((1, d, d), lambda i, k: (i, 0, k)),
                  pl.BlockSpec((1, 1, d), lambda i, k: (i, 0, k))],
        out_specs=pl.BlockSpec((1, n, d), lambda i, k: (i, 0, k)),
        out_shape=jax.ShapeDtypeStruct((depth, n, d6), F32),
        compiler_params=_params(("arbitrary", "arbitrary")),
        name="adaln",
    )(c_all, w_ada, b_ada.reshape(depth, 1, d6))


def _mla_proj_kernel(x_ref, g_ref, sh_ref, sc_ref, ct_ref, st_ref, cts_ref, sts_ref, wcq_ref, wckv_ref,
                     wkra_ref, wkrb_ref, qg_ref, kvg_ref, wqa_ref, wqb_ref, *rest, heads, with_kv):
    if with_kv:
        wka_ref, wvt_ref, ckv_ref, kr_ref, q_ref, k_ref, vt_ref = rest
    else:
        ckv_ref, kr_ref, q_ref = rest
    h = _modulate(x_ref[...], g_ref[...], sh_ref[0], sc_ref[0]).astype(BF16)
    cq = _rms(_dot(h, wcq_ref[...]), qg_ref[...]).astype(BF16)
    ckv = _rms(_dot(h, wckv_ref[...]), kvg_ref[...])
    kr = _dot(h, wkra_ref[...]) * ct_ref[...] + _dot(h, wkrb_ref[...]) * st_ref[...]
    ckv_ref[...] = ckv
    kr_ref[...] = kr[:, :QK_ROPE]
    rep = lambda t: jnp.concatenate([t] * heads, axis=1)
    q = (_dot(cq, wqa_ref[...]) * rep(cts_ref[...]) + _dot(cq, wqb_ref[...]) * rep(sts_ref[...])).astype(BF16)
    if not with_kv:
        q_ref[...] = q
        return
    ckvb = ckv.astype(BF16)
    k = (_dot(ckvb, wka_ref[...]) + rep(kr)).astype(BF16)
    for hd in range(heads):
        q_ref[hd] = q[:, hd * LANE:(hd + 1) * LANE]
        k_ref[hd] = k[:, hd * LANE:(hd + 1) * LANE]
    vt_ref[...] = _dot_nt(wvt_ref[...], ckvb).astype(BF16)


def _rot_cols(w):
    half = w.shape[-1] // 2
    return jnp.concatenate([-w[..., half:], w[..., :half]], axis=-1)


def _prep_mla(w_in, w_qb, w_kvb):
    d = w_in.shape[0]
    q_rank = w_qb.shape[0]
    kv_rank = w_kvb.shape[0]
    hd = LANE
    pad = hd - QK_ROPE - QK_NOPE
    w_kr = w_in[:, q_rank + kv_rank:]
    zk = jnp.zeros((d, hd - QK_ROPE), F32)
    wq = w_qb.reshape(q_rank, MLA_HEADS, QK_NOPE + QK_ROPE)
    q_nope, q_pe = wq[..., :QK_NOPE], wq[..., QK_NOPE:]
    zq = jnp.zeros((q_rank, MLA_HEADS, pad), F32)
    wkv = w_kvb.reshape(kv_rank, MLA_HEADS, QK_NOPE + V_DIM)
    k_nope, v = wkv[..., :QK_NOPE], wkv[..., QK_NOPE:]
    return dict(
        wcq=w_in[:, :q_rank].astype(BF16),
        wckv=w_in[:, q_rank:q_rank + kv_rank].astype(BF16),
        wkra=jnp.concatenate([w_kr, zk], axis=1).astype(BF16),
        wkrb=jnp.concatenate([_rot_cols(w_kr), zk], axis=1).astype(BF16),
        wqa=jnp.concatenate([q_pe, q_nope, zq], axis=-1).reshape(q_rank, MLA_HEADS * hd).astype(BF16),
        wqb=jnp.concatenate([_rot_cols(q_pe), jnp.zeros((q_rank, MLA_HEADS, hd - QK_ROPE), F32)],
                            axis=-1).reshape(q_rank, MLA_HEADS * hd).astype(BF16),
        wka=jnp.concatenate([jnp.zeros((kv_rank, MLA_HEADS, QK_ROPE), F32), k_nope,
                             jnp.zeros((kv_rank, MLA_HEADS, pad), F32)],
                            axis=-1).reshape(kv_rank, MLA_HEADS * hd).astype(BF16),
        wvt=v.reshape(kv_rank, MLA_HEADS * V_DIM).T.astype(BF16),
        wabs=jnp.concatenate([jnp.zeros((MLA_HEADS, QK_ROPE, kv_rank), F32), k_nope.transpose(1, 2, 0),
                              jnp.zeros((MLA_HEADS, pad, kv_rank), F32)], axis=1).astype(BF16),
        wvbd=_block_diag_pairs(v.transpose(1, 0, 2)).astype(BF16),
    )


def _block_diag_pairs(v):
    h, r, dv = v.shape
    v = v.reshape(h // 2, 2, r, dv)
    z = jnp.zeros((h // 2, r, dv), v.dtype)
    top = jnp.concatenate([v[:, 0], z], axis=-1)
    bot = jnp.concatenate([z, v[:, 1]], axis=-1)
    return jnp.concatenate([top, bot], axis=1)


def _rope_tables(pos, scale):
    half = QK_ROPE // 2
    inv = ROPE_THETA ** (-jnp.arange(half, dtype=F32) / half)
    ang = pos.astype(F32)[:, None] * inv[None, :]
    cos, sin = jnp.cos(ang), jnp.sin(ang)
    n = pos.shape[0]
    ct = jnp.concatenate([cos, cos, jnp.ones((n, LANE - QK_ROPE), F32)], axis=1)
    st = jnp.concatenate([sin, sin, jnp.zeros((n, LANE - QK_ROPE), F32)], axis=1)
    return ct, st, ct * scale, st * scale


def _mla_proj(x, g, shift, scale, tabs, qg, kvg, w, rows_per_seq, with_kv):
    t, d = x.shape
    tile = min(ROW_TILE, t)
    n_tab_tiles = tabs[0].shape[0] // tile
    heads = MLA_HEADS
    kv_rank = w["wckv"].shape[1]
    full = lambda a: pl.BlockSpec(a.shape, lambda i: (0,) * a.ndim)
    row = lambda n: pl.BlockSpec((tile, n), lambda i: (i, 0))
    tab = pl.BlockSpec((tile, LANE), lambda i: (i % n_tab_tiles, 0))
    mod = _mod_specs(t, tile, rows_per_seq, d)
    ws = [w["wcq"], w["wckv"], w["wkra"], w["wkrb"], qg, kvg, w["wqa"], w["wqb"]]
    outs = [jax.ShapeDtypeStruct((t, kv_rank), F32), jax.ShapeDtypeStruct((t, QK_ROPE), F32)]
    out_specs = [row(kv_rank), row(QK_ROPE)]
    if with_kv:
        ws += [w["wka"], w["wvt"]]
        head_major = pl.BlockSpec((heads, tile, LANE), lambda i: (0, i, 0))
        outs += [jax.ShapeDtypeStruct((heads, t, LANE), BF16), jax.ShapeDtypeStruct((heads, t, LANE), BF16),
                 jax.ShapeDtypeStruct((heads * V_DIM, t), BF16)]
        out_specs += [head_major, head_major, pl.BlockSpec((heads * V_DIM, tile), lambda i: (0, i))]
    else:
        outs.append(jax.ShapeDtypeStruct((t, heads * LANE), BF16))
        out_specs.append(row(heads * LANE))
    return pl.pallas_call(
        functools.partial(_mla_proj_kernel, heads=heads, with_kv=with_kv),
        grid=(t // tile,),
        in_specs=[row(d), full(g), mod, mod, tab, tab, tab, tab] + [full(a) for a in ws],
        out_specs=out_specs,
        out_shape=outs,
        compiler_params=_params(("arbitrary",)),
        name="mla_proj",
    )(x, g, _mod_arr(shift, rows_per_seq, tile), _mod_arr(scale, rows_per_seq, tile), *tabs, *ws)


def _flash_kernel(qi_tab, ki_tab, q_ref, k_ref, vt_ref, o_ref, m_sc, l_sc, acc_sc, s0_sc, s1_sc, *, tile, heads):
    t = pl.program_id(1)
    qi = qi_tab[t]
    ki = ki_tab[t]

    @pl.when(ki == 0)
    def _():
        m_sc[...] = jnp.full(m_sc.shape, NEG, F32)
        l_sc[...] = jnp.zeros(l_sc.shape, F32)
        acc_sc[...] = jnp.zeros(acc_sc.shape, F32)

    def scores(h, s_sc):
        s_sc[...] = _dot_nt(k_ref[h], q_ref[h])

    def sweep(diagonal):
        if diagonal:
            visible = (lax.broadcasted_iota(jnp.int32, (tile, tile), 0)
                       <= lax.broadcasted_iota(jnp.int32, (tile, tile), 1))

        def softmax_pv(h, s_sc):
            s = s_sc[...]
            if diagonal:
                s = jnp.where(visible, s, NEG)
            m_prev = m_sc[h]
            m_new = jnp.maximum(m_prev, jnp.max(s, axis=0, keepdims=True))
            alpha = jnp.exp2(m_prev - m_new)
            p = jnp.exp2(s - m_new)
            l_new = alpha * l_sc[h] + jnp.sum(p, axis=0, keepdims=True)
            acc = alpha * acc_sc[h] + _dot(vt_ref[h], p.astype(BF16))
            if diagonal:
                o_ref[h] = (acc / l_new).astype(o_ref.dtype)
            else:
                m_sc[h] = m_new
                l_sc[h] = l_new
                acc_sc[h] = acc

        scores(0, s0_sc)

        def pair(i, carry):
            h0 = 2 * i
            scores(h0 + 1, s1_sc)
            softmax_pv(h0, s0_sc)
            scores(jnp.minimum(h0 + 2, heads - 1), s0_sc)
            softmax_pv(h0 + 1, s1_sc)
            return carry

        lax.fori_loop(0, heads // 2, pair, 0)

    @pl.when(ki < qi)
    def _():
        sweep(False)

    @pl.when(ki == qi)
    def _():
        sweep(True)


def _flash(q, k, vt, n_seq, seq_len):
    heads = q.shape[0]
    tile = min(ATTN_TILE, seq_len)
    nq = seq_len // tile
    pairs = [(i, j) for i in range(nq) for j in range(i + 1)]
    qi_tab = jnp.array([p[0] for p in pairs], jnp.int32)
    ki_tab = jnp.array([p[1] for p in pairs], jnp.int32)
    grid_spec = pltpu.PrefetchScalarGridSpec(
        num_scalar_prefetch=2,
        grid=(n_seq, len(pairs)),
        in_specs=[pl.BlockSpec((heads, tile, LANE), lambda b, t, qt, kt: (0, b * nq + qt[t], 0)),
                  pl.BlockSpec((heads, tile, LANE), lambda b, t, qt, kt: (0, b * nq + kt[t], 0)),
                  pl.BlockSpec((heads, V_DIM, tile), lambda b, t, qt, kt: (0, 0, b * nq + kt[t]))],
        out_specs=pl.BlockSpec((heads, V_DIM, tile), lambda b, t, qt, kt: (0, 0, b * nq + qt[t])),
        scratch_shapes=[pltpu.VMEM((heads, 1, tile), F32), pltpu.VMEM((heads, 1, tile), F32),
                        pltpu.VMEM((heads, V_DIM, tile), F32), pltpu.VMEM((tile, tile), F32),
                        pltpu.VMEM((tile, tile), F32)],
    )
    return pl.pallas_call(
        functools.partial(_flash_kernel, tile=tile, heads=heads),
        grid_spec=grid_spec,
        out_shape=jax.ShapeDtypeStruct(vt.shape, BF16),
        compiler_params=_params(("arbitrary", "arbitrary")),
        name="flash_attn",
    )(qi_tab, ki_tab, q, k, vt)


def _headmat_kernel(a_ref, w_ref, o_ref):
    o_ref[...] = _dot(a_ref[...], w_ref[0]).astype(o_ref.dtype)


def _headmat(a, w, in_w, out_w):
    n, rows = w.shape[0], a.shape[0]
    return pl.pallas_call(
        _headmat_kernel,
        grid=(n,),
        in_specs=[pl.BlockSpec((rows, in_w), lambda i: (0, i)),
                  pl.BlockSpec((1, in_w, out_w), lambda i: (i, 0, 0))],
        out_specs=pl.BlockSpec((rows, out_w), lambda i: (0, i)),
        out_shape=jax.ShapeDtypeStruct((rows, n * out_w), BF16),
        compiler_params=_params(("arbitrary",)),
        name="head_matmul",
    )(a, w)


def _decode_kernel(pt_ref, qlat_ref, q_ref, ckv_ref, kr_ref, lat_hbm, rope_hbm, o_ref, latbuf, ropebuf, kcat, rcat,
                   sem, *, layer, pages):
    b = pl.program_id(0)
    slot = b % 2

    def fetch(seq, into):
        def issue(i, carry):
            pg = pt_ref[seq * pages + i]
            pltpu.make_async_copy(lat_hbm.at[layer, pg], latbuf.at[into, i], sem.at[into]).start(priority=0)
            pltpu.make_async_copy(rope_hbm.at[layer, pg], ropebuf.at[into, i], sem.at[into]).start(priority=1)
            return carry

        lax.fori_loop(0, pages, issue, 0)

    @pl.when(b == 0)
    def _():
        fetch(0, 0)

    @pl.when(b + 1 < pl.num_programs(0))
    def _():
        fetch(b + 1, 1 - slot)

    pltpu.make_async_copy(lat_hbm.at[layer, pl.ds(0, pages)], latbuf.at[slot], sem.at[slot]).wait()
    pltpu.make_async_copy(rope_hbm.at[layer, pl.ds(0, pages)], ropebuf.at[slot], sem.at[slot]).wait()
    for i in range(pages):
        kcat[i * PAGE_SIZE:(i + 1) * PAGE_SIZE, :] = latbuf[slot, i].astype(BF16)
        rcat[:, i * PAGE_SIZE:(i + 1) * PAGE_SIZE] = ropebuf[slot, i].astype(BF16)
    ql = qlat_ref[0]
    qp = q_ref[0][:, :QK_ROPE]
    keys = kcat[...]
    s = _dot_nt(ql, keys) + _dot(qp, rcat[...])
    kl = ckv_ref[0].astype(BF16).astype(F32)
    kp = kr_ref[0].astype(BF16).astype(F32)
    s1 = (jnp.sum(ql.astype(F32) * kl, axis=1, keepdims=True)
          + jnp.sum(qp.astype(F32) * kp, axis=1, keepdims=True))
    m = jnp.maximum(jnp.max(s, axis=1, keepdims=True), s1)
    p = jnp.exp(s - m)
    p1 = jnp.exp(s1 - m)
    l = jnp.sum(p, axis=1, keepdims=True) + p1
    acc = _dot(p.astype(BF16), keys) + p1.astype(BF16).astype(F32) * kl
    o_ref[0] = (acc / l).astype(o_ref.dtype)


def _decode_attn(layer, page_table, qlat, q, ckv, kr, cache_lat, cache_rope):
    b, heads, rank = qlat.shape
    pages = page_table.shape[1]
    seq = lambda n, w: pl.BlockSpec((1, n, w), lambda b_, pt: (b_, 0, 0))
    hbm = pl.BlockSpec(memory_space=pl.ANY)
    grid_spec = pltpu.PrefetchScalarGridSpec(
        num_scalar_prefetch=1,
        grid=(b,),
        in_specs=[seq(heads, rank), seq(heads, LANE), seq(1, rank), seq(1, QK_ROPE), hbm, hbm],
        out_specs=seq(heads, rank),
        scratch_shapes=[pltpu.VMEM((2, pages, PAGE_SIZE, rank), F32), pltpu.VMEM((2, pages, QK_ROPE, PAGE_SIZE), F32),
                        pltpu.VMEM((pages * PAGE_SIZE, rank), BF16), pltpu.VMEM((QK_ROPE, pages * PAGE_SIZE), BF16),
                        pltpu.SemaphoreType.DMA((2,))],
    )
    return pl.pallas_call(
        functools.partial(_decode_kernel, layer=layer, pages=pages),
        grid_spec=grid_spec,
        out_shape=jax.ShapeDtypeStruct((b, heads, rank), BF16),
        compiler_params=_params(("arbitrary",)),
        name="decode_attn",
    )(page_table.reshape(-1), qlat, q, ckv, kr, cache_lat, cache_rope)


def _proj_res_kernel(a_ref, w_ref, x_ref, gate_ref, o_ref, *, transposed):
    y = _dot_tn(a_ref[...], w_ref[...]) if transposed else _dot(a_ref[...], w_ref[...])
    o_ref[...] = x_ref[...] + gate_ref[0] * y


def _proj_res(a, w, x, gate, rows_per_seq, transposed=False):
    t, d = x.shape
    k = w.shape[0]
    tile = min(FFN_ROW_TILE, t)
    a_spec = pl.BlockSpec((k, tile), lambda i: (0, i)) if transposed else pl.BlockSpec((tile, k), lambda i: (i, 0))
    return pl.pallas_call(
        functools.partial(_proj_res_kernel, transposed=transposed),
        grid=(t // tile,),
        in_specs=[a_spec, pl.BlockSpec((k, d), lambda i: (0, 0)),
                  pl.BlockSpec((tile, d), lambda i: (i, 0)), _mod_specs(t, tile, rows_per_seq, d)],
        out_specs=pl.BlockSpec((tile, d), lambda i: (i, 0)),
        out_shape=jax.ShapeDtypeStruct((t, d), F32),
        compiler_params=_params(("arbitrary",)),
        name="proj_residual",
    )(a, w, x, _mod_arr(gate, rows_per_seq, tile))


def _ffn_kernel(x_ref, g_ref, sh_ref, sc_ref, gate_ref, wg_ref, wu_ref, wd_ref, o_ref, h_sc, acc_sc):
    f = pl.program_id(1)

    @pl.when(f == 0)
    def _():
        h_sc[...] = _modulate(x_ref[...], g_ref[...], sh_ref[0], sc_ref[0]).astype(BF16)
        acc_sc[...] = jnp.zeros(acc_sc.shape, F32)

    h = h_sc[...]
    a = (_silu(_dot(h, wg_ref[...])) * _dot(h, wu_ref[...])).astype(BF16)
    acc_sc[...] += _dot(a, wd_ref[...])

    @pl.when(f == pl.num_programs(1) - 1)
    def _():
        o_ref[...] = x_ref[...] + gate_ref[0] * acc_sc[...]


def _col_tile(ff):
    return FFN_COL_TILE if ff % FFN_COL_TILE == 0 else ff


def _ffn(x, g, shift, scale, gate, wg, wu, wd, rows_per_seq):
    t, d = x.shape
    ff = wg.shape[1]
    tile = min(DENSE_FFN_ROW_TILE, t if rows_per_seq == 1 else rows_per_seq)
    tf = _col_tile(ff)
    mod = _mod_specs(t, tile, rows_per_seq, d)
    row = pl.BlockSpec((tile, d), lambda i, f: (i, 0))
    return pl.pallas_call(
        _ffn_kernel,
        grid=(t // tile, ff // tf),
        in_specs=[row, pl.BlockSpec(g.shape, lambda i, f: (0, 0)), mod, mod, mod,
                  pl.BlockSpec((d, tf), lambda i, f: (0, f)), pl.BlockSpec((d, tf), lambda i, f: (0, f)),
                  pl.BlockSpec((tf, d), lambda i, f: (f, 0))],
        out_specs=row,
        out_shape=jax.ShapeDtypeStruct((t, d), F32),
        scratch_shapes=[pltpu.VMEM((tile, d), BF16), pltpu.VMEM((tile, d), F32)],
        compiler_params=_params(("arbitrary", "arbitrary")),
        name="swiglu",
    )(x, g, *[_mod_arr(m, rows_per_seq, tile) for m in (shift, scale, gate)], wg, wu, wd)


_E1, _E2, _R1, _R2, _W1, _W2 = range(6)


def _router_kernel(x_ref, g_ref, sh_ref, sc_ref, wr_ref, meta_ref, cnt_ref, h3_ref, cnt_sc):
    i = pl.program_id(0)

    @pl.when(i == 0)
    def _():
        cnt_sc[...] = jnp.zeros(cnt_sc.shape, F32)

    h = _modulate(x_ref[...], g_ref[...], sh_ref[0], sc_ref[0])
    for j in range(h3_ref.shape[1]):
        h3_ref[:, j, :] = h[:, j * LANE:(j + 1) * LANE]
    hi = h.astype(BF16)
    lo = (h - hi.astype(F32)).astype(BF16)
    w = wr_ref[...]
    whi = w.astype(BF16)
    wlo = (w - whi.astype(F32)).astype(BF16)
    logits = _dot(hi, whi) + _dot(lo, whi) + _dot(hi, wlo)
    lane = lax.broadcasted_iota(jnp.int32, logits.shape, 1).astype(F32)
    logits = jnp.where(lane < N_EXPERTS, logits, NEG)
    p = jnp.exp(logits - jnp.max(logits, axis=1, keepdims=True))
    p = p / jnp.sum(p, axis=1, keepdims=True)
    p1 = jnp.max(p, axis=1, keepdims=True)
    i1 = jnp.min(jnp.where(p == p1, lane, float(LANE)), axis=1, keepdims=True)
    rest = jnp.where(lane == i1, -1.0, p)
    p2 = jnp.max(rest, axis=1, keepdims=True)
    i2 = jnp.min(jnp.where(rest == p2, lane, float(LANE)), axis=1, keepdims=True)
    den = p1 + p2
    hit1, hit2 = lane == i1, lane == i2
    onehot = jnp.where(hit1 | hit2, 1.0, 0.0)
    tile = onehot.shape[0]
    ltri = jnp.where(lax.broadcasted_iota(jnp.int32, (tile, tile), 0)
                     >= lax.broadcasted_iota(jnp.int32, (tile, tile), 1), 1.0, 0.0).astype(BF16)
    incl = _dot(ltri, onehot.astype(BF16))
    before = cnt_sc[...] + incl - onehot
    r1 = jnp.sum(jnp.where(hit1, before, 0.0), axis=1, keepdims=True)
    r2 = jnp.sum(jnp.where(hit2, before, 0.0), axis=1, keepdims=True)
    cnt = cnt_sc[...] + incl[tile - 1:tile, :]
    cnt_sc[...] = cnt
    cnt_ref[...] = jnp.broadcast_to(cnt, cnt_ref.shape)
    rec = jnp.zeros(logits.shape, F32)
    for ln, val in ((_E1, i1), (_E2, i2), (_R1, r1), (_R2, r2), (_W1, p1 / den), (_W2, p2 / den)):
        rec = jnp.where(lane == ln, val, rec)
    meta_ref[...] = rec


def _router(x, g, shift, scale, w_router, rows_per_seq):
    t, d = x.shape
    tile = min(FFN_ROW_TILE, t)
    wr = jnp.concatenate([w_router, jnp.zeros((d, LANE - w_router.shape[1]), F32)], axis=1)
    mod = _mod_specs(t, tile, rows_per_seq, d)
    return pl.pallas_call(
        _router_kernel,
        grid=(t // tile,),
        in_specs=[pl.BlockSpec((tile, d), lambda i: (i, 0)), pl.BlockSpec(g.shape, lambda i: (0, 0)), mod, mod,
                  pl.BlockSpec((d, LANE), lambda i: (0, 0))],
        out_specs=[pl.BlockSpec((tile, LANE), lambda i: (i, 0)), pl.BlockSpec((SUBLANE, LANE), lambda i: (0, 0)),
                   pl.BlockSpec((tile, d // LANE, LANE), lambda i: (i, 0, 0))],
        out_shape=[jax.ShapeDtypeStruct((t, LANE), F32), jax.ShapeDtypeStruct((SUBLANE, LANE), F32),
                   jax.ShapeDtypeStruct((t, d // LANE, LANE), F32)],
        scratch_shapes=[pltpu.VMEM((1, LANE), F32)],
        compiler_params=_params(("arbitrary",)),
        name="router",
    )(x, g, _mod_arr(shift, rows_per_seq, tile), _mod_arr(scale, rows_per_seq, tile), wr)


def _row_copy(src, dst, sem):
    return pltpu.make_async_copy(src, dst, sem)


def _dispatch_kernel(pos_ref, tail_ref, h_ref, xs_ref, zero_sc, sem, *, tile):
    i = pl.program_id(0)
    base = i * tile * 2

    @pl.when(i == 0)
    def _():
        zero_sc[...] = jnp.zeros(zero_sc.shape, F32)
        for e in range(N_EXPERTS):
            fill = _row_copy(zero_sc, xs_ref.at[pl.ds(tail_ref[e], tile)], sem)
            fill.start()
            fill.wait()

    def issue(t, carry):
        _row_copy(h_ref.at[t], xs_ref.at[pos_ref[base + 2 * t]], sem).start(priority=0)
        _row_copy(h_ref.at[t], xs_ref.at[pos_ref[base + 2 * t + 1]], sem).start(priority=1)
        return carry

    lax.fori_loop(0, tile, issue, 0)
    for _ in range(2):
        _row_copy(h_ref, xs_ref.at[pl.ds(0, tile)], sem).wait()


def _dispatch(pos, tails, h3, n_rows, tile):
    t, chunks, _ = h3.shape
    grid_spec = pltpu.PrefetchScalarGridSpec(
        num_scalar_prefetch=2,
        grid=(t // tile,),
        in_specs=[pl.BlockSpec((tile, chunks, LANE), lambda i, p, tl: (i, 0, 0))],
        out_specs=pl.BlockSpec(memory_space=pl.ANY),
        scratch_shapes=[pltpu.VMEM((tile, chunks, LANE), F32), pltpu.SemaphoreType.DMA(())],
    )
    return pl.pallas_call(
        functools.partial(_dispatch_kernel, tile=tile),
        grid_spec=grid_spec,
        out_shape=jax.ShapeDtypeStruct((n_rows, chunks, LANE), F32),
        compiler_params=_params(("arbitrary",), disable_bounds_checks=True),
        name="moe_dispatch",
    )(pos, tails, h3)


def _moe_ffn_kernel(blk_ref, exp_ref, nt_ref, dst_ref, x_ref, wg_ref, wu_ref, wd_ref, y_ref, h_sc, acc_sc, y_sc, sem,
                    *, tile):
    i, f = pl.program_id(0), pl.program_id(1)
    chunks = x_ref.shape[1]
    last_f = f == pl.num_programs(1) - 1
    slot = i % 2

    def scattered(which):
        return _row_copy(y_sc.at[which], y_ref.at[pl.ds(0, tile)], sem.at[which])

    @pl.when(i < nt_ref[0])
    def _():
        @pl.when(f == 0)
        def _():
            h_sc[...] = jnp.concatenate([x_ref[:, j, :] for j in range(chunks)], axis=1).astype(BF16)
            acc_sc[...] = jnp.zeros(acc_sc.shape, F32)

        h = h_sc[...]
        a = (_silu(_dot(h, wg_ref[0])) * _dot(h, wu_ref[0])).astype(BF16)
        acc_sc[...] += _dot(a, wd_ref[0])

        @pl.when(last_f)
        def _():
            @pl.when(i >= 2)
            def _():
                scattered(slot).wait()

            for j in range(chunks):
                y_sc[slot, :, j, :] = acc_sc[:, j * LANE:(j + 1) * LANE]
            base = i * tile

            def issue(r, carry):
                _row_copy(y_sc.at[slot, 2 * r], y_ref.at[dst_ref[base + 2 * r]], sem.at[slot]).start(priority=0)
                _row_copy(y_sc.at[slot, 2 * r + 1], y_ref.at[dst_ref[base + 2 * r + 1]],
                          sem.at[slot]).start(priority=1)
                return carry

            lax.fori_loop(0, tile // 2, issue, 0)

    @pl.when((i == pl.num_programs(0) - 1) & last_f)
    def _():
        scattered((nt_ref[0] - 1) % 2).wait()

        @pl.when(nt_ref[0] >= 2)
        def _():
            scattered(nt_ref[0] % 2).wait()


def _moe_ffn(blk_tab, exp_tab, n_tiles, dst, xs, wg, wu, wd, n_out, tile):
    n_rows, chunks, _ = xs.shape
    d = chunks * LANE
    ff = wg.shape[2]
    tf = _col_tile(ff)
    nf = ff // tf
    col = lambda i, f, nt: jnp.where(i < nt[0], f, nf - 1)
    grid_spec = pltpu.PrefetchScalarGridSpec(
        num_scalar_prefetch=4,
        grid=(n_rows // tile, nf),
        in_specs=[pl.BlockSpec((tile, chunks, LANE), lambda i, f, bt, et, nt, ds: (bt[i], 0, 0)),
                  pl.BlockSpec((1, d, tf), lambda i, f, bt, et, nt, ds: (et[i], 0, col(i, f, nt))),
                  pl.BlockSpec((1, d, tf), lambda i, f, bt, et, nt, ds: (et[i], 0, col(i, f, nt))),
                  pl.BlockSpec((1, tf, d), lambda i, f, bt, et, nt, ds: (et[i], col(i, f, nt), 0))],
        out_specs=pl.BlockSpec(memory_space=pl.ANY),
        scratch_shapes=[pltpu.VMEM((tile, d), BF16), pltpu.VMEM((tile, d), F32),
                        pltpu.VMEM((2, tile, chunks, LANE), F32), pltpu.SemaphoreType.DMA((2,))],
    )
    return pl.pallas_call(
        functools.partial(_moe_ffn_kernel, tile=tile),
        grid_spec=grid_spec,
        out_shape=jax.ShapeDtypeStruct((n_out, chunks, LANE), F32),
        compiler_params=_params(("arbitrary", "arbitrary"), disable_bounds_checks=True),
        name="moe_swiglu",
    )(blk_tab, exp_tab, n_tiles, dst, xs, wg, wu, wd)


def _combine_kernel(meta_ref, x_ref, gate_ref, y_ref, o_ref):
    meta = meta_ref[...]
    w1, w2 = meta[:, _W1:_W1 + 1], meta[:, _W2:_W2 + 1]
    gate = gate_ref[0]
    chunks = y_ref.shape[1] // 2
    for j in range(chunks):
        sl = slice(j * LANE, (j + 1) * LANE)
        o_ref[:, sl] = x_ref[:, sl] + gate[:, sl] * (w1 * y_ref[:, j, :] + w2 * y_ref[:, chunks + j, :])


def _combine(meta, x, gate, y2, rows_per_seq, tile):
    t, d = x.shape
    row = lambda n: pl.BlockSpec((tile, n), lambda i: (i, 0))
    return pl.pallas_call(
        _combine_kernel,
        grid=(t // tile,),
        in_specs=[row(LANE), row(d), _mod_specs(t, tile, rows_per_seq, d),
                  pl.BlockSpec((tile, y2.shape[1], LANE), lambda i: (i, 0, 0))],
        out_specs=row(d),
        out_shape=jax.ShapeDtypeStruct((t, d), F32),
        compiler_params=_params(("arbitrary",)),
        name="moe_combine",
    )(meta, x, _mod_arr(gate, rows_per_seq, tile), y2)


def _moe(x, g, shift, scale, gate, w_router, wg, wu, wd, layer, rows_per_seq):
    t, d = x.shape
    tile = min(FFN_ROW_TILE, t)
    meta, counts, h3 = _router(x, g, shift, scale, w_router, rows_per_seq)
    n_max = 2 * t // tile + N_EXPERTS + 1
    cnt = counts[0, :N_EXPERTS].astype(jnp.int32)
    tiles = (cnt + tile - 1) // tile
    ends = jnp.cumsum(tiles)
    n_tiles = ends[-1]
    blk_tab = jnp.minimum(jnp.arange(n_max, dtype=jnp.int32), n_tiles - 1)
    exp_tab = jnp.sum(blk_tab[:, None] >= ends[None, :], axis=1).astype(jnp.int32) + layer * N_EXPERTS
    offs = (ends - tiles) * tile
    experts = meta[:, _E1:_E2 + 1].astype(jnp.int32)
    ranks = meta[:, _R1:_R2 + 1].astype(jnp.int32)
    pos = (offs[experts] + ranks).reshape(-1)
    n_rows = n_max * tile
    xs = _dispatch(pos, offs + cnt, h3, n_rows, tile)
    dst = (2 * t + jnp.arange(n_rows, dtype=jnp.int32)).at[pos].set(jnp.arange(2 * t, dtype=jnp.int32))
    ys = _moe_ffn(blk_tab, exp_tab, n_tiles.reshape(1).astype(jnp.int32), dst, xs, wg, wu, wd, 2 * t + n_rows, tile)
    return _combine(meta, x, gate, ys.reshape(-1, 2 * ys.shape[1], LANE), rows_per_seq, tile)


def _softplus(x):
    return jnp.maximum(x, 0.0) + jnp.log1p(jnp.exp(-jnp.abs(x)))


def _ssm_in_kernel(x_ref, g_ref, sh_ref, sc_ref, wz_ref, wx_ref, wdt_ref, cw_ref, cb_ref, dtb_ref,
                   z_ref, xs_ref, b_ref, c_ref, dt_ref, conv_ref, ext_sc, *, tile, tiles_per_seq, d_inner, n_heads):
    i = pl.program_id(0)

    @pl.when(i % tiles_per_seq == 0)
    def _():
        ext_sc[0:SUBLANE, :] = jnp.zeros((SUBLANE, ext_sc.shape[1]), F32)

    h = _modulate(x_ref[...], g_ref[...], sh_ref[0], sc_ref[0]).astype(BF16)
    z_ref[...] = _dot(h, wz_ref[...])
    u = _dot(h, wx_ref[...])
    ext_sc[SUBLANE:SUBLANE + tile, :] = u
    cw = cw_ref[...]
    y = cw[0:1] * ext_sc[SUBLANE - 3:SUBLANE - 3 + tile, :]
    y = y + cw[1:2] * ext_sc[SUBLANE - 2:SUBLANE - 2 + tile, :]
    y = y + cw[2:3] * ext_sc[SUBLANE - 1:SUBLANE - 1 + tile, :]
    y = y + cw[3:4] * u + cb_ref[...]
    tail = ext_sc[tile:tile + SUBLANE, :]
    conv_ref[0] = tail
    ext_sc[0:SUBLANE, :] = tail
    xbc = _silu(y)
    gn = (xbc.shape[1] - d_inner) // 2
    xs_ref[...] = xbc[:, :d_inner]
    b_ref[...] = xbc[:, d_inner:d_inner + gn].astype(BF16)
    c_ref[...] = xbc[:, d_inner + gn:].astype(BF16)
    dt = _softplus(_dot(h, wdt_ref[...]) + dtb_ref[...])
    lane = lax.broadcasted_iota(jnp.int32, dt.shape, 1)
    dt_ref[...] = jnp.where(lane < n_heads, dt, 0.0)


def _ssm_in(x, g, shift, scale, w, rows_per_seq):
    t, d = x.shape
    tile = min(ROW_TILE, rows_per_seq)
    d_inner, conv_dim = w["wz"].shape[1], w["wx"].shape[1]
    gn = (conv_dim - d_inner) // 2
    n_seq = t // rows_per_seq
    tiles_per_seq = rows_per_seq // tile
    full = lambda a: pl.BlockSpec(a.shape, lambda i: (0,) * a.ndim)
    row = lambda n: pl.BlockSpec((tile, n), lambda i: (i, 0))
    mod = _mod_specs(t, tile, rows_per_seq, d)
    ws = [w["wz"], w["wx"], w["wdt"], w["conv_w"], w["conv_b"], w["dt_bias"]]
    return pl.pallas_call(
        functools.partial(_ssm_in_kernel, tile=tile, tiles_per_seq=tiles_per_seq, d_inner=d_inner,
                          n_heads=d_inner // SSM_HEAD_DIM),
        grid=(t // tile,),
        in_specs=[row(d), full(g), mod, mod] + [full(a) for a in ws],
        out_specs=[row(d_inner), row(d_inner), row(gn), row(gn), row(LANE),
                   pl.BlockSpec((1, SUBLANE, conv_dim), lambda i: (i // tiles_per_seq, 0, 0))],
        out_shape=[jax.ShapeDtypeStruct((t, d_inner), F32), jax.ShapeDtypeStruct((t, d_inner), F32),
                   jax.ShapeDtypeStruct((t, gn), BF16), jax.ShapeDtypeStruct((t, gn), BF16),
                   jax.ShapeDtypeStruct((t, LANE), F32), jax.ShapeDtypeStruct((n_seq, SUBLANE, conv_dim), F32)],
        scratch_shapes=[pltpu.VMEM((tile + SUBLANE, conv_dim), F32)],
        compiler_params=_params(("arbitrary",)),
        name="ssm_in_conv",
    )(x, g, _mod_arr(shift, rows_per_seq, tile), _mod_arr(scale, rows_per_seq, tile), *ws)


def _ssd_kernel(xs_ref, b_ref, c_ref, dt_ref, z_ref, alog_ref, e_ref, dskip_ref, ng_ref, y_ref, hout_ref,
                st_sc, y_sc, *, cl, n_groups, pairs_per_group):
    c = pl.program_id(1)

    @pl.when(c == 0)
    def _():
        st_sc[...] = jnp.zeros(st_sc.shape, F32)

    dt = dt_ref[0]
    d_a = dt * (-jnp.exp(alog_ref[...]))
    rowi = lax.broadcasted_iota(jnp.int32, (cl, cl), 0)
    coli = lax.broadcasted_iota(jnp.int32, (cl, cl), 1)
    causal = rowi >= coli
    ltri = jnp.where(causal, 1.0, 0.0).astype(BF16)
    cum = sum(_dot(ltri, part) for part in _split3(d_a))
    cum_t = cum.T
    expand = e_ref[...]
    dt_x = sum(_dot(part, expand) for part in _split3(dt))
    cum_x = sum(_dot(part, expand) for part in _split3(cum))
    xs = xs_ref[0]
    xdt = xs * dt_x
    xdt_b = xdt.astype(BF16)
    w_out = (xdt * jnp.exp(cum_x[cl - 1:cl, :] - cum_x)).astype(BF16)
    grow = jnp.exp(cum_x)
    lane = lax.broadcasted_iota(jnp.int32, (cl, LANE), 1)
    first = lax.broadcasted_iota(jnp.int32, (LANE, 1), 0) < SSM_HEAD_DIM
    for g in range(n_groups):
        bg = b_ref[0, :, g * D_STATE:(g + 1) * D_STATE]
        cg = c_ref[0, :, g * D_STATE:(g + 1) * D_STATE]
        cb = _dot_nt(cg, bg)
        for j in range(pairs_per_group):
            pr = g * pairs_per_group + j
            h0, h1 = 2 * pr, 2 * pr + 1
            sl = slice(pr * LANE, (pr + 1) * LANE)
            seg0 = jnp.exp(jnp.where(causal, cum[:, h0:h0 + 1] - cum_t[h0:h0 + 1, :], -jnp.inf))
            seg1 = jnp.exp(jnp.where(causal, cum[:, h1:h1 + 1] - cum_t[h1:h1 + 1, :], -jnp.inf))
            xp = xdt_b[:, sl]
            y_diag = jnp.where(lane < SSM_HEAD_DIM, _dot((cb * seg0).astype(BF16), xp),
                               _dot((cb * seg1).astype(BF16), xp))
            st = st_sc[pr]
            y_sc[:, sl] = y_diag + _dot_nt(cg, st.astype(BF16)) * grow[:, sl]
            decay = jnp.exp(jnp.where(first, cum_t[h0:h0 + 1, cl - 1:cl], cum_t[h1:h1 + 1, cl - 1:cl]))
            st_sc[pr] = st * decay + _dot_tn(w_out[:, sl], bg)
    zz = z_ref[0]
    y = (y_sc[...] + dskip_ref[...] * xs) * _silu(zz)
    y_ref[0] = _rms(y, ng_ref[...]).astype(y_ref.dtype)

    @pl.when(c == pl.num_programs(1) - 1)
    def _():
        hout_ref[0] = st_sc[...]


def _ssd(xs, bm, cm, dt, z, w, n_seq, seq_len):
    d_inner = xs.shape[1]
    gn = bm.shape[1]
    n_pairs = d_inner // LANE
    cl = min(SSD_CHUNK, seq_len)
    nc = seq_len // cl
    r3 = lambda a: a.reshape(n_seq, seq_len, a.shape[1])
    blk = lambda n: pl.BlockSpec((1, cl, n), lambda b, c: (b, c, 0))
    full = lambda a: pl.BlockSpec(a.shape, lambda b, c: (0,) * a.ndim)
    consts = [w["a_log"], w["expand"], w["d_skip"], w["norm_g"]]
    y, h_out = pl.pallas_call(
        functools.partial(_ssd_kernel, cl=cl, n_groups=SSM_GROUPS, pairs_per_group=n_pairs // SSM_GROUPS),
        grid=(n_seq, nc),
        in_specs=[blk(d_inner), blk(gn), blk(gn), blk(LANE), blk(d_inner)] + [full(a) for a in consts],
        out_specs=[blk(d_inner), pl.BlockSpec((1, n_pairs, LANE, D_STATE), lambda b, c: (b, 0, 0, 0))],
        out_shape=[jax.ShapeDtypeStruct((n_seq, seq_len, d_inner), BF16),
                   jax.ShapeDtypeStruct((n_seq, n_pairs, LANE, D_STATE), F32)],
        scratch_shapes=[pltpu.VMEM((n_pairs, LANE, D_STATE), F32), pltpu.VMEM((cl, d_inner), F32)],
        compiler_params=_params(("arbitrary", "arbitrary")),
        name="ssd_scan",
    )(r3(xs), r3(bm), r3(cm), r3(dt), r3(z), *consts)
    return y.reshape(n_seq * seq_len, d_inner), h_out


def _prep_ssm(w_in, conv_w, conv_b, dt_bias, a_log, d_skip, norm_g, w_out):
    d = w_in.shape[0]
    d_inner = w_out.shape[0]
    n_heads = d_inner // SSM_HEAD_DIM
    conv_dim = conv_w.shape[1]
    pad = LANE - n_heads
    w_dt = w_in[:, d_inner + conv_dim:]
    rep = lambda v: jnp.repeat(v, SSM_HEAD_DIM, axis=-1)
    head_of = jnp.arange(d_inner) // SSM_HEAD_DIM
    return dict(
        wz=w_in[:, :d_inner].astype(BF16),
        wx=w_in[:, d_inner:d_inner + conv_dim].astype(BF16),
        wdt=jnp.concatenate([w_dt, jnp.zeros((d, pad), F32)], axis=1).astype(BF16),
        wdt_x=rep(w_dt).astype(BF16),
        conv_w=conv_w, conv_b=conv_b.reshape(1, conv_dim),
        dt_bias=jnp.concatenate([dt_bias, jnp.zeros((pad,), F32)]).reshape(1, LANE),
        dt_bias_x=rep(dt_bias).reshape(1, d_inner),
        a_log=jnp.concatenate([a_log, jnp.zeros((pad,), F32)]).reshape(1, LANE),
        a_log_x=rep(a_log).reshape(1, d_inner),
        expand=(jnp.arange(LANE)[:, None] == head_of[None, :]).astype(BF16),
        d_skip=rep(d_skip).reshape(1, d_inner),
        norm_g=norm_g.reshape(1, d_inner),
        w_out=w_out.astype(BF16),
    )


def _ssm_step_in_kernel(x_ref, g_ref, sh_ref, sc_ref, wz_ref, wx_ref, wdt_ref, cw_ref, cb_ref, dtb_ref, alog_ref,
                        buf_ref, z_ref, xs_ref, b_ref, c_ref, conv_ref, xdt_t_ref, dec_t_ref, *, d_inner):
    h = _modulate(x_ref[...], g_ref[...], sh_ref[0], sc_ref[0]).astype(BF16)
    z_ref[...] = _dot(h, wz_ref[...])
    u = _dot(h, wx_ref[...])
    cw = cw_ref[...]
    y = cw[0:1] * buf_ref[0] + cw[1:2] * buf_ref[1] + cw[2:3] * buf_ref[2] + cw[3:4] * u + cb_ref[...]
    conv_ref[0] = buf_ref[1]
    conv_ref[1] = buf_ref[2]
    conv_ref[2] = u
    xbc = _silu(y)
    gn = (xbc.shape[1] - d_inner) // 2
    xs = xbc[:, :d_inner]
    xs_ref[...] = xs
    b_ref[...] = xbc[:, d_inner:d_inner + gn]
    c_ref[...] = xbc[:, d_inner + gn:]
    dt = _softplus(_dot(h, wdt_ref[...]) + dtb_ref[...])
    xdt_t_ref[...] = (xs * dt).T
    dec_t_ref[...] = jnp.exp(dt * (-jnp.exp(alog_ref[...]))).T


def _ssm_step_in(x, g, shift, scale, buf_t, w):
    n, d = x.shape
    d_inner, conv_dim = w["wz"].shape[1], w["wx"].shape[1]
    gn = (conv_dim - d_inner) // 2
    ws = [w["wz"], w["wx"], w["wdt_x"], w["conv_w"], w["conv_b"], w["dt_bias_x"], w["a_log_x"], buf_t]
    sds = jax.ShapeDtypeStruct
    return pl.pallas_call(
        functools.partial(_ssm_step_in_kernel, d_inner=d_inner),
        out_shape=[sds((n, d_inner), F32), sds((n, d_inner), F32), sds((n, gn), F32), sds((n, gn), F32),
                   sds((CONV_W - 1, n, conv_dim), F32), sds((d_inner, n), F32), sds((d_inner, n), F32)],
        compiler_params=pltpu.CompilerParams(vmem_limit_bytes=VMEM_LIMIT),
        name="ssm_step_in",
    )(x, g, shift.reshape(1, n, d), scale.reshape(1, n, d), *ws)


def _ssm_step_kernel(st_ref, xdt_t_ref, dec_t_ref, b_ref, c_ref, *rest, n_groups):
    hout_ref, y_t_ref = rest[-2:]
    b = pl.program_id(0)

    @pl.when(b == 0)
    def _():
        y_t_ref[...] = jnp.zeros(y_t_ref.shape, F32)

    rg = st_ref.shape[2] // n_groups
    mine = lax.broadcasted_iota(jnp.int32, (rg, LANE), 1) == b
    for g in range(n_groups):
        rs = slice(g * rg, (g + 1) * rg)
        xcol = jnp.sum(jnp.where(mine, xdt_t_ref[rs, :], 0.0), axis=1, keepdims=True)
        dcol = jnp.sum(jnp.where(mine, dec_t_ref[rs, :], 0.0), axis=1, keepdims=True)
        bg = b_ref[0, :, g * D_STATE:(g + 1) * D_STATE]
        cg = c_ref[0, :, g * D_STATE:(g + 1) * D_STATE]
        s_new = st_ref[0, 0, rs, :] * dcol + xcol * bg
        hout_ref[0, 0, rs, :] = s_new
        ycol = jnp.sum(s_new * cg, axis=1, keepdims=True)
        y_t_ref[rs, :] = jnp.where(mine, ycol, y_t_ref[rs, :])


def _ssm_step(layer, state, xdt_t, dec_t, bm, cm, new_state=None):
    _, n, rows, ns = state.shape
    full = lambda a: pl.BlockSpec(a.shape, lambda b: (0,) * a.ndim)
    seq_row = pl.BlockSpec((1, 1, bm.shape[1]), lambda b: (b, 0, 0))
    slab = pl.BlockSpec((1, 1, rows, ns), lambda b: (layer, b, 0, 0))
    args = [state, xdt_t, dec_t, bm.reshape(n, 1, -1), cm.reshape(n, 1, -1)]
    in_specs = [slab, full(xdt_t), full(dec_t), seq_row, seq_row]
    aliases = {}
    if new_state is not None:
        args.append(new_state)
        in_specs.append(pl.BlockSpec(memory_space=pl.ANY))
        aliases = {len(args) - 1: 0}
    return pl.pallas_call(
        functools.partial(_ssm_step_kernel, n_groups=SSM_GROUPS),
        grid=(n,),
        in_specs=in_specs,
        out_specs=[slab, pl.BlockSpec((rows, n), lambda b: (0, 0))],
        out_shape=[jax.ShapeDtypeStruct(state.shape, F32), jax.ShapeDtypeStruct((rows, n), F32)],
        input_output_aliases=aliases,
        compiler_params=_params(("arbitrary",)),
        name="ssm_step",
    )(*args)


def _ssm_step_out_kernel(y_t_ref, xs_ref, z_ref, dskip_ref, ng_ref, w_ref, x_ref, gate_ref, o_ref):
    y = (y_t_ref[...].T + dskip_ref[...] * xs_ref[...]) * _silu(z_ref[...])
    y = _rms(y, ng_ref[...]).astype(BF16)
    o_ref[...] = x_ref[...] + gate_ref[0] * _dot(y, w_ref[...])


def _ssm_step_out(y_t, xs, z, w, x, gate):
    n, d = x.shape
    return pl.pallas_call(
        _ssm_step_out_kernel,
        out_shape=jax.ShapeDtypeStruct((n, d), F32),
        compiler_params=pltpu.CompilerParams(vmem_limit_bytes=VMEM_LIMIT),
        name="ssm_step_out",
    )(y_t, xs, z, w["d_skip"], w["norm_g"], w["w_out"], x, gate.reshape(1, n, d))


def _final_norm_kernel(x_ref, g_ref, o_ref):
    o_ref[...] = _rms(x_ref[...], g_ref[...])


def _final_norm(x, g):
    t, d = x.shape
    tile = min(FFN_ROW_TILE, t)
    return pl.pallas_call(
        _final_norm_kernel,
        grid=(t // tile,),
        in_specs=[pl.BlockSpec((tile, d), lambda i: (i, 0)), pl.BlockSpec((1, d), lambda i: (0, 0))],
        out_specs=pl.BlockSpec((tile, d), lambda i: (i, 0)),
        out_shape=jax.ShapeDtypeStruct((t, d), F32),
        compiler_params=_params(("arbitrary",)),
        name="final_norm",
    )(x, g.reshape(1, d))


def kernel(x_prompt, x_sample, c_prompt, c_sample, cache_kv_latent, cache_k_rope, page_table, state_ssm, state_conv, w_ada, b_ada, norm_mix_g, norm_ffn_g, norm_final_g, mla_w_in, mla_q_norm_g, mla_kv_norm_g, mla_w_qb, mla_w_kvb, mla_w_o, ssm_w_in, ssm_conv_w, ssm_conv_b, ssm_dt_bias, ssm_a_log, ssm_d, ssm_norm_g, ssm_w_out, ffn_w_gate, ffn_w_up, ffn_w_down, moe_w_router, moe_w_gate, moe_w_up, moe_w_down):
    nb, seq, d = x_prompt.shape
    ns = x_sample.shape[0]
    depth = w_ada.shape[0]
    past_len = page_table.shape[1] * cache_kv_latent.shape[2]
    attn_scale = (QK_NOPE + QK_ROPE) ** -0.5

    xp = x_prompt.reshape(nb * seq, d)
    xs = x_sample.reshape(ns, d)
    mods = _ada(jnp.concatenate([c_prompt, c_sample], axis=0), w_ada, b_ada)
    mods = mods.reshape(depth, nb + ns, 6, d)
    tabs_p = _rope_tables(jnp.arange(seq), attn_scale * LOG2E)
    tabs_s = _rope_tables(jnp.full((ns,), past_len), attn_scale)
    state = state_ssm.reshape(state_ssm.shape[0], ns, -1, D_STATE)
    cache_rope_t = cache_k_rope.transpose(0, 1, 3, 2)
    moe_w = [w.astype(BF16).reshape((-1,) + w.shape[2:]) for w in (moe_w_gate, moe_w_up, moe_w_down)]

    lat_p, rope_p, lat_s, rope_s = [], [], [], []
    ssm_p, conv_p, ssm_s, conv_s = [], [], None, []
    for i in range(depth):
        j = i // 2
        mp = [mods[i, :nb, k] for k in range(6)]
        ms = [mods[i, nb:, k] for k in range(6)]
        g_mix = norm_mix_g[i].reshape(1, d)
        g_ffn = norm_ffn_g[i].reshape(1, d)
        if i % 2 == 0:
            w = _prep_mla(mla_w_in[j], mla_w_qb[j], mla_w_kvb[j])
            qg = mla_q_norm_g[j].reshape(1, -1)
            kvg = mla_kv_norm_g[j].reshape(1, -1)
            w_o = mla_w_o[j].astype(BF16)
            ckv, kr, q, k, vt = _mla_proj(xp, g_mix, mp[0], mp[1], tabs_p, qg, kvg, w, seq, True)
            o_t = _flash(q, k, vt.reshape(MLA_HEADS, V_DIM, nb * seq), nb, seq)
            xp = _proj_res(o_t.reshape(MLA_HEADS * V_DIM, nb * seq), w_o, xp, mp[2], seq, transposed=True)
            lat_p.append(ckv.reshape(nb, seq, -1))
            rope_p.append(kr.reshape(nb, seq, -1))

            ckv, kr, q = _mla_proj(xs, g_mix, ms[0], ms[1], tabs_s, qg, kvg, w, 1, False)
            kv_rank = ckv.shape[1]
            qlat = _headmat(q, w["wabs"], LANE, kv_rank)
            o_lat = _decode_attn(j, page_table, qlat.reshape(ns, MLA_HEADS, kv_rank), q.reshape(ns, MLA_HEADS, LANE),
                                 ckv.reshape(ns, 1, kv_rank), kr.reshape(ns, 1, QK_ROPE), cache_kv_latent,
                                 cache_rope_t)
            o = _headmat(o_lat.reshape(ns, MLA_HEADS * kv_rank), w["wvbd"], 2 * kv_rank, 2 * V_DIM)
            xs = _proj_res(o, w_o, xs, ms[2], 1)
            lat_s.append(ckv.reshape(ns, 1, -1))
            rope_s.append(kr.reshape(ns, 1, -1))
        else:
            w = _prep_ssm(ssm_w_in[j], ssm_conv_w[j], ssm_conv_b[j], ssm_dt_bias[j], ssm_a_log[j], ssm_d[j],
                          ssm_norm_g[j], ssm_w_out[j])
            z, xin, bm, cm, dt, conv = _ssm_in(xp, g_mix, mp[0], mp[1], w, seq)
            y, h_new = _ssd(xin, bm, cm, dt, z, w, nb, seq)
            xp = _proj_res(y, w["w_out"], xp, mp[2], seq)
            ssm_p.append(h_new.reshape(nb, -1, SSM_HEAD_DIM, D_STATE))
            conv_p.append(conv[:, SUBLANE - (CONV_W - 1):])

            buf_t = state_conv[j].transpose(1, 0, 2)
            z, xin, bm, cm, conv, xdt_t, dec_t = _ssm_step_in(xs, g_mix, ms[0], ms[1], buf_t, w)
            ssm_s, y_t = _ssm_step(j, state, xdt_t, dec_t, bm, cm, ssm_s)
            xs = _ssm_step_out(y_t, xin, z, w, xs, ms[2])
            conv_s.append(conv.transpose(1, 0, 2))
        if i % 2 == 0:
            wg = ffn_w_gate[j].astype(BF16)
            wu = ffn_w_up[j].astype(BF16)
            wd = ffn_w_down[j].astype(BF16)
            xp = _ffn(xp, g_ffn, mp[3], mp[4], mp[5], wg, wu, wd, seq)
            xs = _ffn(xs, g_ffn, ms[3], ms[4], ms[5], wg, wu, wd, 1)
        else:
            xp = _moe(xp, g_ffn, mp[3], mp[4], mp[5], moe_w_router[j], *moe_w, j, seq)
            xs = _moe(xs, g_ffn, ms[3], ms[4], ms[5], moe_w_router[j], *moe_w, j, 1)
    y_prompt = _final_norm(xp, norm_final_g).reshape(nb, seq, d)
    y_sample = _final_norm(xs, norm_final_g).reshape(ns, 1, d)
    return (y_prompt, y_sample, jnp.stack(lat_p), jnp.stack(rope_p), jnp.stack(lat_s), jnp.stack(rope_s),
            jnp.stack(ssm_p), jnp.stack(conv_p), ssm_s.reshape(state_ssm.shape), jnp.stack(conv_s))
```

```python
import functools
import math

import jax
import jax.numpy as jnp
from jax import lax
from jax.experimental import pallas as pl
from jax.experimental.pallas import tpu as pltpu

F32 = jnp.float32
BF16 = jnp.bfloat16

RMS_EPS = 1e-6
MLA_HEADS = 16
QK_NOPE = 64
QK_ROPE = 32
V_DIM = 64
ROPE_THETA = 10000.0
PAGE_SIZE = 128
SSM_HEAD_DIM = 64
SSM_GROUPS = 4
D_STATE = 128
CONV_W = 4
SSD_CHUNK = 128
N_EXPERTS = 8

LANE = 128
SUBLANE = 8
VMEM_LIMIT = 56 << 20

ROW_TILE = 256
FFN_ROW_TILE = 512
DENSE_FFN_ROW_TILE = 1024
FFN_COL_TILE = 1408
ATTN_TILE = 512
NEG = -1e30
LOG2E = math.log2(math.e)


def _params(sem, **kw):
    return pltpu.CompilerParams(dimension_semantics=sem, vmem_limit_bytes=VMEM_LIMIT, **kw)


def _dot(a, b):
    return jnp.dot(a, b, preferred_element_type=F32)


def _dot_nt(a, b):
    return lax.dot_general(a, b, (((1,), (1,)), ((), ())), preferred_element_type=F32)


def _dot_tn(a, b):
    return lax.dot_general(a, b, (((0,), (0,)), ((), ())), preferred_element_type=F32)


def _split3(v):
    hi = v.astype(BF16)
    r = v - hi.astype(F32)
    mid = r.astype(BF16)
    lo = (r - mid.astype(F32)).astype(BF16)
    return hi, mid, lo


def _silu(x):
    return x * jax.nn.sigmoid(x)


def _rms(x, g):
    return (x * lax.rsqrt(jnp.mean(x * x, axis=-1, keepdims=True) + RMS_EPS)) * g


def _modulate(x, g, shift, scale):
    return _rms(x, g) * (1.0 + scale) + shift


def _mod_specs(n_rows, tile, rows_per_seq, d):
    if rows_per_seq == 1:
        return pl.BlockSpec((1, tile, d), lambda i, *_: (i, 0, 0))
    tiles_per_seq = rows_per_seq // tile
    return pl.BlockSpec((1, 1, d), lambda i, *_: (i // tiles_per_seq, 0, 0))


def _mod_arr(m, rows_per_seq, tile):
    if rows_per_seq == 1:
        return m.reshape(m.shape[0] // tile, tile, m.shape[1])
    return m.reshape(m.shape[0], 1, m.shape[1])


def _ada_kernel(c_ref, w_ref, b_ref, o_ref):
    s = _silu(c_ref[...]).astype(BF16)
    o_ref[0] = _dot(s, w_ref[0].astype(BF16)) + b_ref[0]


def _ada(c_all, w_ada, b_ada):
    depth, d, d6 = w_ada.shape
    n = c_all.shape[0]
    return pl.pallas_call(
        _ada_kernel,
        grid=(depth, d6 // d),
        in_specs=[pl.BlockSpec((n, d), lambda i, k: (0, 0)),
                  pl.BlockSpec((1, d, d), lambda i, k: (i, 0, k)),
                  pl.BlockSpec((1, 1, d), lambda i, k: (i, 0, k))],
        out_specs=pl.BlockSpec((1, n, d), lambda i, k: (i, 0, k)),
        out_shape=jax.ShapeDtypeStruct((depth, n, d6), F32),
        compiler_params=_params(("arbitrary", "arbitrary")),
        name="adaln",
    )(c_all, w_ada, b_ada.reshape(depth, 1, d6))


def _mla_proj_kernel(x_ref, g_ref, sh_ref, sc_ref, ct_ref, st_ref, cts_ref, sts_ref, wcq_ref, wckv_ref,
                     wkra_ref, wkrb_ref, qg_ref, kvg_ref, wqa_ref, wqb_ref, *rest, heads, with_kv):
    if with_kv:
        wka_ref, wvt_ref, ckv_ref, kr_ref, q_ref, k_ref, vt_ref = rest
    else:
        ckv_ref, kr_ref, q_ref = rest
    h = _modulate(x_ref[...], g_ref[...], sh_ref[0], sc_ref[0]).astype(BF16)
    cq = _rms(_dot(h, wcq_ref[...]), qg_ref[...]).astype(BF16)
    ckv = _rms(_dot(h, wckv_ref[...]), kvg_ref[...])
    kr = _dot(h, wkra_ref[...]) * ct_ref[...] + _dot(h, wkrb_ref[...]) * st_ref[...]
    ckv_ref[...] = ckv
    kr_ref[...] = kr[:, :QK_ROPE]
    rep = lambda t: jnp.concatenate([t] * heads, axis=1)
    q = (_dot(cq, wqa_ref[...]) * rep(cts_ref[...]) + _dot(cq, wqb_ref[...]) * rep(sts_ref[...])).astype(BF16)
    if not with_kv:
        q_ref[...] = q
        return
    ckvb = ckv.astype(BF16)
    k = (_dot(ckvb, wka_ref[...]) + rep(kr)).astype(BF16)
    for hd in range(heads):
        q_ref[hd] = q[:, hd * LANE:(hd + 1) * LANE]
        k_ref[hd] = k[:, hd * LANE:(hd + 1) * LANE]
    vt_ref[...] = _dot_nt(wvt_ref[...], ckvb).astype(BF16)


def _rot_cols(w):
    half = w.shape[-1] // 2
    return jnp.concatenate([-w[..., half:], w[..., :half]], axis=-1)


def _prep_mla(w_in, w_qb, w_kvb):
    d = w_in.shape[0]
    q_rank = w_qb.shape[0]
    kv_rank = w_kvb.shape[0]
    hd = LANE
    pad = hd - QK_ROPE - QK_NOPE
    w_kr = w_in[:, q_rank + kv_rank:]
    zk = jnp.zeros((d, hd - QK_ROPE), F32)
    wq = w_qb.reshape(q_rank, MLA_HEADS, QK_NOPE + QK_ROPE)
    q_nope, q_pe = wq[..., :QK_NOPE], wq[..., QK_NOPE:]
    zq = jnp.zeros((q_rank, MLA_HEADS, pad), F32)
    wkv = w_kvb.reshape(kv_rank, MLA_HEADS, QK_NOPE + V_DIM)
    k_nope, v = wkv[..., :QK_NOPE], wkv[..., QK_NOPE:]
    return dict(
        wcq=w_in[:, :q_rank].astype(BF16),
        wckv=w_in[:, q_rank:q_rank + kv_rank].astype(BF16),
        wkra=jnp.concatenate([w_kr, zk], axis=1).astype(BF16),
        wkrb=jnp.concatenate([_rot_cols(w_kr), zk], axis=1).astype(BF16),
        wqa=jnp.concatenate([q_pe, q_nope, zq], axis=-1).reshape(q_rank, MLA_HEADS * hd).astype(BF16),
        wqb=jnp.concatenate([_rot_cols(q_pe), jnp.zeros((q_rank, MLA_HEADS, hd - QK_ROPE), F32)],
                            axis=-1).reshape(q_rank, MLA_HEADS * hd).astype(BF16),
        wka=jnp.concatenate([jnp.zeros((kv_rank, MLA_HEADS, QK_ROPE), F32), k_nope,
                             jnp.zeros((kv_rank, MLA_HEADS, pad), F32)],
                            axis=-1).reshape(kv_rank, MLA_HEADS * hd).astype(BF16),
        wvt=v.reshape(kv_rank, MLA_HEADS * V_DIM).T.astype(BF16),
        wabs=jnp.concatenate([jnp.zeros((MLA_HEADS, QK_ROPE, kv_rank), F32), k_nope.transpose(1, 2, 0),
                              jnp.zeros((MLA_HEADS, pad, kv_rank), F32)], axis=1).astype(BF16),
        wvbd=_block_diag_pairs(v.transpose(1, 0, 2)).astype(BF16),
    )


def _block_diag_pairs(v):
    h, r, dv = v.shape
    v = v.reshape(h // 2, 2, r, dv)
    z = jnp.zeros((h // 2, r, dv), v.dtype)
    top = jnp.concatenate([v[:, 0], z], axis=-1)
    bot = jnp.concatenate([z, v[:, 1]], axis=-1)
    return jnp.concatenate([top, bot], axis=1)


def _rope_tables(pos, scale):
    half = QK_ROPE // 2
    inv = ROPE_THETA ** (-jnp.arange(half, dtype=F32) / half)
    ang = pos.astype(F32)[:, None] * inv[None, :]
    cos, sin = jnp.cos(ang), jnp.sin(ang)
    n = pos.shape[0]
    ct = jnp.concatenate([cos, cos, jnp.ones((n, LANE - QK_ROPE), F32)], axis=1)
    st = jnp.concatenate([sin, sin, jnp.zeros((n, LANE - QK_ROPE), F32)], axis=1)
    return ct, st, ct * scale, st * scale


def _mla_proj(x, g, shift, scale, tabs, qg, kvg, w, rows_per_seq, with_kv):
    t, d = x.shape
    tile = min(ROW_TILE, t)
    n_tab_tiles = tabs[0].shape[0] // tile
    heads = MLA_HEADS
    kv_rank = w["wckv"].shape[1]
    full = lambda a: pl.BlockSpec(a.shape, lambda i: (0,) * a.ndim)
    row = lambda n: pl.BlockSpec((tile, n), lambda i: (i, 0))
    tab = pl.BlockSpec((tile, LANE), lambda i: (i % n_tab_tiles, 0))
    mod = _mod_specs(t, tile, rows_per_seq, d)
    ws = [w["wcq"], w["wckv"], w["wkra"], w["wkrb"], qg, kvg, w["wqa"], w["wqb"]]
    outs = [jax.ShapeDtypeStruct((t, kv_rank), F32), jax.ShapeDtypeStruct((t, QK_ROPE), F32)]
    out_specs = [row(kv_rank), row(QK_ROPE)]
    if with_kv:
        ws += [w["wka"], w["wvt"]]
        head_major = pl.BlockSpec((heads, tile, LANE), lambda i: (0, i, 0))
        outs += [jax.ShapeDtypeStruct((heads, t, LANE), BF16), jax.ShapeDtypeStruct((heads, t, LANE), BF16),
                 jax.ShapeDtypeStruct((heads * V_DIM, t), BF16)]
        out_specs += [head_major, head_major, pl.BlockSpec((heads * V_DIM, tile), lambda i: (0, i))]
    else:
        outs.append(jax.ShapeDtypeStruct((t, heads * LANE), BF16))
        out_specs.append(row(heads * LANE))
    return pl.pallas_call(
        functools.partial(_mla_proj_kernel, heads=heads, with_kv=with_kv),
        grid=(t // tile,),
        in_specs=[row(d), full(g), mod, mod, tab, tab, tab, tab] + [full(a) for a in ws],
        out_specs=out_specs,
        out_shape=outs,
        compiler_params=_params(("arbitrary",)),
        name="mla_proj",
    )(x, g, _mod_arr(shift, rows_per_seq, tile), _mod_arr(scale, rows_per_seq, tile), *tabs, *ws)


def _flash_kernel(qi_tab, ki_tab, q_ref, k_ref, vt_ref, o_ref, m_sc, l_sc, acc_sc, s0_sc, s1_sc, *, tile, heads):
    t = pl.program_id(1)
    qi = qi_tab[t]
    ki = ki_tab[t]

    @pl.when(ki == 0)
    def _():
        m_sc[...] = jnp.full(m_sc.shape, NEG, F32)
        l_sc[...] = jnp.zeros(l_sc.shape, F32)
        acc_sc[...] = jnp.zeros(acc_sc.shape, F32)

    def scores(h, s_sc):
        s_sc[...] = _dot_nt(k_ref[h], q_ref[h])

    def sweep(diagonal):
        if diagonal:
            visible = (lax.broadcasted_iota(jnp.int32, (tile, tile), 0)
                       <= lax.broadcasted_iota(jnp.int32, (tile, tile), 1))

        def softmax_pv(h, s_sc):
            s = s_sc[...]
            if diagonal:
                s = jnp.where(visible, s, NEG)
            m_prev = m_sc[h]
            m_new = jnp.maximum(m_prev, jnp.max(s, axis=0, keepdims=True))
            alpha = jnp.exp2(m_prev - m_new)
            p = jnp.exp2(s - m_new)
            l_new = alpha * l_sc[h] + jnp.sum(p, axis=0, keepdims=True)
            acc = alpha * acc_sc[h] + _dot(vt_ref[h], p.astype(BF16))
            if diagonal:
                o_ref[h] = (acc / l_new).astype(o_ref.dtype)
            else:
                m_sc[h] = m_new
                l_sc[h] = l_new
                acc_sc[h] = acc

        scores(0, s0_sc)

        def pair(i, carry):
            h0 = 2 * i
            scores(h0 + 1, s1_sc)
            softmax_pv(h0, s0_sc)
            scores(jnp.minimum(h0 + 2, heads - 1), s0_sc)
            softmax_pv(h0 + 1, s1_sc)
            return carry

        lax.fori_loop(0, heads // 2, pair, 0)

    @pl.when(ki < qi)
    def _():
        sweep(False)

    @pl.when(ki == qi)
    def _():
        sweep(True)


def _flash(q, k, vt, n_seq, seq_len):
    heads = q.shape[0]
    tile = min(ATTN_TILE, seq_len)
    nq = seq_len // tile
    pairs = [(i, j) for i in range(nq) for j in range(i + 1)]
    qi_tab = jnp.array([p[0] for p in pairs], jnp.int32)
    ki_tab = jnp.array([p[1] for p in pairs], jnp.int32)
    grid_spec = pltpu.PrefetchScalarGridSpec(
        num_scalar_prefetch=2,
        grid=(n_seq, len(pairs)),
        in_specs=[pl.BlockSpec((heads, tile, LANE), lambda b, t, qt, kt: (0, b * nq + qt[t], 0)),
                  pl.BlockSpec((heads, tile, LANE), lambda b, t, qt, kt: (0, b * nq + kt[t], 0)),
                  pl.BlockSpec((heads, V_DIM, tile), lambda b, t, qt, kt: (0, 0, b * nq + kt[t]))],
        out_specs=pl.BlockSpec((heads, V_DIM, tile), lambda b, t, qt, kt: (0, 0, b * nq + qt[t])),
        scratch_shapes=[pltpu.VMEM((heads, 1, tile), F32), pltpu.VMEM((heads, 1, tile), F32),
                        pltpu.VMEM((heads, V_DIM, tile), F32), pltpu.VMEM((tile, tile), F32),
                        pltpu.VMEM((tile, tile), F32)],
    )
    return pl.pallas_call(
        functools.partial(_flash_kernel, tile=tile, heads=heads),
        grid_spec=grid_spec,
        out_shape=jax.ShapeDtypeStruct(vt.shape, BF16),
        compiler_params=_params(("arbitrary", "arbitrary")),
        name="flash_attn",
    )(qi_tab, ki_tab, q, k, vt)


def _headmat_kernel(a_ref, w_ref, o_ref):
    o_ref[...] = _dot(a_ref[...], w_ref[0]).astype(o_ref.dtype)


def _headmat(a, w, in_w, out_w):
    n, rows = w.shape[0], a.shape[0]
    return pl.pallas_call(
        _headmat_kernel,
        grid=(n,),
        in_specs=[pl.BlockSpec((rows, in_w), lambda i: (0, i)),
                  pl.BlockSpec((1, in_w, out_w), lambda i: (i, 0, 0))],
        out_specs=pl.BlockSpec((rows, out_w), lambda i: (0, i)),
        out_shape=jax.ShapeDtypeStruct((rows, n * out_w), BF16),
        compiler_params=_params(("arbitrary",)),
        name="head_matmul",
    )(a, w)


def _decode_kernel(pt_ref, qlat_ref, q_ref, ckv_ref, kr_ref, lat_hbm, rope_hbm, o_ref, latbuf, ropebuf, kcat, rcat,
                   sem, *, layer, pages):
    b = pl.program_id(0)
    slot = b % 2

    def fetch(seq, into):
        def issue(i, carry):
            pg = pt_ref[seq * pages + i]
            pltpu.make_async_copy(lat_hbm.at[layer, pg], latbuf.at[into, i], sem.at[into]).start(priority=0)
            pltpu.make_async_copy(rope_hbm.at[layer, pg], ropebuf.at[into, i], sem.at[into]).start(priority=1)
            return carry

        lax.fori_loop(0, pages, issue, 0)

    @pl.when(b == 0)
    def _():
        fetch(0, 0)

    @pl.when(b + 1 < pl.num_programs(0))
    def _():
        fetch(b + 1, 1 - slot)

    pltpu.make_async_copy(lat_hbm.at[layer, pl.ds(0, pages)], latbuf.at[slot], sem.at[slot]).wait()
    pltpu.make_async_copy(rope_hbm.at[layer, pl.ds(0, pages)], ropebuf.at[slot], sem.at[slot]).wait()
    for i in range(pages):
        kcat[i * PAGE_SIZE:(i + 1) * PAGE_SIZE, :] = latbuf[slot, i].astype(BF16)
        rcat[:, i * PAGE_SIZE:(i + 1) * PAGE_SIZE] = ropebuf[slot, i].astype(BF16)
    ql = qlat_ref[0]
    qp = q_ref[0][:, :QK_ROPE]
    keys = kcat[...]
    s = _dot_nt(ql, keys) + _dot(qp, rcat[...])
    kl = ckv_ref[0].astype(BF16).astype(F32)
    kp = kr_ref[0].astype(BF16).astype(F32)
    s1 = (jnp.sum(ql.astype(F32) * kl, axis=1, keepdims=True)
          + jnp.sum(qp.astype(F32) * kp, axis=1, keepdims=True))
    m = jnp.maximum(jnp.max(s, axis=1, keepdims=True), s1)
    p = jnp.exp(s - m)
    p1 = jnp.exp(s1 - m)
    l = jnp.sum(p, axis=1, keepdims=True) + p1
    acc = _dot(p.astype(BF16), keys) + p1.astype(BF16).astype(F32) * kl
    o_ref[0] = (acc / l).astype(o_ref.dtype)


def _decode_attn(layer, page_table, qlat, q, ckv, kr, cache_lat, cache_rope):
    b, heads, rank = qlat.shape
    pages = page_table.shape[1]
    seq = lambda n, w: pl.BlockSpec((1, n, w), lambda b_, pt: (b_, 0, 0))
    hbm = pl.BlockSpec(memory_space=pl.ANY)
    grid_spec = pltpu.PrefetchScalarGridSpec(
        num_scalar_prefetch=1,
        grid=(b,),
        in_specs=[seq(heads, rank), seq(heads, LANE), seq(1, rank), seq(1, QK_ROPE), hbm, hbm],
        out_specs=seq(heads, rank),
        scratch_shapes=[pltpu.VMEM((2, pages, PAGE_SIZE, rank), F32), pltpu.VMEM((2, pages, QK_ROPE, PAGE_SIZE), F32),
                        pltpu.VMEM((pages * PAGE_SIZE, rank), BF16), pltpu.VMEM((QK_ROPE, pages * PAGE_SIZE), BF16),
                        pltpu.SemaphoreType.DMA((2,))],
    )
    return pl.pallas_call(
        functools.partial(_decode_kernel, layer=layer, pages=pages),
        grid_spec=grid_spec,
        out_shape=jax.ShapeDtypeStruct((b, heads, rank), BF16),
        compiler_params=_params(("arbitrary",)),
        name="decode_attn",
    )(page_table.reshape(-1), qlat, q, ckv, kr, cache_lat, cache_rope)


def _proj_res_kernel(a_ref, w_ref, x_ref, gate_ref, o_ref, *, transposed):
    y = _dot_tn(a_ref[...], w_ref[...]) if transposed else _dot(a_ref[...], w_ref[...])
    o_ref[...] = x_ref[...] + gate_ref[0] * y


def _proj_res(a, w, x, gate, rows_per_seq, transposed=False):
    t, d = x.shape
    k = w.shape[0]
    tile = min(FFN_ROW_TILE, t)
    a_spec = pl.BlockSpec((k, tile), lambda i: (0, i)) if transposed else pl.BlockSpec((tile, k), lambda i: (i, 0))
    return pl.pallas_call(
        functools.partial(_proj_res_kernel, transposed=transposed),
        grid=(t // tile,),
        in_specs=[a_spec, pl.BlockSpec((k, d), lambda i: (0, 0)),
                  pl.BlockSpec((tile, d), lambda i: (i, 0)), _mod_specs(t, tile, rows_per_seq, d)],
        out_specs=pl.BlockSpec((tile, d), lambda i: (i, 0)),
        out_shape=jax.ShapeDtypeStruct((t, d), F32),
        compiler_params=_params(("arbitrary",)),
        name="proj_residual",
    )(a, w, x, _mod_arr(gate, rows_per_seq, tile))


def _ffn_kernel(x_ref, g_ref, sh_ref, sc_ref, gate_ref, wg_ref, wu_ref, wd_ref, o_ref, h_sc, acc_sc):
    f = pl.program_id(1)

    @pl.when(f == 0)
    def _():
        h_sc[...] = _modulate(x_ref[...], g_ref[...], sh_ref[0], sc_ref[0]).astype(BF16)
        acc_sc[...] = jnp.zeros(acc_sc.shape, F32)

    h = h_sc[...]
    a = (_silu(_dot(h, wg_ref[...])) * _dot(h, wu_ref[...])).astype(BF16)
    acc_sc[...] += _dot(a, wd_ref[...])

    @pl.when(f == pl.num_programs(1) - 1)
    def _():
        o_ref[...] = x_ref[...] + gate_ref[0] * acc_sc[...]


def _col_tile(ff):
    return FFN_COL_TILE if ff % FFN_COL_TILE == 0 else ff


def _ffn(x, g, shift, scale, gate, wg, wu, wd, rows_per_seq):
    t, d = x.shape
    ff = wg.shape[1]
    tile = min(DENSE_FFN_ROW_TILE, t if rows_per_seq == 1 else rows_per_seq)
    tf = _col_tile(ff)
    mod = _mod_specs(t, tile, rows_per_seq, d)
    row = pl.BlockSpec((tile, d), lambda i, f: (i, 0))
    return pl.pallas_call(
        _ffn_kernel,
        grid=(t // tile, ff // tf),
        in_specs=[row, pl.BlockSpec(g.shape, lambda i, f: (0, 0)), mod, mod, mod,
                  pl.BlockSpec((d, tf), lambda i, f: (0, f)), pl.BlockSpec((d, tf), lambda i, f: (0, f)),
                  pl.BlockSpec((tf, d), lambda i, f: (f, 0))],
        out_specs=row,
        out_shape=jax.ShapeDtypeStruct((t, d), F32),
        scratch_shapes=[pltpu.VMEM((tile, d), BF16), pltpu.VMEM((tile, d), F32)],
        compiler_params=_params(("arbitrary", "arbitrary")),
        name="swiglu",
    )(x, g, *[_mod_arr(m, rows_per_seq, tile) for m in (shift, scale, gate)], wg, wu, wd)


_E1, _E2, _R1, _R2, _W1, _W2 = range(6)


def _router_kernel(x_ref, g_ref, sh_ref, sc_ref, wr_ref, meta_ref, cnt_ref, h3_ref, cnt_sc):
    i = pl.program_id(0)

    @pl.when(i == 0)
    def _():
        cnt_sc[...] = jnp.zeros(cnt_sc.shape, F32)

    h = _modulate(x_ref[...], g_ref[...], sh_ref[0], sc_ref[0])
    for j in range(h3_ref.shape[1]):
        h3_ref[:, j, :] = h[:, j * LANE:(j + 1) * LANE]
    hi = h.astype(BF16)
    lo = (h - hi.astype(F32)).astype(BF16)
    w = wr_ref[...]
    whi = w.astype(BF16)
    wlo = (w - whi.astype(F32)).astype(BF16)
    logits = _dot(hi, whi) + _dot(lo, whi) + _dot(hi, wlo)
    lane = lax.broadcasted_iota(jnp.int32, logits.shape, 1).astype(F32)
    logits = jnp.where(lane < N_EXPERTS, logits, NEG)
    p = jnp.exp(logits - jnp.max(logits, axis=1, keepdims=True))
    p = p / jnp.sum(p, axis=1, keepdims=True)
    p1 = jnp.max(p, axis=1, keepdims=True)
    i1 = jnp.min(jnp.where(p == p1, lane, float(LANE)), axis=1, keepdims=True)
    rest = jnp.where(lane == i1, -1.0, p)
    p2 = jnp.max(rest, axis=1, keepdims=True)
    i2 = jnp.min(jnp.where(rest == p2, lane, float(LANE)), axis=1, keepdims=True)
    den = p1 + p2
    hit1, hit2 = lane == i1, lane == i2
    onehot = jnp.where(hit1 | hit2, 1.0, 0.0)
    tile = onehot.shape[0]
    ltri = jnp.where(lax.broadcasted_iota(jnp.int32, (tile, tile), 0)
                     >= lax.broadcasted_iota(jnp.int32, (tile, tile), 1), 1.0, 0.0).astype(BF16)
    incl = _dot(ltri, onehot.astype(BF16))
    before = cnt_sc[...] + incl - onehot
    r1 = jnp.sum(jnp.where(hit1, before, 0.0), axis=1, keepdims=True)
    r2 = jnp.sum(jnp.where(hit2, before, 0.0), axis=1, keepdims=True)
    cnt = cnt_sc[...] + incl[tile - 1:tile, :]
    cnt_sc[...] = cnt
    cnt_ref[...] = jnp.broadcast_to(cnt, cnt_ref.shape)
    rec = jnp.zeros(logits.shape, F32)
    for ln, val in ((_E1, i1), (_E2, i2), (_R1, r1), (_R2, r2), (_W1, p1 / den), (_W2, p2 / den)):
        rec = jnp.where(lane == ln, val, rec)
    meta_ref[...] = rec


def _router(x, g, shift, scale, w_router, rows_per_seq):
    t, d = x.shape
    tile = min(FFN_ROW_TILE, t)
    wr = jnp.concatenate([w_router, jnp.zeros((d, LANE - w_router.shape[1]), F32)], axis=1)
    mod = _mod_specs(t, tile, rows_per_seq, d)
    return pl.pallas_call(
        _router_kernel,
        grid=(t // tile,),
        in_specs=[pl.BlockSpec((tile, d), lambda i: (i, 0)), pl.BlockSpec(g.shape, lambda i: (0, 0)), mod, mod,
                  pl.BlockSpec((d, LANE), lambda i: (0, 0))],
        out_specs=[pl.BlockSpec((tile, LANE), lambda i: (i, 0)), pl.BlockSpec((SUBLANE, LANE), lambda i: (0, 0)),
                   pl.BlockSpec((tile, d // LANE, LANE), lambda i: (i, 0, 0))],
        out_shape=[jax.ShapeDtypeStruct((t, LANE), F32), jax.ShapeDtypeStruct((SUBLANE, LANE), F32),
                   jax.ShapeDtypeStruct((t, d // LANE, LANE), F32)],
        scratch_shapes=[pltpu.VMEM((1, LANE), F32)],
        compiler_params=_params(("arbitrary",)),
        name="router",
    )(x, g, _mod_arr(shift, rows_per_seq, tile), _mod_arr(scale, rows_per_seq, tile), wr)


def _row_copy(src, dst, sem):
    return pltpu.make_async_copy(src, dst, sem)


def _dispatch_kernel(pos_ref, tail_ref, h_ref, xs_ref, zero_sc, sem, *, tile):
    i = pl.program_id(0)
    base = i * tile * 2

    @pl.when(i == 0)
    def _():
        zero_sc[...] = jnp.zeros(zero_sc.shape, F32)
        for e in range(N_EXPERTS):
            fill = _row_copy(zero_sc, xs_ref.at[pl.ds(tail_ref[e], tile)], sem)
            fill.start()
            fill.wait()

    def issue(t, carry):
        _row_copy(h_ref.at[t], xs_ref.at[pos_ref[base + 2 * t]], sem).start(priority=0)
        _row_copy(h_ref.at[t], xs_ref.at[pos_ref[base + 2 * t + 1]], sem).start(priority=1)
        return carry

    lax.fori_loop(0, tile, issue, 0)
    for _ in range(2):
        _row_copy(h_ref, xs_ref.at[pl.ds(0, tile)], sem).wait()


def _dispatch(pos, tails, h3, n_rows, tile):
    t, chunks, _ = h3.shape
    grid_spec = pltpu.PrefetchScalarGridSpec(
        num_scalar_prefetch=2,
        grid=(t // tile,),
        in_specs=[pl.BlockSpec((tile, chunks, LANE), lambda i, p, tl: (i, 0, 0))],
        out_specs=pl.BlockSpec(memory_space=pl.ANY),
        scratch_shapes=[pltpu.VMEM((tile, chunks, LANE), F32), pltpu.SemaphoreType.DMA(())],
    )
    return pl.pallas_call(
        functools.partial(_dispatch_kernel, tile=tile),
        grid_spec=grid_spec,
        out_shape=jax.ShapeDtypeStruct((n_rows, chunks, LANE), F32),
        compiler_params=_params(("arbitrary",), disable_bounds_checks=True),
        name="moe_dispatch",
    )(pos, tails, h3)


def _moe_ffn_kernel(blk_ref, exp_ref, nt_ref, x_ref, wg_ref, wu_ref, wd_ref, y_ref, h_sc, acc_sc):
    i, f = pl.program_id(0), pl.program_id(1)
    chunks = x_ref.shape[1]

    @pl.when(i < nt_ref[0])
    def _():
        @pl.when(f == 0)
        def _():
            h_sc[...] = jnp.concatenate([x_ref[:, j, :] for j in range(chunks)], axis=1).astype(BF16)
            acc_sc[...] = jnp.zeros(acc_sc.shape, F32)

        h = h_sc[...]
        a = (_silu(_dot(h, wg_ref[0])) * _dot(h, wu_ref[0])).astype(BF16)
        acc_sc[...] += _dot(a, wd_ref[0])

        @pl.when(f == pl.num_programs(1) - 1)
        def _():
            for j in range(chunks):
                y_ref[:, j, :] = acc_sc[:, j * LANE:(j + 1) * LANE]


def _moe_ffn(blk_tab, exp_tab, n_tiles, xs, wg, wu, wd, tile):
    n_rows, chunks, _ = xs.shape
    d = chunks * LANE
    ff = wg.shape[2]
    tf = _col_tile(ff)
    nf = ff // tf
    col = lambda i, f, nt: jnp.where(i < nt[0], f, nf - 1)
    rows = pl.BlockSpec((tile, chunks, LANE), lambda i, f, bt, et, nt: (bt[i], 0, 0))
    grid_spec = pltpu.PrefetchScalarGridSpec(
        num_scalar_prefetch=3,
        grid=(n_rows // tile, nf),
        in_specs=[rows,
                  pl.BlockSpec((1, d, tf), lambda i, f, bt, et, nt: (et[i], 0, col(i, f, nt))),
                  pl.BlockSpec((1, d, tf), lambda i, f, bt, et, nt: (et[i], 0, col(i, f, nt))),
                  pl.BlockSpec((1, tf, d), lambda i, f, bt, et, nt: (et[i], col(i, f, nt), 0))],
        out_specs=rows,
        scratch_shapes=[pltpu.VMEM((tile, d), BF16), pltpu.VMEM((tile, d), F32)],
    )
    return pl.pallas_call(
        _moe_ffn_kernel,
        grid_spec=grid_spec,
        out_shape=jax.ShapeDtypeStruct(xs.shape, F32),
        compiler_params=_params(("arbitrary", "arbitrary")),
        name="moe_swiglu",
    )(blk_tab, exp_tab, n_tiles, xs, wg, wu, wd)


def _combine_kernel(pos_ref, meta_ref, x_ref, gate_ref, *rest, tile, final_norm):
    if final_norm:
        fg_ref, ys_ref, o_ref, buf, sem = rest
    else:
        ys_ref, o_ref, buf, sem = rest
    i = pl.program_id(0)
    slot = i % 2

    def gather(step, into):
        base = step * tile * 2

        def issue(t, carry):
            _row_copy(ys_ref.at[pos_ref[base + 2 * t]], buf.at[into, 0, t], sem.at[into]).start(priority=0)
            _row_copy(ys_ref.at[pos_ref[base + 2 * t + 1]], buf.at[into, 1, t], sem.at[into]).start(priority=1)
            return carry

        lax.fori_loop(0, tile, issue, 0)

    @pl.when(i == 0)
    def _():
        gather(0, 0)

    @pl.when(i + 1 < pl.num_programs(0))
    def _():
        gather(i + 1, 1 - slot)

    for k in range(2):
        _row_copy(ys_ref.at[pl.ds(0, tile)], buf.at[slot, k], sem.at[slot]).wait()
    meta = meta_ref[...]
    w1, w2 = meta[:, _W1:_W1 + 1], meta[:, _W2:_W2 + 1]
    gate = gate_ref[0]
    for j in range(buf.shape[3]):
        sl = slice(j * LANE, (j + 1) * LANE)
        o_ref[:, sl] = x_ref[:, sl] + gate[:, sl] * (w1 * buf[slot, 0, :, j, :] + w2 * buf[slot, 1, :, j, :])
    if final_norm:
        o_ref[...] = _rms(o_ref[...], fg_ref[...])


def _combine(pos, meta, x, gate, ys, rows_per_seq, tile, final_g=None):
    t, d = x.shape
    chunks = d // LANE
    row = lambda n: pl.BlockSpec((tile, n), lambda i, p: (i, 0))
    in_specs = [row(LANE), row(d), _mod_specs(t, tile, rows_per_seq, d)]
    args = [pos, meta, x, _mod_arr(gate, rows_per_seq, tile)]
    if final_g is not None:
        in_specs.append(pl.BlockSpec((1, d), lambda i, p: (0, 0)))
        args.append(final_g.reshape(1, d))
    grid_spec = pltpu.PrefetchScalarGridSpec(
        num_scalar_prefetch=1,
        grid=(t // tile,),
        in_specs=in_specs + [pl.BlockSpec(memory_space=pl.ANY)],
        out_specs=row(d),
        scratch_shapes=[pltpu.VMEM((2, 2, tile, chunks, LANE), F32), pltpu.SemaphoreType.DMA((2,))],
    )
    return pl.pallas_call(
        functools.partial(_combine_kernel, tile=tile, final_norm=final_g is not None),
        grid_spec=grid_spec,
        out_shape=jax.ShapeDtypeStruct((t, d), F32),
        compiler_params=_params(("arbitrary",), disable_bounds_checks=True),
        name="moe_combine",
    )(*args, ys)


def _moe(x, g, shift, scale, gate, w_router, wg, wu, wd, layer, rows_per_seq, final_g=None):
    t, d = x.shape
    tile = min(FFN_ROW_TILE, t)
    meta, counts, h3 = _router(x, g, shift, scale, w_router, rows_per_seq)
    n_max = 2 * t // tile + N_EXPERTS + 1
    cnt = counts[0, :N_EXPERTS].astype(jnp.int32)
    tiles = (cnt + tile - 1) // tile
    ends = jnp.cumsum(tiles)
    n_tiles = ends[-1]
    blk_tab = jnp.minimum(jnp.arange(n_max, dtype=jnp.int32), n_tiles - 1)
    exp_tab = jnp.sum(blk_tab[:, None] >= ends[None, :], axis=1).astype(jnp.int32) + layer * N_EXPERTS
    offs = (ends - tiles) * tile
    experts = meta[:, _E1:_E2 + 1].astype(jnp.int32)
    ranks = meta[:, _R1:_R2 + 1].astype(jnp.int32)
    pos = (offs[experts] + ranks).reshape(-1)
    xs = _dispatch(pos, offs + cnt, h3, n_max * tile, tile)
    ys = _moe_ffn(blk_tab, exp_tab, n_tiles.reshape(1).astype(jnp.int32), xs, wg, wu, wd, tile)
    return _combine(pos, meta, x, gate, ys, rows_per_seq, tile, final_g)


def _softplus(x):
    return jnp.maximum(x, 0.0) + jnp.log1p(jnp.exp(-jnp.abs(x)))


def _ssm_in_kernel(x_ref, g_ref, sh_ref, sc_ref, wz_ref, wx_ref, wdt_ref, cw_ref, cb_ref, dtb_ref,
                   z_ref, xs_ref, b_ref, c_ref, dt_ref, conv_ref, ext_sc, *, tile, tiles_per_seq, d_inner, n_heads):
    i = pl.program_id(0)

    @pl.when(i % tiles_per_seq == 0)
    def _():
        ext_sc[0:SUBLANE, :] = jnp.zeros((SUBLANE, ext_sc.shape[1]), F32)

    h = _modulate(x_ref[...], g_ref[...], sh_ref[0], sc_ref[0]).astype(BF16)
    z_ref[...] = _dot(h, wz_ref[...])
    u = _dot(h, wx_ref[...])
    ext_sc[SUBLANE:SUBLANE + tile, :] = u
    cw = cw_ref[...]
    y = cw[0:1] * ext_sc[SUBLANE - 3:SUBLANE - 3 + tile, :]
    y = y + cw[1:2] * ext_sc[SUBLANE - 2:SUBLANE - 2 + tile, :]
    y = y + cw[2:3] * ext_sc[SUBLANE - 1:SUBLANE - 1 + tile, :]
    y = y + cw[3:4] * u + cb_ref[...]
    tail = ext_sc[tile:tile + SUBLANE, :]
    conv_ref[0] = tail
    ext_sc[0:SUBLANE, :] = tail
    xbc = _silu(y)
    gn = (xbc.shape[1] - d_inner) // 2
    xs_ref[...] = xbc[:, :d_inner]
    b_ref[...] = xbc[:, d_inner:d_inner + gn].astype(BF16)
    c_ref[...] = xbc[:, d_inner + gn:].astype(BF16)
    dt = _softplus(_dot(h, wdt_ref[...]) + dtb_ref[...])
    lane = lax.broadcasted_iota(jnp.int32, dt.shape, 1)
    dt_ref[...] = jnp.where(lane < n_heads, dt, 0.0)


def _ssm_in(x, g, shift, scale, w, rows_per_seq):
    t, d = x.shape
    tile = min(ROW_TILE, rows_per_seq)
    d_inner, conv_dim = w["wz"].shape[1], w["wx"].shape[1]
    gn = (conv_dim - d_inner) // 2
    n_seq = t // rows_per_seq
    tiles_per_seq = rows_per_seq // tile
    full = lambda a: pl.BlockSpec(a.shape, lambda i: (0,) * a.ndim)
    row = lambda n: pl.BlockSpec((tile, n), lambda i: (i, 0))
    mod = _mod_specs(t, tile, rows_per_seq, d)
    ws = [w["wz"], w["wx"], w["wdt"], w["conv_w"], w["conv_b"], w["dt_bias"]]
    return pl.pallas_call(
        functools.partial(_ssm_in_kernel, tile=tile, tiles_per_seq=tiles_per_seq, d_inner=d_inner,
                          n_heads=d_inner // SSM_HEAD_DIM),
        grid=(t // tile,),
        in_specs=[row(d), full(g), mod, mod] + [full(a) for a in ws],
        out_specs=[row(d_inner), row(d_inner), row(gn), row(gn), row(LANE),
                   pl.BlockSpec((1, SUBLANE, conv_dim), lambda i: (i // tiles_per_seq, 0, 0))],
        out_shape=[jax.ShapeDtypeStruct((t, d_inner), F32), jax.ShapeDtypeStruct((t, d_inner), F32),
                   jax.ShapeDtypeStruct((t, gn), BF16), jax.ShapeDtypeStruct((t, gn), BF16),
                   jax.ShapeDtypeStruct((t, LANE), F32), jax.ShapeDtypeStruct((n_seq, SUBLANE, conv_dim), F32)],
        scratch_shapes=[pltpu.VMEM((tile + SUBLANE, conv_dim), F32)],
        compiler_params=_params(("arbitrary",)),
        name="ssm_in_conv",
    )(x, g, _mod_arr(shift, rows_per_seq, tile), _mod_arr(scale, rows_per_seq, tile), *ws)


def _ssd_kernel(xs_ref, b_ref, c_ref, dt_ref, z_ref, alog_ref, e_ref, dskip_ref, ng_ref, y_ref, hout_ref,
                st_sc, y_sc, *, cl, n_groups, pairs_per_group):
    c = pl.program_id(1)

    @pl.when(c == 0)
    def _():
        st_sc[...] = jnp.zeros(st_sc.shape, F32)

    dt = dt_ref[0]
    d_a = dt * (-jnp.exp(alog_ref[...]))
    rowi = lax.broadcasted_iota(jnp.int32, (cl, cl), 0)
    coli = lax.broadcasted_iota(jnp.int32, (cl, cl), 1)
    causal = rowi >= coli
    ltri = jnp.where(causal, 1.0, 0.0).astype(BF16)
    cum = sum(_dot(ltri, part) for part in _split3(d_a))
    cum_t = cum.T
    expand = e_ref[...]
    dt_x = sum(_dot(part, expand) for part in _split3(dt))
    cum_x = sum(_dot(part, expand) for part in _split3(cum))
    xs = xs_ref[0]
    xdt = xs * dt_x
    xdt_b = xdt.astype(BF16)
    w_out = (xdt * jnp.exp(cum_x[cl - 1:cl, :] - cum_x)).astype(BF16)
    grow = jnp.exp(cum_x)
    lane = lax.broadcasted_iota(jnp.int32, (cl, LANE), 1)
    first = lax.broadcasted_iota(jnp.int32, (LANE, 1), 0) < SSM_HEAD_DIM
    for g in range(n_groups):
        bg = b_ref[0, :, g * D_STATE:(g + 1) * D_STATE]
        cg = c_ref[0, :, g * D_STATE:(g + 1) * D_STATE]
        cb = _dot_nt(cg, bg)
        for j in range(pairs_per_group):
            pr = g * pairs_per_group + j
            h0, h1 = 2 * pr, 2 * pr + 1
            sl = slice(pr * LANE, (pr + 1) * LANE)
            seg0 = jnp.exp(jnp.where(causal, cum[:, h0:h0 + 1] - cum_t[h0:h0 + 1, :], -jnp.inf))
            seg1 = jnp.exp(jnp.where(causal, cum[:, h1:h1 + 1] - cum_t[h1:h1 + 1, :], -jnp.inf))
            xp = xdt_b[:, sl]
            y_diag = jnp.where(lane < SSM_HEAD_DIM, _dot((cb * seg0).astype(BF16), xp),
                               _dot((cb * seg1).astype(BF16), xp))
            st = st_sc[pr]
            y_sc[:, sl] = y_diag + _dot_nt(cg, st.astype(BF16)) * grow[:, sl]
            decay = jnp.exp(jnp.where(first, cum_t[h0:h0 + 1, cl - 1:cl], cum_t[h1:h1 + 1, cl - 1:cl]))
            st_sc[pr] = st * decay + _dot_tn(w_out[:, sl], bg)
    zz = z_ref[0]
    y = (y_sc[...] + dskip_ref[...] * xs) * _silu(zz)
    y_ref[0] = _rms(y, ng_ref[...]).astype(y_ref.dtype)

    @pl.when(c == pl.num_programs(1) - 1)
    def _():
        hout_ref[0] = st_sc[...]


def _ssd(xs, bm, cm, dt, z, w, n_seq, seq_len):
    d_inner = xs.shape[1]
    gn = bm.shape[1]
    n_pairs = d_inner // LANE
    cl = min(SSD_CHUNK, seq_len)
    nc = seq_len // cl
    r3 = lambda a: a.reshape(n_seq, seq_len, a.shape[1])
    blk = lambda n: pl.BlockSpec((1, cl, n), lambda b, c: (b, c, 0))
    full = lambda a: pl.BlockSpec(a.shape, lambda b, c: (0,) * a.ndim)
    consts = [w["a_log"], w["expand"], w["d_skip"], w["norm_g"]]
    y, h_out = pl.pallas_call(
        functools.partial(_ssd_kernel, cl=cl, n_groups=SSM_GROUPS, pairs_per_group=n_pairs // SSM_GROUPS),
        grid=(n_seq, nc),
        in_specs=[blk(d_inner), blk(gn), blk(gn), blk(LANE), blk(d_inner)] + [full(a) for a in consts],
        out_specs=[blk(d_inner), pl.BlockSpec((1, n_pairs, LANE, D_STATE), lambda b, c: (b, 0, 0, 0))],
        out_shape=[jax.ShapeDtypeStruct((n_seq, seq_len, d_inner), BF16),
                   jax.ShapeDtypeStruct((n_seq, n_pairs, LANE, D_STATE), F32)],
        scratch_shapes=[pltpu.VMEM((n_pairs, LANE, D_STATE), F32), pltpu.VMEM((cl, d_inner), F32)],
        compiler_params=_params(("arbitrary", "arbitrary")),
        name="ssd_scan",
    )(r3(xs), r3(bm), r3(cm), r3(dt), r3(z), *consts)
    return y.reshape(n_seq * seq_len, d_inner), h_out


def _prep_ssm(w_in, conv_w, conv_b, dt_bias, a_log, d_skip, norm_g, w_out):
    d = w_in.shape[0]
    d_inner = w_out.shape[0]
    n_heads = d_inner // SSM_HEAD_DIM
    conv_dim = conv_w.shape[1]
    pad = LANE - n_heads
    w_dt = w_in[:, d_inner + conv_dim:]
    rep = lambda v: jnp.repeat(v, SSM_HEAD_DIM, axis=-1)
    head_of = jnp.arange(d_inner) // SSM_HEAD_DIM
    return dict(
        wz=w_in[:, :d_inner].astype(BF16),
        wx=w_in[:, d_inner:d_inner + conv_dim].astype(BF16),
        wdt=jnp.concatenate([w_dt, jnp.zeros((d, pad), F32)], axis=1).astype(BF16),
        wdt_x=rep(w_dt).astype(BF16),
        conv_w=conv_w, conv_b=conv_b.reshape(1, conv_dim),
        dt_bias=jnp.concatenate([dt_bias, jnp.zeros((pad,), F32)]).reshape(1, LANE),
        dt_bias_x=rep(dt_bias).reshape(1, d_inner),
        a_log=jnp.concatenate([a_log, jnp.zeros((pad,), F32)]).reshape(1, LANE),
        a_log_x=rep(a_log).reshape(1, d_inner),
        expand=(jnp.arange(LANE)[:, None] == head_of[None, :]).astype(BF16),
        d_skip=rep(d_skip).reshape(1, d_inner),
        norm_g=norm_g.reshape(1, d_inner),
        w_out=w_out.astype(BF16),
    )


def _ssm_step_in_kernel(x_ref, g_ref, sh_ref, sc_ref, wz_ref, wx_ref, wdt_ref, cw_ref, cb_ref, dtb_ref, alog_ref,
                        buf_ref, z_ref, xs_ref, b_ref, c_ref, conv_ref, xdt_t_ref, dec_t_ref, *, d_inner):
    h = _modulate(x_ref[...], g_ref[...], sh_ref[0], sc_ref[0]).astype(BF16)
    z_ref[...] = _dot(h, wz_ref[...])
    u = _dot(h, wx_ref[...])
    cw = cw_ref[...]
    y = cw[0:1] * buf_ref[0] + cw[1:2] * buf_ref[1] + cw[2:3] * buf_ref[2] + cw[3:4] * u + cb_ref[...]
    conv_ref[0] = buf_ref[1]
    conv_ref[1] = buf_ref[2]
    conv_ref[2] = u
    xbc = _silu(y)
    gn = (xbc.shape[1] - d_inner) // 2
    xs = xbc[:, :d_inner]
    xs_ref[...] = xs
    b_ref[...] = xbc[:, d_inner:d_inner + gn]
    c_ref[...] = xbc[:, d_inner + gn:]
    dt = _softplus(_dot(h, wdt_ref[...]) + dtb_ref[...])
    xdt_t_ref[...] = (xs * dt).T
    dec_t_ref[...] = jnp.exp(dt * (-jnp.exp(alog_ref[...]))).T


def _ssm_step_in(x, g, shift, scale, buf_t, w):
    n, d = x.shape
    d_inner, conv_dim = w["wz"].shape[1], w["wx"].shape[1]
    gn = (conv_dim - d_inner) // 2
    ws = [w["wz"], w["wx"], w["wdt_x"], w["conv_w"], w["conv_b"], w["dt_bias_x"], w["a_log_x"], buf_t]
    sds = jax.ShapeDtypeStruct
    return pl.pallas_call(
        functools.partial(_ssm_step_in_kernel, d_inner=d_inner),
        out_shape=[sds((n, d_inner), F32), sds((n, d_inner), F32), sds((n, gn), F32), sds((n, gn), F32),
                   sds((CONV_W - 1, n, conv_dim), F32), sds((d_inner, n), F32), sds((d_inner, n), F32)],
        compiler_params=pltpu.CompilerParams(vmem_limit_bytes=VMEM_LIMIT),
        name="ssm_step_in",
    )(x, g, shift.reshape(1, n, d), scale.reshape(1, n, d), *ws)


def _ssm_step_kernel(st_ref, xdt_t_ref, dec_t_ref, b_ref, c_ref, *rest, n_groups):
    hout_ref, y_t_ref = rest[-2:]
    b = pl.program_id(0)

    @pl.when(b == 0)
    def _():
        y_t_ref[...] = jnp.zeros(y_t_ref.shape, F32)

    rg = st_ref.shape[2] // n_groups
    mine = lax.broadcasted_iota(jnp.int32, (rg, LANE), 1) == b
    for g in range(n_groups):
        rs = slice(g * rg, (g + 1) * rg)
        xcol = jnp.sum(jnp.where(mine, xdt_t_ref[rs, :], 0.0), axis=1, keepdims=True)
        dcol = jnp.sum(jnp.where(mine, dec_t_ref[rs, :], 0.0), axis=1, keepdims=True)
        bg = b_ref[0, :, g * D_STATE:(g + 1) * D_STATE]
        cg = c_ref[0, :, g * D_STATE:(g + 1) * D_STATE]
        s_new = st_ref[0, 0, rs, :] * dcol + xcol * bg
        hout_ref[0, 0, rs, :] = s_new
        ycol = jnp.sum(s_new * cg, axis=1, keepdims=True)
        y_t_ref[rs, :] = jnp.where(mine, ycol, y_t_ref[rs, :])


def _ssm_step(layer, state, xdt_t, dec_t, bm, cm, new_state=None):
    _, n, rows, ns = state.shape
    full = lambda a: pl.BlockSpec(a.shape, lambda b: (0,) * a.ndim)
    seq_row = pl.BlockSpec((1, 1, bm.shape[1]), lambda b: (b, 0, 0))
    slab = pl.BlockSpec((1, 1, rows, ns), lambda b: (layer, b, 0, 0))
    args = [state, xdt_t, dec_t, bm.reshape(n, 1, -1), cm.reshape(n, 1, -1)]
    in_specs = [slab, full(xdt_t), full(dec_t), seq_row, seq_row]
    aliases = {}
    if new_state is not None:
        args.append(new_state)
        in_specs.append(pl.BlockSpec(memory_space=pl.ANY))
        aliases = {len(args) - 1: 0}
    return pl.pallas_call(
        functools.partial(_ssm_step_kernel, n_groups=SSM_GROUPS),
        grid=(n,),
        in_specs=in_specs,
        out_specs=[slab, pl.BlockSpec((rows, n), lambda b: (0, 0))],
        out_shape=[jax.ShapeDtypeStruct(state.shape, F32), jax.ShapeDtypeStruct((rows, n), F32)],
        input_output_aliases=aliases,
        compiler_params=_params(("arbitrary",)),
        name="ssm_step",
    )(*args)


def _ssm_step_out_kernel(y_t_ref, xs_ref, z_ref, dskip_ref, ng_ref, w_ref, x_ref, gate_ref, o_ref):
    y = (y_t_ref[...].T + dskip_ref[...] * xs_ref[...]) * _silu(z_ref[...])
    y = _rms(y, ng_ref[...]).astype(BF16)
    o_ref[...] = x_ref[...] + gate_ref[0] * _dot(y, w_ref[...])


def _ssm_step_out(y_t, xs, z, w, x, gate):
    n, d = x.shape
    return pl.pallas_call(
        _ssm_step_out_kernel,
        out_shape=jax.ShapeDtypeStruct((n, d), F32),
        compiler_params=pltpu.CompilerParams(vmem_limit_bytes=VMEM_LIMIT),
        name="ssm_step_out",
    )(y_t, xs, z, w["d_skip"], w["norm_g"], w["w_out"], x, gate.reshape(1, n, d))


def _final_norm_kernel(x_ref, g_ref, o_ref):
    o_ref[...] = _rms(x_ref[...], g_ref[...])


def _final_norm(x, g):
    t, d = x.shape
    tile = min(FFN_ROW_TILE, t)
    return pl.pallas_call(
        _final_norm_kernel,
        grid=(t // tile,),
        in_specs=[pl.BlockSpec((tile, d), lambda i: (i, 0)), pl.BlockSpec((1, d), lambda i: (0, 0))],
        out_specs=pl.BlockSpec((tile, d), lambda i: (i, 0)),
        out_shape=jax.ShapeDtypeStruct((t, d), F32),
        compiler_params=_params(("arbitrary",)),
        name="final_norm",
    )(x, g.reshape(1, d))


def kernel(x_prompt, x_sample, c_prompt, c_sample, cache_kv_latent, cache_k_rope, page_table, state_ssm, state_conv, w_ada, b_ada, norm_mix_g, norm_ffn_g, norm_final_g, mla_w_in, mla_q_norm_g, mla_kv_norm_g, mla_w_qb, mla_w_kvb, mla_w_o, ssm_w_in, ssm_conv_w, ssm_conv_b, ssm_dt_bias, ssm_a_log, ssm_d, ssm_norm_g, ssm_w_out, ffn_w_gate, ffn_w_up, ffn_w_down, moe_w_router, moe_w_gate, moe_w_up, moe_w_down):
    nb, seq, d = x_prompt.shape
    ns = x_sample.shape[0]
    depth = w_ada.shape[0]
    past_len = page_table.shape[1] * cache_kv_latent.shape[2]
    attn_scale = (QK_NOPE + QK_ROPE) ** -0.5

    xp = x_prompt.reshape(nb * seq, d)
    xs = x_sample.reshape(ns, d)
    mods = _ada(jnp.concatenate([c_prompt, c_sample], axis=0), w_ada, b_ada)
    mods = mods.reshape(depth, nb + ns, 6, d)
    tabs_p = _rope_tables(jnp.arange(seq), attn_scale * LOG2E)
    tabs_s = _rope_tables(jnp.full((ns,), past_len), attn_scale)
    state = state_ssm.reshape(state_ssm.shape[0], ns, -1, D_STATE)
    cache_rope_t = cache_k_rope.transpose(0, 1, 3, 2)
    moe_w = [w.astype(BF16).reshape((-1,) + w.shape[2:]) for w in (moe_w_gate, moe_w_up, moe_w_down)]

    lat_p, rope_p, lat_s, rope_s = [], [], [], []
    ssm_p, conv_p, ssm_s, conv_s = [], [], None, []
    for i in range(depth):
        j = i // 2
        mp = [mods[i, :nb, k] for k in range(6)]
        ms = [mods[i, nb:, k] for k in range(6)]
        g_mix = norm_mix_g[i].reshape(1, d)
        g_ffn = norm_ffn_g[i].reshape(1, d)
        if i % 2 == 0:
            w = _prep_mla(mla_w_in[j], mla_w_qb[j], mla_w_kvb[j])
            qg = mla_q_norm_g[j].reshape(1, -1)
            kvg = mla_kv_norm_g[j].reshape(1, -1)
            w_o = mla_w_o[j].astype(BF16)
            ckv, kr, q, k, vt = _mla_proj(xp, g_mix, mp[0], mp[1], tabs_p, qg, kvg, w, seq, True)
            o_t = _flash(q, k, vt.reshape(MLA_HEADS, V_DIM, nb * seq), nb, seq)
            xp = _proj_res(o_t.reshape(MLA_HEADS * V_DIM, nb * seq), w_o, xp, mp[2], seq, transposed=True)
            lat_p.append(ckv.reshape(nb, seq, -1))
            rope_p.append(kr.reshape(nb, seq, -1))

            ckv, kr, q = _mla_proj(xs, g_mix, ms[0], ms[1], tabs_s, qg, kvg, w, 1, False)
            kv_rank = ckv.shape[1]
            qlat = _headmat(q, w["wabs"], LANE, kv_rank)
            o_lat = _decode_attn(j, page_table, qlat.reshape(ns, MLA_HEADS, kv_rank), q.reshape(ns, MLA_HEADS, LANE),
                                 ckv.reshape(ns, 1, kv_rank), kr.reshape(ns, 1, QK_ROPE), cache_kv_latent,
                                 cache_rope_t)
            o = _headmat(o_lat.reshape(ns, MLA_HEADS * kv_rank), w["wvbd"], 2 * kv_rank, 2 * V_DIM)
            xs = _proj_res(o, w_o, xs, ms[2], 1)
            lat_s.append(ckv.reshape(ns, 1, -1))
            rope_s.append(kr.reshape(ns, 1, -1))
        else:
            w = _prep_ssm(ssm_w_in[j], ssm_conv_w[j], ssm_conv_b[j], ssm_dt_bias[j], ssm_a_log[j], ssm_d[j],
                          ssm_norm_g[j], ssm_w_out[j])
            z, xin, bm, cm, dt, conv = _ssm_in(xp, g_mix, mp[0], mp[1], w, seq)
            y, h_new = _ssd(xin, bm, cm, dt, z, w, nb, seq)
            xp = _proj_res(y, w["w_out"], xp, mp[2], seq)
            ssm_p.append(h_new.reshape(nb, -1, SSM_HEAD_DIM, D_STATE))
            conv_p.append(conv[:, SUBLANE - (CONV_W - 1):])

            buf_t = state_conv[j].transpose(1, 0, 2)
            z, xin, bm, cm, conv, xdt_t, dec_t = _ssm_step_in(xs, g_mix, ms[0], ms[1], buf_t, w)
            ssm_s, y_t = _ssm_step(j, state, xdt_t, dec_t, bm, cm, ssm_s)
            xs = _ssm_step_out(y_t, xin, z, w, xs, ms[2])
            conv_s.append(conv.transpose(1, 0, 2))
        if i % 2 == 0:
            wg = ffn_w_gate[j].astype(BF16)
            wu = ffn_w_up[j].astype(BF16)
            wd = ffn_w_down[j].astype(BF16)
            xp = _ffn(xp, g_ffn, mp[3], mp[4], mp[5], wg, wu, wd, seq)
            xs = _ffn(xs, g_ffn, ms[3], ms[4], ms[5], wg, wu, wd, 1)
        else:
            final_g = norm_final_g if i == depth - 1 else None
            xp = _moe(xp, g_ffn, mp[3], mp[4], mp[5], moe_w_router[j], *moe_w, j, seq, final_g)
            xs = _moe(xs, g_ffn, ms[3], ms[4], ms[5], moe_w_router[j], *moe_w, j, 1, final_g)
    if depth % 2:
        xp, xs = _final_norm(xp, norm_final_g), _final_norm(xs, norm_final_g)
    y_prompt = xp.reshape(nb, seq, d)
    y_sample = xs.reshape(ns, 1, d)
    return (y_prompt, y_sample, jnp.stack(lat_p), jnp.stack(rope_p), jnp.stack(lat_s), jnp.stack(rope_s),
            jnp.stack(ssm_p), jnp.stack(conv_p), ssm_s.reshape(state_ssm.shape), jnp.stack(conv_s))
```
